```python
import math
import jax, jax.numpy as jnp
from jax import lax
import numpy as np

D_MODEL = 2048
BATCH = 4
SEQ = 2048
DEPTH = 4
DEC_BATCH = 128
DEC_SEQ = 4
PAST_LEN = 16384
PAGE_SIZE = 128

N_MIXERS = 2
N_A_LAYERS = (DEPTH + 1) // 2
N_B_LAYERS = DEPTH // 2
A_HEAD_DIM = 128
A_HEADS = D_MODEL // A_HEAD_DIM
A_WIDTH = A_HEADS * A_HEAD_DIM
A_VDIM = D_MODEL // A_HEADS
A_CHUNK = 16
B_HEADS = 8
B_QK = D_MODEL // B_HEADS
B_V = 2 * D_MODEL // B_HEADS
B_CHUNK = 64
ROPE_BASE = 10000.0
N_GROUPS = 4
EXPERTS_PER_GROUP = 8
EXPERT_FF = D_MODEL // 4
TOP_K = 2
EPS = 1e-6

kernel_name = "hgrn2_retention_hmoe_adaln_step"


def rmsnorm(x):
    xf = x.astype(jnp.float32)
    return (xf * lax.rsqrt(jnp.mean(xf * xf, -1, keepdims=True) + EPS)).astype(x.dtype)


def modulate(xn, shift, scale):
    return xn * (1.0 + scale[:, None]) + shift[:, None]


def split_chunks(a, n, c):
    b, _, h, xd = a.shape
    return jnp.moveaxis(a.reshape(b, n, c, h, xd), 1, 0)


def merge_chunks(o):
    n, b, c, h, xd = o.shape
    return jnp.moveaxis(o, 0, 1).reshape(b, n * c, h, xd)


def hgrn_scan(q, k, v, logf, s0):
    t = q.shape[1]
    c = math.gcd(t, A_CHUNK)
    n = t // c
    causal = jnp.tril(jnp.ones((c, c), dtype=bool))[None, :, :, None, None]

    def step(s, inp):
        qc, kc, vc, lc = inp
        cum = jnp.cumsum(lc, axis=1)
        diff = cum[:, :, None] - cum[:, None, :]
        decay = jnp.where(causal, jnp.exp(jnp.where(causal, diff, 0.0)), 0.0)
        att = jnp.einsum('bthk,bshk,btshk->bhts', qc, kc, decay)
        o = (jnp.einsum('bhts,bshv->bthv', att, vc)
             + jnp.einsum('bthk,bhkv->bthv', qc * jnp.exp(cum), s))
        last = cum[:, -1]
        k_dec = kc * jnp.exp(last[:, None] - cum)
        s_new = jnp.exp(last)[..., None] * s + jnp.einsum('bshk,bshv->bhkv', k_dec, vc)
        return s_new, o

    s_fin, o = lax.scan(step, s0, (split_chunks(q, n, c), split_chunks(k, n, c),
                                    split_chunks(v, n, c), split_chunks(logf, n, c)))
    return merge_chunks(o), s_fin


def retention_scan(q, k, v, s0, log_gamma):
    t = q.shape[1]
    c = math.gcd(t, B_CHUNK)
    n = t // c
    idx = jnp.arange(c, dtype=jnp.float32)
    rel = idx[:, None] - idx[None, :]
    intra = jnp.where(rel[None] >= 0,
                      jnp.exp(jnp.maximum(rel, 0.0)[None] * log_gamma[:, None, None]), 0.0)
    inter = jnp.exp((idx + 1.0)[:, None] * log_gamma[None])[None, :, :, None]
    tail = jnp.exp((c - 1.0 - idx)[:, None] * log_gamma[None])[None, :, :, None]
    whole = jnp.exp(c * log_gamma)[None, :, None, None]

    def step(s, inp):
        qc, kc, vc = inp
        att = jnp.einsum('bthd,bshd->bhts', qc, kc) * intra[None]
        o = (jnp.einsum('bhts,bshv->bthv', att, vc)
             + jnp.einsum('bthd,bhdv->bthv', qc * inter, s))
        s_new = whole * s + jnp.einsum('bshd,bshv->bhdv', kc * tail, vc)
        return s_new, o

    s_fin, o = lax.scan(step, s0, (split_chunks(q, n, c), split_chunks(k, n, c),
                                    split_chunks(v, n, c)))
    return merge_chunks(o), s_fin


def rotary(x, pos):
    half = x.shape[-1] // 2
    inv = 1.0 / (ROPE_BASE ** jnp.linspace(0.0, 1.0, half, dtype=jnp.float32))
    ang = pos[:, None] * inv[None]
    cos = jnp.cos(ang)[None, :, None]
    sin = jnp.sin(ang)[None, :, None]
    x1, x2 = x[..., :half], x[..., half:]
    return jnp.concatenate([x1 * cos - x2 * sin, x1 * sin + x2 * cos], axis=-1)


def hgrn_mixer(h, w_in, lb, norm_gain, w_out, s0):
    b, t, _ = h.shape
    proj = jnp.einsum('btd,de->bte', h, w_in).astype(jnp.float32)
    q = jax.nn.silu(proj[..., :A_WIDTH]).reshape(b, t, A_HEADS, A_HEAD_DIM)
    z = proj[..., A_WIDTH:2 * A_WIDTH]
    i = proj[..., 2 * A_WIDTH:3 * A_WIDTH].reshape(b, t, A_HEADS, A_VDIM)
    g = proj[..., 3 * A_WIDTH:]
    lb = lb[None, None]
    f = lb + (1.0 - lb) * jax.nn.sigmoid(z)
    logf = jnp.log(f)
    k = 1.0 - f
    o, s = hgrn_scan(q, k.reshape(b, t, A_HEADS, A_HEAD_DIM), i,
                     logf.reshape(b, t, A_HEADS, A_HEAD_DIM), s0.astype(jnp.float32))
    o = o * lax.rsqrt(jnp.mean(o * o, -1, keepdims=True) + EPS)
    o = o.reshape(b, t, -1) * norm_gain * jax.nn.silu(g)
    y = jnp.einsum('bte,ed->btd', o.astype(h.dtype), w_out)
    return y, s.astype(s0.dtype)


def retention_mixer(h, w_in, norm_gain, w_out, s0, pos, log_gamma):
    b, t, _ = h.shape
    qk = B_HEADS * B_QK
    vw = B_HEADS * B_V
    proj = jnp.einsum('btd,de->bte', h, w_in).astype(jnp.float32)
    q = rotary(proj[..., :qk].reshape(b, t, B_HEADS, B_QK), pos)
    k = rotary(proj[..., qk:2 * qk].reshape(b, t, B_HEADS, B_QK), pos) * (B_QK ** -0.5)
    v = proj[..., 2 * qk:2 * qk + vw].reshape(b, t, B_HEADS, B_V)
    g = proj[..., 2 * qk + vw:]
    o, s = retention_scan(q, k, v, s0.astype(jnp.float32), log_gamma)
    mu = jnp.mean(o, -1, keepdims=True)
    var = jnp.mean(jnp.square(o - mu), -1, keepdims=True)
    o = ((o - mu) * lax.rsqrt(var + EPS)).reshape(b, t, -1) * norm_gain
    o = jax.nn.silu(g) * o
    y = jnp.einsum('bte,ed->btd', o.astype(h.dtype), w_out)
    return y, s.astype(s0.dtype)


def hier_moe(h, w_router_group, w_router_expert, w_gate, w_up, w_down):
    b, t, d = h.shape
    x = h.reshape(-1, d)
    lg = jnp.einsum('nd,dg->ng', x, w_router_group).astype(jnp.float32)
    pg = jax.nn.softmax(lg, axis=-1)
    gi = jnp.argmax(lg, axis=-1)
    wg = jnp.take_along_axis(pg, gi[:, None], axis=-1)
    le = jnp.einsum('nd,gde->nge', x, w_router_expert).astype(jnp.float32)
    le = jnp.take_along_axis(le, gi[:, None, None], axis=1)[:, 0]
    pe = jax.nn.softmax(le, axis=-1)
    top_p, top_i = lax.top_k(pe, TOP_K)
    top_p = top_p / jnp.sum(top_p, -1, keepdims=True)
    within = jnp.sum(jax.nn.one_hot(top_i, EXPERTS_PER_GROUP) * top_p[..., None], axis=1)
    comb = (jax.nn.one_hot(gi, N_GROUPS)[:, :, None] * (wg * within)[:, None, :]).astype(x.dtype)
    y = jnp.zeros_like(x)
    for gr in range(N_GROUPS):
        hg = (jax.nn.silu(jnp.einsum('nd,edf->nef', x, w_gate[gr]))
              * jnp.einsum('nd,edf->nef', x, w_up[gr]))
        y = y + jnp.einsum('nef,efd->nd', hg * comb[:, gr, :, None], w_down[gr])
    return y.reshape(b, t, d)


def run_trunk(x, c, s_hgrn, s_ret, pos, ada_w, ada_b, hgrn_w_in, lb_all, hgrn_norm_gain,
              hgrn_w_out, ret_w_in, ret_norm_gain, ret_w_out, log_gamma, moe_router_group,
              moe_router_expert, moe_w_gate, moe_w_up, moe_w_down, final_norm_gain):
    new_a, new_b = [], []
    cs = jax.nn.silu(c)
    for l in range(DEPTH):
        mod = jnp.einsum('bd,de->be', cs, ada_w[l]) + ada_b[l]
        sh1, sc1, g1, sh2, sc2, g2 = jnp.split(mod, 6, axis=-1)
        hn = modulate(rmsnorm(x), sh1, sc1)
        j = l // N_MIXERS
        if l % N_MIXERS == 0:
            y, s = hgrn_mixer(hn, hgrn_w_in[j], lb_all[j], hgrn_norm_gain[j], hgrn_w_out[j], s_hgrn[j])
            new_a.append(s)
        else:
            y, s = retention_mixer(hn, ret_w_in[j], ret_norm_gain[j], ret_w_out[j], s_ret[j], pos, log_gamma)
            new_b.append(s)
        x = x + g1[:, None] * y
        hn = modulate(rmsnorm(x), sh2, sc2)
        x = x + g2[:, None] * hier_moe(hn, moe_router_group[l], moe_router_expert[l],
                                       moe_w_gate[l], moe_w_up[l], moe_w_down[l])
    out = rmsnorm(x) * final_norm_gain
    return out, jnp.stack(new_a), jnp.stack(new_b)


def setup_inputs(seed: int = 0) -> dict:
    key = jax.random.key(seed)
    ks = jax.random.split(key, 24)
    f32 = jnp.float32
    d = D_MODEL
    nrm = lambda k, shape, s: jax.random.normal(k, shape, f32) * s
    ret_in = 2 * B_HEADS * B_QK + 2 * B_HEADS * B_V
    return {
        "x_prompt": nrm(ks[0], (BATCH, SEQ, d), 1.0),
        "x_sample": nrm(ks[1], (DEC_BATCH, DEC_SEQ, d), 1.0),
        "state_hgrn": nrm(ks[2], (N_A_LAYERS, DEC_BATCH, A_HEADS, A_HEAD_DIM, A_VDIM), 0.5),
        "state_ret": nrm(ks[3], (N_B_LAYERS, DEC_BATCH, B_HEADS, B_QK, B_V), 1.0),
        "c_prompt": nrm(ks[4], (BATCH, d), 1.0),
        "c_sample": nrm(ks[5], (DEC_BATCH, d), 1.0),
        "ada_w": nrm(ks[6], (DEPTH, d, 6 * d), 0.5 * d ** -0.5),
        "ada_b": nrm(ks[7], (DEPTH, 6 * d), 0.01),
        "hgrn_w_in": nrm(ks[8], (N_A_LAYERS, d, 4 * A_WIDTH), d ** -0.5),
        "hgrn_lb_logits": nrm(ks[9], (N_A_LAYERS, A_WIDTH), 0.5),
        "hgrn_norm_gain": 1.0 + nrm(ks[10], (N_A_LAYERS, A_HEADS * A_VDIM), 0.02),
        "hgrn_w_out": nrm(ks[11], (N_A_LAYERS, A_HEADS * A_VDIM, d), (A_HEADS * A_VDIM) ** -0.5),
        "ret_w_in": nrm(ks[12], (N_B_LAYERS, d, ret_in), d ** -0.5),
        "ret_norm_gain": 1.0 + nrm(ks[13], (N_B_LAYERS, B_HEADS * B_V), 0.02),
        "ret_w_out": nrm(ks[14], (N_B_LAYERS, B_HEADS * B_V, d), (B_HEADS * B_V) ** -0.5),
        "moe_router_group": nrm(ks[15], (DEPTH, d, N_GROUPS), d ** -0.5),
        "moe_router_expert": nrm(ks[16], (DEPTH, N_GROUPS, d, EXPERTS_PER_GROUP), d ** -0.5),
        "moe_w_gate": nrm(ks[17], (DEPTH, N_GROUPS, EXPERTS_PER_GROUP, d, EXPERT_FF), d ** -0.5),
        "moe_w_up": nrm(ks[18], (DEPTH, N_GROUPS, EXPERTS_PER_GROUP, d, EXPERT_FF), d ** -0.5),
        "moe_w_down": nrm(ks[19], (DEPTH, N_GROUPS, EXPERTS_PER_GROUP, EXPERT_FF, d), EXPERT_FF ** -0.5),
        "final_norm_gain": 1.0 + nrm(ks[20], (d,), 0.02),
    }


def reference(x_prompt, x_sample, state_hgrn, state_ret, c_prompt, c_sample, ada_w, ada_b,
              hgrn_w_in, hgrn_lb_logits, hgrn_norm_gain, hgrn_w_out, ret_w_in, ret_norm_gain,
              ret_w_out, moe_router_group, moe_router_expert, moe_w_gate, moe_w_up, moe_w_down,
              final_norm_gain):
    p = jax.nn.softmax(hgrn_lb_logits.astype(jnp.float32), axis=0)
    lb_all = jnp.cumsum(p, axis=0) - p[0:1]
    log_gamma = jnp.log1p(-jnp.exp2(-5.0 - jnp.arange(B_HEADS, dtype=jnp.float32)))
    pos_prompt = jnp.arange(SEQ, dtype=jnp.float32)
    pos_sample = jnp.arange(DEC_SEQ, dtype=jnp.float32) + PAST_LEN
    zero_hgrn = jnp.zeros((N_A_LAYERS, BATCH, A_HEADS, A_HEAD_DIM, A_VDIM), x_prompt.dtype)
    zero_ret = jnp.zeros((N_B_LAYERS, BATCH, B_HEADS, B_QK, B_V), x_prompt.dtype)
    y_prompt, state_hgrn_prompt, state_ret_prompt = run_trunk(
        x_prompt, c_prompt, zero_hgrn, zero_ret, pos_prompt, ada_w, ada_b, hgrn_w_in, lb_all,
        hgrn_norm_gain, hgrn_w_out, ret_w_in, ret_norm_gain, ret_w_out, log_gamma,
        moe_router_group, moe_router_expert, moe_w_gate, moe_w_up, moe_w_down, final_norm_gain)
    y_sample, state_hgrn_sample, state_ret_sample = run_trunk(
        x_sample, c_sample, state_hgrn, state_ret, pos_sample, ada_w, ada_b, hgrn_w_in, lb_all,
        hgrn_norm_gain, hgrn_w_out, ret_w_in, ret_norm_gain, ret_w_out, log_gamma,
        moe_router_group, moe_router_expert, moe_w_gate, moe_w_up, moe_w_down, final_norm_gain)
    return (y_prompt, y_sample, state_hgrn_prompt, state_ret_prompt, state_hgrn_sample, state_ret_sample)
```

```python
import functools
import math

import numpy as np
import jax
import jax.numpy as jnp
from jax import lax
from jax.experimental import pallas as pl
from jax.experimental.pallas import tpu as pltpu

F32 = jnp.float32
BF16 = jnp.bfloat16
EPS = 1e-6
PAST_LEN = 16384
ROPE_BASE = 10000.0
HIGHEST = lax.Precision.HIGHEST

V7X_VMEM_LIMIT_BYTES = 56 * 1024 * 1024
LANES = 128
BF16_SUBLANES = 16

HGRN_CHUNK = 128
RET_CHUNK = 256
STEP_PAD = 16
EXPERT_TILE = 256
PROJ_TN = 1024
ADA_TN = 1024


def _cparams(*sem):
    return pltpu.CompilerParams(dimension_semantics=sem, vmem_limit_bytes=V7X_VMEM_LIMIT_BYTES)


def _dot(a, b):
    return jnp.dot(a, b, preferred_element_type=F32)


def _dot_nt(a, b):
    return lax.dot_general(a, b, (((1,), (1,)), ((), ())), preferred_element_type=F32)


def _silu(x):
    return x * jax.nn.sigmoid(x)


def _ada_kernel(c_ref, w_ref, b_ref, o_ref):
    c = c_ref[...]
    o_ref[0] = _dot(_silu(c).astype(BF16), w_ref[0].astype(BF16)) + b_ref[0]


def _ada_mod(c_all, ada_w, ada_b):
    depth, d, n6 = ada_w.shape
    r = c_all.shape[0]
    tn = min(ADA_TN, n6)
    return pl.pallas_call(
        _ada_kernel,
        grid=(depth, n6 // tn),
        in_specs=[pl.BlockSpec((r, d), lambda l, n: (0, 0)),
                  pl.BlockSpec((1, d, tn), lambda l, n: (l, 0, n)),
                  pl.BlockSpec((1, 1, tn), lambda l, n: (l, 0, n))],
        out_specs=pl.BlockSpec((1, r, tn), lambda l, n: (l, 0, n)),
        out_shape=jax.ShapeDtypeStruct((depth, r, n6), F32),
        compiler_params=_cparams("parallel", "parallel"),
        name="ada_mod",
    )(c_all, ada_w, ada_b.reshape(depth, 1, n6))


class _Stream:
    def __init__(self, batch, seq, dec_batch, dec_seq, d):
        self.batch, self.seq, self.dec_batch, self.dec_seq, self.d = batch, seq, dec_batch, dec_seq, d
        self.n_prompt = batch * seq
        self.n_sample = dec_batch * dec_seq
        self.tm = self.n_sample
        assert seq % self.tm == 0 and self.tm % BF16_SUBLANES == 0
        self.tiles_per_seq = seq // self.tm
        self.n_prompt_tiles = self.n_prompt // self.tm
        self.n_tiles = self.n_prompt_tiles + 1
        self.n = self.n_prompt + self.n_sample

    def row_spec(self, width, col_block, tile_axis=0):
        tps, last = self.tiles_per_seq, self.batch - 1

        def imap(*idx):
            return (jnp.minimum(idx[tile_axis] // tps, last), 0, col_block(*idx))
        return pl.BlockSpec((1, 1, width), imap)

    def tok_spec(self, width, col_block):
        return pl.BlockSpec((self.tm, width), lambda *idx: (0, col_block(*idx)))


def _pick(is_sample, tok_ref, row_ref):
    return jnp.where(is_sample, tok_ref[...], row_ref[0])


def _rms(x):
    return x * lax.rsqrt(jnp.mean(x * x, axis=-1, keepdims=True) + EPS)


def _norm_mod_kernel(n_prompt_tiles, x_ref, shr, scr, sht, sct, o_ref):
    is_s = pl.program_id(0) >= n_prompt_tiles
    h = _rms(x_ref[...]) * (1.0 + _pick(is_s, sct, scr)) + _pick(is_s, sht, shr)
    o_ref[...] = h.astype(BF16)


def _norm_mod(st, x, mod_row, mod_tok, j_shift, j_scale):
    d = st.d
    return pl.pallas_call(
        functools.partial(_norm_mod_kernel, st.n_prompt_tiles),
        grid=(st.n_tiles,),
        in_specs=[pl.BlockSpec((st.tm, d), lambda i: (i, 0)),
                  st.row_spec(d, lambda i: j_shift), st.row_spec(d, lambda i: j_scale),
                  st.tok_spec(d, lambda i: j_shift), st.tok_spec(d, lambda i: j_scale)],
        out_specs=pl.BlockSpec((st.tm, d), lambda i: (i, 0)),
        out_shape=jax.ShapeDtypeStruct((st.n, d), BF16),
        compiler_params=_cparams("parallel"),
        name="norm_mod",
    )(x, mod_row, mod_row, mod_tok, mod_tok)


def _norm_mod_router_kernel(n_prompt_tiles, n_groups, n_experts, x_ref, shr, scr, sht, sct, wr_ref,
                            o_ref, eid_ref, ew_ref):
    is_s = pl.program_id(0) >= n_prompt_tiles
    h = _rms(x_ref[...]) * (1.0 + _pick(is_s, sct, scr)) + _pick(is_s, sht, shr)
    o_ref[...] = h.astype(BF16)

    logits = jnp.dot(h, wr_ref[...], precision=HIGHEST, preferred_element_type=F32)
    tm = logits.shape[0]
    lane = lax.broadcasted_iota(jnp.int32, (tm, LANES), 1)
    neg_inf = jnp.float32(-jnp.inf)

    def first_index_of_max(vals, vmax):
        return jnp.min(jnp.where(vals == vmax, lane, LANES), axis=1, keepdims=True)

    is_g = lane < n_groups
    lg = jnp.where(is_g, logits, neg_inf)
    mg = jnp.max(lg, axis=1, keepdims=True)
    eg = jnp.exp(lg - mg)
    pg = eg / jnp.sum(eg, axis=1, keepdims=True)
    gi = first_index_of_max(lg, mg)
    wg = jnp.sum(jnp.where(lane == gi, pg, 0.0), axis=1, keepdims=True)

    lo = n_groups + gi * n_experts
    sel = (lane >= lo) & (lane < lo + n_experts)
    le = jnp.where(sel, logits, neg_inf)
    me = jnp.max(le, axis=1, keepdims=True)
    ee = jnp.exp(le - me)
    pe = jnp.where(sel, ee / jnp.sum(ee, axis=1, keepdims=True), -1.0)
    p1 = jnp.max(pe, axis=1, keepdims=True)
    i1 = first_index_of_max(pe, p1)
    pe2 = jnp.where(lane == i1, -1.0, pe)
    p2 = jnp.max(pe2, axis=1, keepdims=True)
    i2 = first_index_of_max(pe2, p2)
    denom = p1 + p2
    col = lax.broadcasted_iota(jnp.int32, (tm, 2), 1)
    eid_ref[...] = jnp.where(col == 0, i1, i2) - n_groups
    ew_ref[...] = wg * jnp.where(col == 0, p1 / denom, p2 / denom)


def _norm_mod_router(st, x, mod_row, mod_tok, j_shift, j_scale, w_router, n_groups, n_experts):
    d = st.d
    return pl.pallas_call(
        functools.partial(_norm_mod_router_kernel, st.n_prompt_tiles, n_groups, n_experts),
        grid=(st.n_tiles,),
        in_specs=[pl.BlockSpec((st.tm, d), lambda i: (i, 0)),
                  st.row_spec(d, lambda i: j_shift), st.row_spec(d, lambda i: j_scale),
                  st.tok_spec(d, lambda i: j_shift), st.tok_spec(d, lambda i: j_scale),
                  pl.BlockSpec((d, LANES), lambda i: (0, 0))],
        out_specs=[pl.BlockSpec((st.tm, d), lambda i: (i, 0)),
                   pl.BlockSpec((st.tm, 2), lambda i: (i, 0)),
                   pl.BlockSpec((st.tm, 2), lambda i: (i, 0))],
        out_shape=[jax.ShapeDtypeStruct((st.n, d), BF16),
                   jax.ShapeDtypeStruct((st.n, 2), jnp.int32),
                   jax.ShapeDtypeStruct((st.n, 2), F32)],
        compiler_params=_cparams("parallel"),
        name="norm_mod_router",
    )(x, mod_row, mod_row, mod_tok, mod_tok, w_router)


def _proj_kernel(a_ref, w_ref, o_ref, wb_ref):
    @pl.when(pl.program_id(1) == 0)
    def _():
        wb_ref[...] = w_ref[...].astype(BF16)
    o_ref[...] = _dot(a_ref[...], wb_ref[...])


def _proj(st, a, w):
    k, n_out = w.shape
    tn = min(PROJ_TN, n_out)
    return pl.pallas_call(
        _proj_kernel,
        grid=(n_out // tn, st.n_tiles),
        in_specs=[pl.BlockSpec((st.tm, k), lambda n, m: (m, 0)),
                  pl.BlockSpec((k, tn), lambda n, m: (0, n))],
        out_specs=pl.BlockSpec((st.tm, tn), lambda n, m: (m, n)),
        out_shape=jax.ShapeDtypeStruct((st.n, n_out), F32),
        scratch_shapes=[pltpu.VMEM((k, tn), BF16)],
        compiler_params=_cparams("parallel", "arbitrary"),
        name="proj_in",
    )(a, w)


def _proj_res_kernel(n_prompt_tiles, a_ref, w_ref, x_ref, gr, gt, o_ref, wb_ref):
    @pl.when(pl.program_id(1) == 0)
    def _():
        wb_ref[...] = w_ref[...].astype(BF16)
    gate = _pick(pl.program_id(1) >= n_prompt_tiles, gt, gr)
    o_ref[...] = x_ref[...] + gate * _dot(a_ref[...], wb_ref[...])


def _proj_residual(st, a, w, x, mod_row, mod_tok, j_gate):
    k, d = w.shape
    tn = min(PROJ_TN // 2, d)
    nb = d // tn
    return pl.pallas_call(
        functools.partial(_proj_res_kernel, st.n_prompt_tiles),
        grid=(nb, st.n_tiles),
        in_specs=[pl.BlockSpec((st.tm, k), lambda n, m: (m, 0)),
                  pl.BlockSpec((k, tn), lambda n, m: (0, n)),
                  pl.BlockSpec((st.tm, tn), lambda n, m: (m, n)),
                  st.row_spec(tn, lambda n, m: j_gate * nb + n, tile_axis=1),
                  st.tok_spec(tn, lambda n, m: j_gate * nb + n)],
        out_specs=pl.BlockSpec((st.tm, tn), lambda n, m: (m, n)),
        out_shape=jax.ShapeDtypeStruct((st.n, d), F32),
        scratch_shapes=[pltpu.VMEM((k, tn), BF16)],
        compiler_params=_cparams("parallel", "arbitrary"),
        name="proj_out",
    )(a, w, x, mod_row, mod_tok)


def _hgrn_tables(c):
    nbits = int(math.log2(c))
    assert 1 << nbits == c
    r = np.arange(c)[:, None]
    j = np.arange(c)[None, :]
    mats = [(j <= r), (j > r)]
    for b in range(nbits):
        mid = ((r >> (b + 1)) << (b + 1)) + (1 << b)
        right = r >= mid
        mats.append(np.where(right, (j >= mid) & (j <= r), (j > r) & (j < mid)))
    level = np.full((c, c), -1, np.int32)
    level[r[:, 0], r[:, 0]] = 0
    diff = r ^ j
    for b in range(nbits):
        level[(j < r) & ((diff >> b) == 1)] = b + 1
    return np.concatenate(mats, 0).astype(np.float32), level, nbits


def _hgrn_kernel(c, nbits, n_valid, q_ref, z_ref, i_ref, g_ref, lb_ref, gain_ref, ms_ref, lv_ref,
                 s0_ref, o_ref, so_ref, s_scr):
    ci = pl.program_id(2)

    @pl.when(ci == 0)
    def _():
        s_scr[...] = s0_ref[0, 0]

    lb = lb_ref[...]
    f = lb + (1.0 - lb) * jax.nn.sigmoid(z_ref[...])
    logf = jnp.log(f)
    k = 1.0 - f
    if n_valid < c:
        valid = lax.broadcasted_iota(jnp.int32, logf.shape, 0) < n_valid
        logf = jnp.where(valid, logf, 0.0)
        k = jnp.where(valid, k, 0.0)
    q = _silu(q_ref[...])
    v = i_ref[...].astype(BF16)

    lf_hi = logf.astype(BF16)
    lf_lo = (logf - lf_hi.astype(F32)).astype(BF16)
    ms = ms_ref[...]
    e = _dot(ms, lf_hi) + _dot(ms, lf_lo)
    cum = e[0:c]
    suf = e[c:2 * c]

    s = s_scr[...]
    o = _dot((q * jnp.exp(cum)).astype(BF16), s.astype(BF16))
    lv = lv_ref[...]
    att = jnp.where(lv == 0, _dot_nt(q.astype(BF16), k.astype(BF16)), 0.0)
    for b in range(nbits):
        gl = jnp.exp(e[(2 + b) * c:(3 + b) * c])
        att = jnp.where(lv == b + 1, _dot_nt((q * gl).astype(BF16), (k * gl).astype(BF16)), att)
    o = o + _dot(att.astype(BF16), v)

    kd_t = (k * jnp.exp(suf)).T.astype(BF16)
    whole = jnp.exp(jnp.broadcast_to(cum[c - 1:c, :], (s.shape[1], s.shape[0])).T)
    s_new = whole * s + _dot(kd_t, v)
    s_scr[...] = s_new

    @pl.when(ci == pl.num_programs(2) - 1)
    def _():
        so_ref[0, 0] = s_new

    o = o * lax.rsqrt(jnp.mean(o * o, axis=-1, keepdims=True) + EPS)
    o_ref[...] = (o * gain_ref[...] * _silu(g_ref[...])).astype(BF16)


def _hgrn_scan(proj, row0, n_seq, seq, c, n_valid, lb, gain, s0):
    _, heads, kd, vd = s0.shape
    width = heads * kd
    n_chunks = seq // c
    blk0 = row0 // c
    mstack, level, nbits = _hgrn_tables(c)

    def seg(j):
        return pl.BlockSpec((c, kd), lambda b, h, ci: (blk0 + b * n_chunks + ci, j * heads + h))

    head_row = pl.BlockSpec((1, kd), lambda b, h, ci: (0, h))
    state = pl.BlockSpec((1, 1, kd, vd), lambda b, h, ci: (b, h, 0, 0))
    return pl.pallas_call(
        functools.partial(_hgrn_kernel, c, nbits, n_valid),
        grid=(n_seq, heads, n_chunks),
        in_specs=[seg(0), seg(1), seg(2), seg(3), head_row, head_row,
                  pl.BlockSpec(mstack.shape, lambda b, h, ci: (0, 0)),
                  pl.BlockSpec(level.shape, lambda b, h, ci: (0, 0)),
                  state],
        out_specs=[pl.BlockSpec((c, vd), lambda b, h, ci: (b * n_chunks + ci, h)), state],
        out_shape=[jax.ShapeDtypeStruct((n_seq * seq, width), BF16),
                   jax.ShapeDtypeStruct(s0.shape, F32)],
        scratch_shapes=[pltpu.VMEM((kd, vd), F32)],
        compiler_params=_cparams("parallel", "parallel", "arbitrary"),
        name="hgrn_scan",
    )(proj, proj, proj, proj, lb.reshape(1, width), gain.reshape(1, width),
      jnp.asarray(mstack, BF16), jnp.asarray(level), s0)


def _rotate(x, cos, sin):
    half = x.shape[-1] // 2
    x1, x2 = x[:, :half], x[:, half:]
    return jnp.concatenate([x1 * cos - x2 * sin, x1 * sin + x2 * cos], axis=-1)


def _ret_kernel(c, n_valid, lg_ref, q_ref, k_ref, v_ref, g_ref, cos_ref, sin_ref, gain_ref, s0_ref,
                o_ref, so_ref, s_scr):
    ci = pl.program_id(2)

    @pl.when(ci == 0)
    def _():
        s_scr[...] = s0_ref[0, 0]

    lg = lg_ref[pl.program_id(1)]
    cos, sin = cos_ref[...], sin_ref[...]
    q = _rotate(q_ref[...], cos, sin)
    k = _rotate(k_ref[...], cos, sin) * (q.shape[-1] ** -0.5)
    v = v_ref[...].astype(BF16)

    t_col = lax.broadcasted_iota(jnp.int32, (c, 1), 0).astype(F32)
    inter = jnp.exp((t_col + 1.0) * lg)
    tail = jnp.exp(jnp.maximum(n_valid - 1.0 - t_col, 0.0) * lg)
    if n_valid < c:
        tail = jnp.where(t_col < n_valid, tail, 0.0)
    rel = (lax.broadcasted_iota(jnp.int32, (c, c), 0) - lax.broadcasted_iota(jnp.int32, (c, c), 1)).astype(F32)
    intra = jnp.where(rel >= 0.0, jnp.exp(jnp.maximum(rel, 0.0) * lg), 0.0)
    whole = jnp.exp(jnp.full((1, 1), n_valid, F32) * lg)

    s = s_scr[...]
    att = _dot_nt(q.astype(BF16), k.astype(BF16)) * intra
    o = _dot(att.astype(BF16), v) + _dot((q * inter).astype(BF16), s.astype(BF16))
    s_new = whole * s + _dot((k * tail).T.astype(BF16), v)
    s_scr[...] = s_new

    @pl.when(ci == pl.num_programs(2) - 1)
    def _():
        so_ref[0, 0] = s_new

    mu = jnp.mean(o, axis=-1, keepdims=True)
    var = jnp.mean(jnp.square(o - mu), axis=-1, keepdims=True)
    o = (o - mu) * lax.rsqrt(var + EPS) * gain_ref[...]
    o_ref[...] = (_silu(g_ref[...]) * o).astype(BF16)


def _ret_scan(proj, row0, n_seq, seq, c, n_valid, cos, sin, log_gamma, gain, s0):
    _, heads, dk, dv = s0.shape
    n_chunks = seq // c
    blk0 = row0 // c
    kq_blocks = heads
    v_off = 2 * heads * dk // dv
    g_off = v_off + heads

    def qk(j):
        return pl.BlockSpec((c, dk), lambda b, h, ci: (blk0 + b * n_chunks + ci, j * kq_blocks + h))

    def vg(off):
        return pl.BlockSpec((c, dv), lambda b, h, ci: (blk0 + b * n_chunks + ci, off + h))

    rope = pl.BlockSpec((c, dk // 2), lambda b, h, ci: (ci, 0))
    state = pl.BlockSpec((1, 1, dk, dv), lambda b, h, ci: (b, h, 0, 0))
    return pl.pallas_call(
        functools.partial(_ret_kernel, c, n_valid),
        grid=(n_seq, heads, n_chunks),
        in_specs=[pl.BlockSpec(memory_space=pltpu.SMEM),
                  qk(0), qk(1), vg(v_off), vg(g_off), rope, rope,
                  pl.BlockSpec((1, dv), lambda b, h, ci: (0, h)), state],
        out_specs=[pl.BlockSpec((c, dv), lambda b, h, ci: (b * n_chunks + ci, h)), state],
        out_shape=[jax.ShapeDtypeStruct((n_seq * seq, heads * dv), BF16),
                   jax.ShapeDtypeStruct(s0.shape, F32)],
        scratch_shapes=[pltpu.VMEM((dk, dv), F32)],
        compiler_params=_cparams("parallel", "parallel", "arbitrary"),
        name="ret_scan",
    )(log_gamma, proj, proj, proj, proj, cos, sin, gain.reshape(1, heads * dv), s0)


def _rope_tables(pos, half):
    inv = 1.0 / (ROPE_BASE ** jnp.linspace(0.0, 1.0, half, dtype=F32))
    ang = pos[:, None] * inv[None]
    return jnp.cos(ang), jnp.sin(ang)


def _expert_kernel(te_ref, ts_ref, na_ref, x_ref, w_ref, wg_ref, wu_ref, wd_ref, o_ref,
                   wgb, wub, wdb):
    t = pl.program_id(0)
    active = t < na_ref[0]
    new_expert = (t == 0) | (te_ref[t] != te_ref[jnp.maximum(t - 1, 0)])

    @pl.when(active & new_expert)
    def _():
        wgb[...] = wg_ref[0].astype(BF16)
        wub[...] = wu_ref[0].astype(BF16)
        wdb[...] = wd_ref[0].astype(BF16)

    @pl.when(active)
    def _():
        x = x_ref[...]
        hg = _silu(_dot(x, wgb[...])) * _dot(x, wub[...])
        o_ref[...] = _dot((hg * w_ref[...]).astype(BF16), wdb[...])

    @pl.when(jnp.logical_not(active))
    def _():
        o_ref[...] = jnp.zeros_like(o_ref)


def _experts(x_sorted, w_sorted, tile_expert, tile_src, n_active, w_gate, w_up, w_down):
    rows, d = x_sorted.shape
    _, _, ff = w_gate.shape
    tm = EXPERT_TILE
    grid_spec = pltpu.PrefetchScalarGridSpec(
        num_scalar_prefetch=3,
        grid=(rows // tm,),
        in_specs=[pl.BlockSpec((tm, d), lambda t, te, ts, na: (ts[t], 0)),
                  pl.BlockSpec((tm, 1), lambda t, te, ts, na: (ts[t], 0)),
                  pl.BlockSpec((1, d, ff), lambda t, te, ts, na: (te[t], 0, 0)),
                  pl.BlockSpec((1, d, ff), lambda t, te, ts, na: (te[t], 0, 0)),
                  pl.BlockSpec((1, ff, d), lambda t, te, ts, na: (te[t], 0, 0))],
        out_specs=pl.BlockSpec((tm, d), lambda t, te, ts, na: (t, 0)),
        scratch_shapes=[pltpu.VMEM((d, ff), BF16), pltpu.VMEM((d, ff), BF16), pltpu.VMEM((ff, d), BF16)],
    )
    return pl.pallas_call(
        _expert_kernel,
        grid_spec=grid_spec,
        out_shape=jax.ShapeDtypeStruct((rows, d), F32),
        compiler_params=_cparams("arbitrary"),
        name="experts",
    )(tile_expert, tile_src, n_active, x_sorted, w_sorted, w_gate, w_up, w_down)


def _dispatch_plan(eid, ew, n_experts_total):
    n, top_k = eid.shape
    tm = EXPERT_TILE
    a = n * top_k
    rows = (a // tm + n_experts_total) * tm
    n_tiles = rows // tm
    flat_e = eid.reshape(a)
    order = jnp.argsort(flat_e, stable=True).astype(jnp.int32)
    sorted_e = flat_e[order]
    counts = jnp.sum(flat_e[:, None] == jnp.arange(n_experts_total, dtype=jnp.int32)[None], axis=0,
                     dtype=jnp.int32)
    padded = (counts + tm - 1) // tm * tm
    pad_end = jnp.cumsum(padded)
    pad_off = pad_end - padded
    off = jnp.cumsum(counts) - counts
    dest = pad_off[sorted_e] + jnp.arange(a, dtype=jnp.int32) - off[sorted_e]
    row_token = jnp.zeros((rows,), jnp.int32).at[dest].set(order // top_k)
    row_w = jnp.zeros((rows,), F32).at[dest].set(ew.reshape(a)[order])
    pos = jnp.zeros((a,), jnp.int32).at[order].set(dest).reshape(n, top_k)
    n_active = (pad_end[-1] // tm).astype(jnp.int32)
    tile_src = jnp.minimum(jnp.arange(n_tiles, dtype=jnp.int32), n_active - 1)
    tile_expert = jnp.minimum(
        jnp.searchsorted(pad_end, tile_src * tm, side="right").astype(jnp.int32), n_experts_total - 1)
    return row_token, row_w.reshape(rows, 1), pos, tile_expert, tile_src, n_active.reshape(1)


def _combine_kernel(n_prompt_tiles, final, x_ref, ya_ref, yb_ref, gr, gt, fg_ref, o_ref):
    gate = _pick(pl.program_id(0) >= n_prompt_tiles, gt, gr)
    x = x_ref[...] + gate * (ya_ref[...] + yb_ref[...])
    if final:
        x = _rms(x) * fg_ref[...]
    o_ref[...] = x


def _combine(st, x, ya, yb, mod_row, mod_tok, j_gate, final_gain, final):
    d = st.d
    tile = pl.BlockSpec((st.tm, d), lambda i: (i, 0))
    return pl.pallas_call(
        functools.partial(_combine_kernel, st.n_prompt_tiles, final),
        grid=(st.n_tiles,),
        in_specs=[tile, tile, tile, st.row_spec(d, lambda i: j_gate), st.tok_spec(d, lambda i: j_gate),
                  pl.BlockSpec((1, d), lambda i: (0, 0))],
        out_specs=tile,
        out_shape=jax.ShapeDtypeStruct((st.n, d), F32),
        compiler_params=_cparams("parallel"),
        name="moe_combine",
    )(x, ya, yb, mod_row, mod_tok, final_gain.reshape(1, d))


def _pad_steps(a, dec_batch, dec_seq):
    w = a.shape[-1]
    a = a.reshape(dec_batch, dec_seq, w)
    a = jnp.pad(a, ((0, 0), (0, STEP_PAD - dec_seq), (0, 0)))
    return a.reshape(dec_batch * STEP_PAD, w)


def _unpad_steps(a, dec_batch, dec_seq):
    w = a.shape[-1]
    return a.reshape(dec_batch, STEP_PAD, w)[:, :dec_seq].reshape(dec_batch * dec_seq, w)


def kernel(x_prompt, x_sample, state_hgrn, state_ret, c_prompt, c_sample, ada_w, ada_b, hgrn_w_in, hgrn_lb_logits, hgrn_norm_gain, hgrn_w_out, ret_w_in, ret_norm_gain, ret_w_out, moe_router_group, moe_router_expert, moe_w_gate, moe_w_up, moe_w_down, final_norm_gain):
    batch, seq, d = x_prompt.shape
    dec_batch, dec_seq, _ = x_sample.shape
    depth = ada_w.shape[0]
    _, _, a_heads, a_kd, a_vd = state_hgrn.shape
    _, _, b_heads, b_dk, b_dv = state_ret.shape
    _, n_groups, _, n_exp = moe_router_expert.shape
    ff = moe_w_gate.shape[-1]
    n_exp_total = n_groups * n_exp
    st = _Stream(batch, seq, dec_batch, dec_seq, d)

    p = jax.nn.softmax(hgrn_lb_logits.astype(F32), axis=0)
    lb_all = jnp.cumsum(p, axis=0) - p[0:1]
    log_gamma = jnp.log1p(-jnp.exp2(-5.0 - jnp.arange(b_heads, dtype=F32)))
    cos_p, sin_p = _rope_tables(jnp.arange(seq, dtype=F32), b_dk // 2)
    cos_s, sin_s = _rope_tables(jnp.arange(STEP_PAD, dtype=F32) + PAST_LEN, b_dk // 2)
    zero_hgrn = jnp.zeros((batch, a_heads, a_kd, a_vd), F32)
    zero_ret = jnp.zeros((batch, b_heads, b_dk, b_dv), F32)

    n_c = batch + dec_batch
    c_rows = -(-n_c // BF16_SUBLANES) * BF16_SUBLANES
    c_all = jnp.pad(jnp.concatenate([c_prompt, c_sample], 0), ((0, c_rows - n_c), (0, 0)))
    mod = _ada_mod(c_all, ada_w, ada_b)

    x = jnp.concatenate([x_prompt.reshape(-1, d), x_sample.reshape(-1, d)], 0)
    hg_p, hg_s, rt_p, rt_s = [], [], [], []
    for l in range(depth):
        mod_row = mod[l, :batch].reshape(batch, 1, 6 * d)
        mod_tok = jnp.repeat(mod[l, batch:n_c], dec_seq, axis=0)
        j = l // 2
        hn = _norm_mod(st, x, mod_row, mod_tok, 0, 1)
        if l % 2 == 0:
            proj = _proj(st, hn, hgrn_w_in[j])
            proj_s = _pad_steps(proj[st.n_prompt:], dec_batch, dec_seq)
            o_p, s_p = _hgrn_scan(proj, 0, batch, seq, min(HGRN_CHUNK, seq), min(HGRN_CHUNK, seq),
                                  lb_all[j], hgrn_norm_gain[j], zero_hgrn)
            o_s, s_s = _hgrn_scan(proj_s, 0, dec_batch, STEP_PAD, STEP_PAD, dec_seq,
                                  lb_all[j], hgrn_norm_gain[j], state_hgrn[j])
            hg_p.append(s_p)
            hg_s.append(s_s)
            w_out = hgrn_w_out[j]
        else:
            proj = _proj(st, hn, ret_w_in[j])
            proj_s = _pad_steps(proj[st.n_prompt:], dec_batch, dec_seq)
            o_p, s_p = _ret_scan(proj, 0, batch, seq, min(RET_CHUNK, seq), min(RET_CHUNK, seq),
                                 cos_p, sin_p, log_gamma, ret_norm_gain[j], zero_ret)
            o_s, s_s = _ret_scan(proj_s, 0, dec_batch, STEP_PAD, STEP_PAD, dec_seq,
                                 cos_s, sin_s, log_gamma, ret_norm_gain[j], state_ret[j])
            rt_p.append(s_p)
            rt_s.append(s_s)
            w_out = ret_w_out[j]
        o = jnp.concatenate([o_p, _unpad_steps(o_s, dec_batch, dec_seq)], 0)
        x = _proj_residual(st, o, w_out, x, mod_row, mod_tok, 2)

        w_router = jnp.concatenate(
            [moe_router_group[l], jnp.moveaxis(moe_router_expert[l], 0, 1).reshape(d, n_exp_total)], 1)
        w_router = jnp.pad(w_router, ((0, 0), (0, LANES - w_router.shape[1])))
        hn, eid, ew = _norm_mod_router(st, x, mod_row, mod_tok, 3, 4, w_router, n_groups, n_exp)
        row_token, row_w, pos, tile_expert, tile_src, n_active = _dispatch_plan(eid, ew, n_exp_total)
        y_sorted = _experts(jnp.take(hn, row_token, axis=0), row_w, tile_expert, tile_src, n_active,
                            moe_w_gate[l].reshape(n_exp_total, d, ff),
                            moe_w_up[l].reshape(n_exp_total, d, ff),
                            moe_w_down[l].reshape(n_exp_total, ff, d))
        x = _combine(st, x, jnp.take(y_sorted, pos[:, 0], axis=0), jnp.take(y_sorted, pos[:, 1], axis=0),
                     mod_row, mod_tok, 5, final_norm_gain, l == depth - 1)

    y_prompt = x[:st.n_prompt].reshape(batch, seq, d)
    y_sample = x[st.n_prompt:].reshape(dec_batch, dec_seq, d)
    return (y_prompt, y_sample, jnp.stack(hg_p), jnp.stack(rt_p), jnp.stack(hg_s), jnp.stack(rt_s))
```

```python
import functools
import math

import numpy as np
import jax
import jax.numpy as jnp
from jax import lax
from jax.experimental import pallas as pl
from jax.experimental.pallas import tpu as pltpu

F32 = jnp.float32
BF16 = jnp.bfloat16
EPS = 1e-6
PAST_LEN = 16384
ROPE_BASE = 10000.0

V7X_VMEM_LIMIT_BYTES = 56 * 1024 * 1024
LANES = 128
BF16_SUBLANES = 16

HGRN_CHUNK = 128
HGRN_HEADS_PER_STEP = 4
RET_CHUNK = 256
EXPERT_TILE = 256
PROJ_TN = 1024
ADA_TN = 1024


def _cparams(*sem):
    return pltpu.CompilerParams(dimension_semantics=sem, vmem_limit_bytes=V7X_VMEM_LIMIT_BYTES)


def _dot(a, b):
    return jnp.dot(a, b, preferred_element_type=F32)


def _dot_nt(a, b):
    return lax.dot_general(a, b, (((1,), (1,)), ((), ())), preferred_element_type=F32)


def _dot2(m, x):
    hi = x.astype(BF16)
    lo = (x - hi.astype(F32)).astype(BF16)
    return _dot(m, hi) + _dot(m, lo)


def _dot2_lhs(x, m):
    hi = x.astype(BF16)
    lo = (x - hi.astype(F32)).astype(BF16)
    return _dot(hi, m) + _dot(lo, m)


def _silu(x):
    return x * jax.nn.sigmoid(x)


def _rms(x):
    return x * lax.rsqrt(jnp.mean(x * x, axis=-1, keepdims=True) + EPS)


def _ada_kernel(c_ref, w_ref, b_ref, o_ref):
    c = c_ref[...]
    o_ref[0] = _dot(_silu(c).astype(BF16), w_ref[0].astype(BF16)) + b_ref[0]


def _ada_mod(c_all, ada_w, ada_b):
    depth, d, n6 = ada_w.shape
    r = c_all.shape[0]
    tn = min(ADA_TN, n6)
    return pl.pallas_call(
        _ada_kernel,
        grid=(depth, n6 // tn),
        in_specs=[pl.BlockSpec((r, d), lambda l, n: (0, 0)),
                  pl.BlockSpec((1, d, tn), lambda l, n: (l, 0, n)),
                  pl.BlockSpec((1, 1, tn), lambda l, n: (l, 0, n))],
        out_specs=pl.BlockSpec((1, r, tn), lambda l, n: (l, 0, n)),
        out_shape=jax.ShapeDtypeStruct((depth, r, n6), F32),
        compiler_params=_cparams("parallel", "parallel"),
        name="ada_mod",
    )(c_all, ada_w, ada_b.reshape(depth, 1, n6))


class _Stream:
    def __init__(self, batch, seq, dec_batch, dec_seq, d):
        self.batch, self.seq, self.dec_batch, self.dec_seq, self.d = batch, seq, dec_batch, dec_seq, d
        self.n_prompt = batch * seq
        self.n_sample = dec_batch * dec_seq
        self.tm = self.n_sample
        assert seq % self.tm == 0 and self.tm % BF16_SUBLANES == 0
        self.tiles_per_seq = seq // self.tm
        self.n_prompt_tiles = self.n_prompt // self.tm
        self.n_tiles = self.n_prompt_tiles + 1
        self.n = self.n_prompt + self.n_sample

    def mod_specs(self, layer, width, col_block, tile_axis=0):
        tps, last, batch = self.tiles_per_seq, self.batch - 1, self.batch

        def row_map(*idx):
            return (layer * batch + jnp.minimum(idx[tile_axis] // tps, last), 0, col_block(*idx))
        return (pl.BlockSpec((1, 1, width), row_map),
                pl.BlockSpec((1, self.tm, width), lambda *idx: (layer, 0, col_block(*idx))))


def _pick(is_sample, row_ref, tok_ref):
    return jnp.where(is_sample, tok_ref[0], row_ref[0])


def _norm_mod_kernel(n_prompt_tiles, x_ref, shr, sht, scr, sct, o_ref):
    is_s = pl.program_id(0) >= n_prompt_tiles
    h = _rms(x_ref[...]) * (1.0 + _pick(is_s, scr, sct)) + _pick(is_s, shr, sht)
    o_ref[...] = h.astype(BF16)


def _norm_mod(st, x, mod_row, mod_tok, layer, j_shift, j_scale):
    d = st.d
    return pl.pallas_call(
        functools.partial(_norm_mod_kernel, st.n_prompt_tiles),
        grid=(st.n_tiles,),
        in_specs=[pl.BlockSpec((st.tm, d), lambda i: (i, 0)),
                  *st.mod_specs(layer, d, lambda i: j_shift), *st.mod_specs(layer, d, lambda i: j_scale)],
        out_specs=pl.BlockSpec((st.tm, d), lambda i: (i, 0)),
        out_shape=jax.ShapeDtypeStruct((st.n, d), BF16),
        compiler_params=_cparams("parallel"),
        name="norm_mod",
    )(x, mod_row, mod_tok, mod_row, mod_tok)


def _norm_mod_router_kernel(n_prompt_tiles, n_groups, n_experts, x_ref, shr, sht, scr, sct, wr_ref,
                            o_ref, eid_ref, ew_ref):
    is_s = pl.program_id(0) >= n_prompt_tiles
    h = _rms(x_ref[...]) * (1.0 + _pick(is_s, scr, sct)) + _pick(is_s, shr, sht)
    o_ref[...] = h

    logits = _dot(h.astype(BF16), wr_ref[...].astype(BF16))
    tm = logits.shape[0]
    lane = lax.broadcasted_iota(jnp.int32, (tm, LANES), 1)
    neg_inf = jnp.float32(-jnp.inf)

    def first_index_of_max(vals, vmax):
        return jnp.min(jnp.where(vals == vmax, lane, LANES), axis=1, keepdims=True)

    is_g = lane < n_groups
    lg = jnp.where(is_g, logits, neg_inf)
    mg = jnp.max(lg, axis=1, keepdims=True)
    eg = jnp.exp(lg - mg)
    pg = eg / jnp.sum(eg, axis=1, keepdims=True)
    gi = first_index_of_max(lg, mg)
    wg = jnp.sum(jnp.where(lane == gi, pg, 0.0), axis=1, keepdims=True)

    lo = n_groups + gi * n_experts
    sel = (lane >= lo) & (lane < lo + n_experts)
    le = jnp.where(sel, logits, neg_inf)
    me = jnp.max(le, axis=1, keepdims=True)
    ee = jnp.exp(le - me)
    pe = jnp.where(sel, ee / jnp.sum(ee, axis=1, keepdims=True), -1.0)
    p1 = jnp.max(pe, axis=1, keepdims=True)
    i1 = first_index_of_max(pe, p1)
    pe2 = jnp.where(lane == i1, -1.0, pe)
    p2 = jnp.max(pe2, axis=1, keepdims=True)
    i2 = first_index_of_max(pe2, p2)
    denom = p1 + p2
    col = lax.broadcasted_iota(jnp.int32, (tm, 2), 1)
    eid_ref[...] = jnp.where(col == 0, i1, i2) - n_groups
    ew_ref[...] = wg * jnp.where(col == 0, p1 / denom, p2 / denom)


def _norm_mod_router(st, x, mod_row, mod_tok, layer, j_shift, j_scale, w_router, n_groups, n_experts):
    d = st.d
    return pl.pallas_call(
        functools.partial(_norm_mod_router_kernel, st.n_prompt_tiles, n_groups, n_experts),
        grid=(st.n_tiles,),
        in_specs=[pl.BlockSpec((st.tm, d), lambda i: (i, 0)),
                  *st.mod_specs(layer, d, lambda i: j_shift), *st.mod_specs(layer, d, lambda i: j_scale),
                  pl.BlockSpec((d, LANES), lambda i: (0, 0))],
        out_specs=[pl.BlockSpec((st.tm, d), lambda i: (i, 0)),
                   pl.BlockSpec((st.tm, 2), lambda i: (i, 0)),
                   pl.BlockSpec((st.tm, 2), lambda i: (i, 0))],
        out_shape=[jax.ShapeDtypeStruct((st.n, d), F32),
                   jax.ShapeDtypeStruct((st.n, 2), jnp.int32),
                   jax.ShapeDtypeStruct((st.n, 2), F32)],
        compiler_params=_cparams("parallel"),
        name="norm_mod_router",
    )(x, mod_row, mod_tok, mod_row, mod_tok, w_router)


def _proj_kernel(a_ref, w_ref, o_ref, wb_ref):
    @pl.when(pl.program_id(1) == 0)
    def _():
        wb_ref[...] = w_ref[0].astype(BF16)
    o_ref[...] = _dot(a_ref[...], wb_ref[...])


def _proj(st, a, w_all, j):
    _, k, n_out = w_all.shape
    tn = min(PROJ_TN, n_out)
    return pl.pallas_call(
        _proj_kernel,
        grid=(n_out // tn, st.n_tiles),
        in_specs=[pl.BlockSpec((st.tm, k), lambda n, m: (m, 0)),
                  pl.BlockSpec((1, k, tn), lambda n, m: (j, 0, n))],
        out_specs=pl.BlockSpec((st.tm, tn), lambda n, m: (m, n)),
        out_shape=jax.ShapeDtypeStruct((st.n, n_out), F32),
        scratch_shapes=[pltpu.VMEM((k, tn), BF16)],
        compiler_params=_cparams("parallel", "arbitrary"),
        name="proj_in",
    )(a, w_all)


def _proj_res_kernel(n_prompt_tiles, a_ref, w_ref, x_ref, gr, gt, o_ref, wb_ref):
    @pl.when(pl.program_id(1) == 0)
    def _():
        wb_ref[...] = w_ref[0].astype(BF16)
    gate = _pick(pl.program_id(1) >= n_prompt_tiles, gr, gt)
    o_ref[...] = x_ref[...] + gate * _dot(a_ref[...], wb_ref[...])


def _proj_residual(st, a, w_all, j, x, mod_row, mod_tok, layer, j_gate):
    _, k, d = w_all.shape
    tn = min(PROJ_TN // 2, d)
    nb = d // tn
    return pl.pallas_call(
        functools.partial(_proj_res_kernel, st.n_prompt_tiles),
        grid=(nb, st.n_tiles),
        in_specs=[pl.BlockSpec((st.tm, k), lambda n, m: (m, 0)),
                  pl.BlockSpec((1, k, tn), lambda n, m: (j, 0, n)),
                  pl.BlockSpec((st.tm, tn), lambda n, m: (m, n)),
                  *st.mod_specs(layer, tn, lambda n, m: j_gate * nb + n, tile_axis=1)],
        out_specs=pl.BlockSpec((st.tm, tn), lambda n, m: (m, n)),
        out_shape=jax.ShapeDtypeStruct((st.n, d), F32),
        scratch_shapes=[pltpu.VMEM((k, tn), BF16)],
        compiler_params=_cparams("parallel", "arbitrary"),
        name="proj_out",
    )(a, w_all, x, mod_row, mod_tok)


def _hgrn_tables(c):
    nbits = int(math.log2(c))
    assert 1 << nbits == c
    r = np.arange(c)[:, None]
    j = np.arange(c)[None, :]
    mats = [(j <= r), (j > r)]
    for b in range(nbits):
        mid = ((r >> (b + 1)) << (b + 1)) + (1 << b)
        right = r >= mid
        mats.append(np.where(right, (j >= mid) & (j <= r), (j > r) & (j < mid)))
    level = np.full((c, c), -1, np.int32)
    level[r[:, 0], r[:, 0]] = 0
    diff = r ^ j
    for b in range(nbits):
        level[(j < r) & ((diff >> b) == 1)] = b + 1
    return np.concatenate(mats, 0).astype(np.float32), level, nbits


def _hgrn_kernel(c, nbits, hb, kd, q_ref, z_ref, i_ref, g_ref, lb_ref, gain_ref, ms_ref, lv_ref,
                 o_ref, so_ref, s_scr):
    ci = pl.program_id(2)

    @pl.when(ci == 0)
    def _():
        s_scr[...] = jnp.zeros_like(s_scr)

    lb = lb_ref[...]
    f = lb + (1.0 - lb) * jax.nn.sigmoid(z_ref[...])
    logf = jnp.log(f)
    k = 1.0 - f
    q = _silu(q_ref[...])
    v = i_ref[...].astype(BF16)
    e = _dot2(ms_ref[...], logf)
    cum = e[0:c]
    qd = (q * jnp.exp(cum)).astype(BF16)
    kdec = k * jnp.exp(e[c:2 * c])
    whole_row = cum[c - 1:c]
    lv = lv_ref[...]
    levels = [(q.astype(BF16), k.astype(BF16))]
    for b in range(nbits):
        gl = jnp.exp(e[(2 + b) * c:(3 + b) * c])
        levels.append(((q * gl).astype(BF16), (k * gl).astype(BF16)))
    gate = gain_ref[...] * _silu(g_ref[...])

    for h in range(hb):
        sl = slice(h * kd, (h + 1) * kd)
        s = s_scr[h]
        o = _dot(qd[:, sl], s.astype(BF16))
        att = jnp.zeros((c, c), F32)
        for b, (ql, kl) in enumerate(levels):
            att = jnp.where(lv == b, _dot_nt(ql[:, sl], kl[:, sl]), att)
        o = o + _dot(att.astype(BF16), v[:, sl])
        whole = jnp.exp(jnp.broadcast_to(whole_row[:, sl], (s.shape[1], s.shape[0])).T)
        s_scr[h] = whole * s + _dot(kdec[:, sl].T.astype(BF16), v[:, sl])
        o = o * lax.rsqrt(jnp.mean(o * o, axis=-1, keepdims=True) + EPS)
        o_ref[:, sl] = (o * gate[:, sl]).astype(BF16)

    @pl.when(ci == pl.num_programs(2) - 1)
    def _():
        so_ref[0] = s_scr[...]


def _hgrn_scan(st, proj, lb, gain, heads, kd, vd):
    width = heads * kd
    c = min(HGRN_CHUNK, st.seq)
    hb = min(HGRN_HEADS_PER_STEP, heads)
    n_chunks = st.seq // c
    groups = heads // hb
    mstack, level, nbits = _hgrn_tables(c)

    def seg(j):
        return pl.BlockSpec((c, hb * kd), lambda b, h, ci: (b * n_chunks + ci, j * groups + h))

    head_row = pl.BlockSpec((1, hb * kd), lambda b, h, ci: (0, h))
    return pl.pallas_call(
        functools.partial(_hgrn_kernel, c, nbits, hb, kd),
        grid=(st.batch, groups, n_chunks),
        in_specs=[seg(0), seg(1), seg(2), seg(3), head_row, head_row,
                  pl.BlockSpec(mstack.shape, lambda b, h, ci: (0, 0)),
                  pl.BlockSpec(level.shape, lambda b, h, ci: (0, 0))],
        out_specs=[pl.BlockSpec((c, hb * vd), lambda b, h, ci: (b * n_chunks + ci, h)),
                   pl.BlockSpec((1, hb, kd, vd), lambda b, h, ci: (b, h, 0, 0))],
        out_shape=[jax.ShapeDtypeStruct((st.n, heads * vd), BF16),
                   jax.ShapeDtypeStruct((st.batch, heads, kd, vd), F32)],
        scratch_shapes=[pltpu.VMEM((hb, kd, vd), F32)],
        compiler_params=_cparams("parallel", "parallel", "arbitrary"),
        name="hgrn_scan",
    )(proj, proj, proj, proj, lb.reshape(1, width), gain.reshape(1, width),
      jnp.asarray(mstack, BF16), jnp.asarray(level))


def _step_tables(bb, t):
    rows = bb * t
    r = np.arange(rows)
    grp, pos = r // t, r % t
    same = grp[:, None] == grp[None, :]
    i, j = r[:, None], r[None, :]
    sums = [same & (j <= i), same & (j > i), same]
    for d in range(1, t):
        sums.append(same & (j <= i) & (j > i - d) & (pos[:, None] >= d))
    shifts = [same & (j == i - d) for d in range(1, t)]
    return (np.concatenate(sums, 0).astype(np.float32), np.concatenate(shifts, 0).astype(np.float32))


def _hgrn_step_kernel(bb, t, heads, kd, q_ref, z_ref, i_ref, g_ref, lb_ref, gain_ref, ms_ref, sh_ref,
                      seg_ref, segt_ref, s_ref, *rest):
    o_ref, so_ref = rest[-2:]
    rows = bb * t
    lb = lb_ref[...]
    f = lb + (1.0 - lb) * jax.nn.sigmoid(z_ref[...])
    logf = jnp.log(f)
    k = 1.0 - f
    q = _silu(q_ref[...])
    v = i_ref[...]
    e = _dot2(ms_ref[...], logf)
    qd = (q * jnp.exp(e[0:rows])).astype(BF16)
    kdec = k * jnp.exp(e[rows:2 * rows])
    whole = jnp.exp(e[2 * rows:3 * rows])

    seg, segt = seg_ref[...], segt_ref[...]
    o = jnp.zeros_like(q)
    for d in range(t):
        if d == 0:
            ks, vs, w = k, v, q * k
        else:
            sh = sh_ref[(d - 1) * rows:d * rows]
            ks, vs = _dot2(sh, k), _dot2(sh, v)
            w = q * ks * jnp.exp(e[(2 + d) * rows:(3 + d) * rows])
        att = _dot2_lhs(w, seg)
        o = o + _dot2_lhs(att, segt) * vs

    gate = gain_ref[...] * _silu(g_ref[...])
    vb = v.astype(BF16)
    row_seq = lax.broadcasted_iota(jnp.int32, (rows, kd), 0) // t
    lane_seq = lax.broadcasted_iota(jnp.int32, (kd, rows), 1) // t
    for h in range(heads):
        sl = slice(h * kd, (h + 1) * kd)
        k_t = kdec[:, sl].T
        w_t = whole[:, sl].T
        o_h = o[:, sl]
        for b in range(bb):
            s = s_ref[0, b, h]
            o_h = o_h + jnp.where(row_seq == b, _dot(qd[:, sl], s.astype(BF16)), 0.0)
            k_b = jnp.where(lane_seq == b, k_t, 0.0).astype(BF16)
            so_ref[0, b, h] = w_t[:, b * t:b * t + 1] * s + _dot(k_b, vb[:, sl])
        o_h = o_h * lax.rsqrt(jnp.mean(o_h * o_h, axis=-1, keepdims=True) + EPS)
        o_ref[:, sl] = (o_h * gate[:, sl]).astype(BF16)


def _hgrn_step(st, proj, lb, gain, state_all, j, o_stream, so_prev):
    _, _, heads, kd, vd = state_all.shape
    width = heads * kd
    t = st.dec_seq
    bb = BF16_SUBLANES // math.gcd(BF16_SUBLANES, t)
    rows = bb * t
    assert st.dec_batch % bb == 0 and st.n_prompt % rows == 0 and heads <= LANES
    rb0 = st.n_prompt // rows
    sums, shifts = _step_tables(bb, t)
    seg = (np.arange(width)[:, None] // kd == np.arange(LANES)[None, :]).astype(np.float32)

    def col(jj):
        return pl.BlockSpec((rows, width), lambda i: (rb0 + i, jj))

    full = lambda a: pl.BlockSpec(a.shape, lambda i: (0,) * a.ndim)
    state = pl.BlockSpec((1, bb, heads, kd, vd), lambda i: (j, i, 0, 0, 0))
    anyspec = pl.BlockSpec(memory_space=pl.ANY)
    args = [proj, proj, proj, proj, lb.reshape(1, width), gain.reshape(1, width),
            jnp.asarray(sums, BF16), jnp.asarray(shifts, BF16), jnp.asarray(seg, BF16),
            jnp.asarray(seg.T, BF16), state_all, o_stream]
    in_specs = [col(0), col(1), col(2), col(3), pl.BlockSpec((1, width), lambda i: (0, 0)),
                pl.BlockSpec((1, width), lambda i: (0, 0)), full(sums), full(shifts), full(seg),
                full(seg.T), state, anyspec]
    aliases = {11: 0}
    if so_prev is not None:
        args.append(so_prev)
        in_specs.append(anyspec)
        aliases[12] = 1
    return pl.pallas_call(
        functools.partial(_hgrn_step_kernel, bb, t, heads, kd),
        grid=(st.dec_batch // bb,),
        in_specs=in_specs,
        out_specs=[pl.BlockSpec((rows, width), lambda i: (rb0 + i, 0)), state],
        out_shape=[jax.ShapeDtypeStruct(o_stream.shape, BF16),
                   jax.ShapeDtypeStruct(state_all.shape, F32)],
        input_output_aliases=aliases,
        compiler_params=_cparams("parallel"),
        name="hgrn_step",
    )(*args)


def _rotate(x, cos, sin):
    half = x.shape[-1] // 2
    x1, x2 = x[:, :half], x[:, half:]
    return jnp.concatenate([x1 * cos - x2 * sin, x1 * sin + x2 * cos], axis=-1)


def _group_norm_gate(o, gain, g):
    mu = jnp.mean(o, axis=-1, keepdims=True)
    var = jnp.mean(jnp.square(o - mu), axis=-1, keepdims=True)
    return (_silu(g) * ((o - mu) * lax.rsqrt(var + EPS) * gain)).astype(BF16)


def _ret_kernel(c, lg_ref, q_ref, k_ref, v_ref, g_ref, cos_ref, sin_ref, gain_ref, o_ref, so_ref, s_scr):
    ci = pl.program_id(2)

    @pl.when(ci == 0)
    def _():
        s_scr[...] = jnp.zeros_like(s_scr)

    lg = lg_ref[pl.program_id(1)]
    cos, sin = cos_ref[...], sin_ref[...]
    q = _rotate(q_ref[...], cos, sin)
    k = _rotate(k_ref[...], cos, sin) * (q.shape[-1] ** -0.5)
    v = v_ref[...].astype(BF16)

    t_col = lax.broadcasted_iota(jnp.int32, (c, 1), 0).astype(F32)
    inter = jnp.exp((t_col + 1.0) * lg)
    tail = jnp.exp((c - 1.0 - t_col) * lg)
    rel = (lax.broadcasted_iota(jnp.int32, (c, c), 0) - lax.broadcasted_iota(jnp.int32, (c, c), 1)).astype(F32)
    intra = jnp.where(rel >= 0.0, jnp.exp(jnp.maximum(rel, 0.0) * lg), 0.0)
    whole = jnp.exp(jnp.full((1, 1), c, F32) * lg)

    s = s_scr[...]
    att = _dot_nt(q.astype(BF16), k.astype(BF16)) * intra
    o = _dot(att.astype(BF16), v) + _dot((q * inter).astype(BF16), s.astype(BF16))
    s_new = whole * s + _dot((k * tail).T.astype(BF16), v)
    s_scr[...] = s_new

    @pl.when(ci == pl.num_programs(2) - 1)
    def _():
        so_ref[0, 0] = s_new

    o_ref[...] = _group_norm_gate(o, gain_ref[...], g_ref[...])


def _ret_scan(st, proj, cos, sin, log_gamma, gain, heads, dk, dv):
    c = min(RET_CHUNK, st.seq)
    n_chunks = st.seq // c
    v_off = 2 * heads * dk // dv
    g_off = v_off + heads

    def qk(j):
        return pl.BlockSpec((c, dk), lambda b, h, ci: (b * n_chunks + ci, j * heads + h))

    def vg(off):
        return pl.BlockSpec((c, dv), lambda b, h, ci: (b * n_chunks + ci, off + h))

    rope = pl.BlockSpec((c, dk // 2), lambda b, h, ci: (ci, 0))
    return pl.pallas_call(
        functools.partial(_ret_kernel, c),
        grid=(st.batch, heads, n_chunks),
        in_specs=[pl.BlockSpec(memory_space=pltpu.SMEM),
                  qk(0), qk(1), vg(v_off), vg(g_off), rope, rope,
                  pl.BlockSpec((1, dv), lambda b, h, ci: (0, h))],
        out_specs=[pl.BlockSpec((c, dv), lambda b, h, ci: (b * n_chunks + ci, h)),
                   pl.BlockSpec((1, 1, dk, dv), lambda b, h, ci: (b, h, 0, 0))],
        out_shape=[jax.ShapeDtypeStruct((st.n, heads * dv), BF16),
                   jax.ShapeDtypeStruct((st.batch, heads, dk, dv), F32)],
        scratch_shapes=[pltpu.VMEM((dk, dv), F32)],
        compiler_params=_cparams("parallel", "parallel", "arbitrary"),
        name="ret_scan",
    )(log_gamma, proj, proj, proj, proj, cos, sin, gain.reshape(1, heads * dv))


def _ret_step_kernel(bb, t, lg_ref, q_ref, k_ref, v_ref, g_ref, cos_ref, sin_ref, pos_ref, rel_ref,
                     gain_ref, s_ref, *rest):
    o_ref, so_ref = rest[-2:]
    rows = bb * t
    lg = lg_ref[pl.program_id(1)]
    cos, sin = cos_ref[...], sin_ref[...]
    q = _rotate(q_ref[...], cos, sin)
    k = _rotate(k_ref[...], cos, sin) * (q.shape[-1] ** -0.5)
    v = v_ref[...].astype(BF16)

    t_col = pos_ref[...][:, :1]
    inter = jnp.exp((t_col + 1.0) * lg)
    tail = jnp.exp((t - 1.0 - t_col) * lg)
    rel = rel_ref[...]
    intra = jnp.where(rel >= 0.0, jnp.exp(jnp.maximum(rel, 0.0) * lg), 0.0)
    whole = jnp.exp(jnp.full((1, 1), t, F32) * lg)

    att = _dot_nt(q.astype(BF16), k.astype(BF16)) * intra
    o = _dot(att.astype(BF16), v)
    qd = (q * inter).astype(BF16)
    k_t = (k * tail).T
    row_seq = lax.broadcasted_iota(jnp.int32, o.shape, 0) // t
    lane_seq = lax.broadcasted_iota(jnp.int32, k_t.shape, 1) // t
    for b in range(bb):
        s = s_ref[0, b, 0]
        o = o + jnp.where(row_seq == b, _dot(qd, s.astype(BF16)), 0.0)
        k_b = jnp.where(lane_seq == b, k_t, 0.0).astype(BF16)
        so_ref[0, b, 0] = whole * s + _dot(k_b, v)
    o_ref[...] = _group_norm_gate(o, gain_ref[...], g_ref[...])


def _ret_step(st, proj, cos, sin, log_gamma, gain, state_all, j, o_stream, so_prev):
    _, _, heads, dk, dv = state_all.shape
    t = st.dec_seq
    bb = BF16_SUBLANES // math.gcd(BF16_SUBLANES, t)
    rows = bb * t
    assert st.dec_batch % bb == 0 and st.n_prompt % rows == 0
    rb0 = st.n_prompt // rows
    v_off = 2 * heads * dk // dv
    g_off = v_off + heads
    r = np.arange(rows)
    same = (r[:, None] // t) == (r[None, :] // t)
    rel = np.where(same & (r[None, :] <= r[:, None]), r[:, None] - r[None, :], -1).astype(np.float32)
    pos = np.broadcast_to((r % t).astype(np.float32)[:, None], (rows, LANES))

    full = lambda a: pl.BlockSpec(a.shape, lambda i, h: (0,) * a.ndim)
    state = pl.BlockSpec((1, bb, 1, dk, dv), lambda i, h: (j, i, h, 0, 0))
    anyspec = pl.BlockSpec(memory_space=pl.ANY)
    cos_r, sin_r = jnp.tile(cos, (bb, 1)), jnp.tile(sin, (bb, 1))
    args = [log_gamma, proj, proj, proj, proj, cos_r, sin_r, jnp.asarray(pos), jnp.asarray(rel),
            gain.reshape(1, heads * dv), state_all, o_stream]
    in_specs = [pl.BlockSpec(memory_space=pltpu.SMEM),
                pl.BlockSpec((rows, dk), lambda i, h: (rb0 + i, h)),
                pl.BlockSpec((rows, dk), lambda i, h: (rb0 + i, heads + h)),
                pl.BlockSpec((rows, dv), lambda i, h: (rb0 + i, v_off + h)),
                pl.BlockSpec((rows, dv), lambda i, h: (rb0 + i, g_off + h)),
                full(cos_r), full(sin_r), full(pos), full(rel),
                pl.BlockSpec((1, dv), lambda i, h: (0, h)), state, anyspec]
    aliases = {11: 0}
    if so_prev is not None:
        args.append(so_prev)
        in_specs.append(anyspec)
        aliases[12] = 1
    return pl.pallas_call(
        functools.partial(_ret_step_kernel, bb, t),
        grid=(st.dec_batch // bb, heads),
        in_specs=in_specs,
        out_specs=[pl.BlockSpec((rows, dv), lambda i, h: (rb0 + i, h)), state],
        out_shape=[jax.ShapeDtypeStruct(o_stream.shape, BF16),
                   jax.ShapeDtypeStruct(state_all.shape, F32)],
        input_output_aliases=aliases,
        compiler_params=_cparams("parallel", "parallel"),
        name="ret_step",
    )(*args)


def _rope_tables(pos, half):
    inv = 1.0 / (ROPE_BASE ** jnp.linspace(0.0, 1.0, half, dtype=F32))
    ang = pos[:, None] * inv[None]
    return jnp.cos(ang), jnp.sin(ang)


def _expert_kernel(te_ref, ts_ref, na_ref, x_ref, w_ref, wg_ref, wu_ref, wd_ref, o_ref,
                   wgb, wub, wdb):
    t = pl.program_id(0)
    active = t < na_ref[0]
    new_expert = (t == 0) | (te_ref[t] != te_ref[jnp.maximum(t - 1, 0)])

    @pl.when(active & new_expert)
    def _():
        wgb[...] = wg_ref[0].astype(BF16)
        wub[...] = wu_ref[0].astype(BF16)
        wdb[...] = wd_ref[0].astype(BF16)

    @pl.when(active)
    def _():
        x = x_ref[...].astype(BF16)
        hg = _silu(_dot(x, wgb[...])) * _dot(x, wub[...])
        o_ref[...] = _dot((hg * w_ref[...]).astype(BF16), wdb[...])

    @pl.when(jnp.logical_not(active))
    def _():
        o_ref[...] = jnp.zeros_like(o_ref)


def _experts(x_sorted, w_sorted, tile_expert, tile_src, n_active, w_gate, w_up, w_down):
    rows, d = x_sorted.shape
    _, _, ff = w_gate.shape
    tm = EXPERT_TILE
    grid_spec = pltpu.PrefetchScalarGridSpec(
        num_scalar_prefetch=3,
        grid=(rows // tm,),
        in_specs=[pl.BlockSpec((tm, d), lambda t, te, ts, na: (ts[t], 0)),
                  pl.BlockSpec((tm, 1), lambda t, te, ts, na: (ts[t], 0)),
                  pl.BlockSpec((1, d, ff), lambda t, te, ts, na: (te[t], 0, 0)),
                  pl.BlockSpec((1, d, ff), lambda t, te, ts, na: (te[t], 0, 0)),
                  pl.BlockSpec((1, ff, d), lambda t, te, ts, na: (te[t], 0, 0))],
        out_specs=pl.BlockSpec((tm, d), lambda t, te, ts, na: (t, 0)),
        scratch_shapes=[pltpu.VMEM((d, ff), BF16), pltpu.VMEM((d, ff), BF16), pltpu.VMEM((ff, d), BF16)],
    )
    return pl.pallas_call(
        _expert_kernel,
        grid_spec=grid_spec,
        out_shape=jax.ShapeDtypeStruct((rows, d), F32),
        compiler_params=_cparams("arbitrary"),
        name="experts",
    )(tile_expert, tile_src, n_active, x_sorted, w_sorted, w_gate, w_up, w_down)


def _dispatch_plan(eid, ew, n_experts_total):
    n, top_k = eid.shape
    tm = EXPERT_TILE
    a = n * top_k
    n_tiles = a // tm + n_experts_total
    rows = n_tiles * tm
    flat_e = eid.reshape(a)
    order = jnp.argsort(flat_e, stable=True).astype(jnp.int32)
    rank = jnp.argsort(order).astype(jnp.int32)
    counts = jnp.sum(flat_e[:, None] == jnp.arange(n_experts_total, dtype=jnp.int32)[None], axis=0,
                     dtype=jnp.int32)
    padded = (counts + tm - 1) // tm * tm
    pad_end = jnp.cumsum(padded)
    pad_off = pad_end - padded
    off = jnp.cumsum(counts) - counts
    pos = (pad_off[flat_e] + rank - off[flat_e]).reshape(n, top_k)
    n_active = (pad_end[-1] // tm).astype(jnp.int32)
    tile_src = jnp.minimum(jnp.arange(n_tiles, dtype=jnp.int32), n_active - 1)
    tile_expert = jnp.minimum(
        jnp.searchsorted(pad_end, tile_src * tm, side="right").astype(jnp.int32), n_experts_total - 1)
    p = jnp.arange(rows, dtype=jnp.int32)
    e_p = jnp.repeat(tile_expert, tm)
    idx = p - pad_off[e_p]
    valid = (idx < counts[e_p]) & (p < n_active * tm)
    src = order[jnp.clip(off[e_p] + idx, 0, a - 1)]
    row_token = jnp.where(valid, src // top_k, 0)
    row_w = jnp.where(valid, ew.reshape(a)[src], 0.0)
    return row_token, row_w.reshape(rows, 1), pos, tile_expert, tile_src, n_active.reshape(1)


def _take_rows(a, idx):
    return a.at[idx].get(mode="promise_in_bounds")


def _combine_kernel(n_prompt_tiles, final, x_ref, ya_ref, yb_ref, gr, gt, fg_ref, o_ref):
    gate = _pick(pl.program_id(0) >= n_prompt_tiles, gr, gt)
    x = x_ref[...] + gate * (ya_ref[...] + yb_ref[...])
    if final:
        x = _rms(x) * fg_ref[...]
    o_ref[...] = x


def _combine(st, x, ya, yb, mod_row, mod_tok, layer, j_gate, final_gain, final):
    d = st.d
    tile = pl.BlockSpec((st.tm, d), lambda i: (i, 0))
    return pl.pallas_call(
        functools.partial(_combine_kernel, st.n_prompt_tiles, final),
        grid=(st.n_tiles,),
        in_specs=[tile, tile, tile, *st.mod_specs(layer, d, lambda i: j_gate),
                  pl.BlockSpec((1, d), lambda i: (0, 0))],
        out_specs=tile,
        out_shape=jax.ShapeDtypeStruct((st.n, d), F32),
        compiler_params=_cparams("parallel"),
        name="moe_combine",
    )(x, ya, yb, mod_row, mod_tok, final_gain.reshape(1, d))


def kernel(x_prompt, x_sample, state_hgrn, state_ret, c_prompt, c_sample, ada_w, ada_b, hgrn_w_in, hgrn_lb_logits, hgrn_norm_gain, hgrn_w_out, ret_w_in, ret_norm_gain, ret_w_out, moe_router_group, moe_router_expert, moe_w_gate, moe_w_up, moe_w_down, final_norm_gain):
    batch, seq, d = x_prompt.shape
    dec_batch, dec_seq, _ = x_sample.shape
    depth = ada_w.shape[0]
    _, _, a_heads, a_kd, a_vd = state_hgrn.shape
    _, _, b_heads, b_dk, b_dv = state_ret.shape
    _, n_groups, _, n_exp = moe_router_expert.shape
    ff = moe_w_gate.shape[-1]
    n_exp_total = n_groups * n_exp
    st = _Stream(batch, seq, dec_batch, dec_seq, d)

    p = jax.nn.softmax(hgrn_lb_logits.astype(F32), axis=0)
    lb_all = jnp.cumsum(p, axis=0) - p[0:1]
    log_gamma = jnp.log1p(-jnp.exp2(-5.0 - jnp.arange(b_heads, dtype=F32)))
    cos_p, sin_p = _rope_tables(jnp.arange(seq, dtype=F32), b_dk // 2)
    cos_s, sin_s = _rope_tables(jnp.arange(dec_seq, dtype=F32) + PAST_LEN, b_dk // 2)
    w_gate = moe_w_gate.reshape(depth * n_exp_total, d, ff)
    w_up = moe_w_up.reshape(depth * n_exp_total, d, ff)
    w_down = moe_w_down.reshape(depth * n_exp_total, ff, d)

    n_c = batch + dec_batch
    c_rows = -(-n_c // BF16_SUBLANES) * BF16_SUBLANES
    c_all = jnp.pad(jnp.concatenate([c_prompt, c_sample], 0), ((0, c_rows - n_c), (0, 0)))
    mod = _ada_mod(c_all, ada_w, ada_b)
    mod_row = mod[:, :batch].reshape(depth * batch, 1, 6 * d)
    mod_tok = jnp.repeat(mod[:, batch:n_c], dec_seq, axis=1)

    x = jnp.concatenate([x_prompt.reshape(-1, d), x_sample.reshape(-1, d)], 0)
    hg_p, rt_p, hg_s, rt_s = [], [], None, None
    for l in range(depth):
        j = l // 2
        hn = _norm_mod(st, x, mod_row, mod_tok, l, 0, 1)
        if l % 2 == 0:
            proj = _proj(st, hn, hgrn_w_in, j)
            o, s_p = _hgrn_scan(st, proj, lb_all[j], hgrn_norm_gain[j], a_heads, a_kd, a_vd)
            o, hg_s = _hgrn_step(st, proj, lb_all[j], hgrn_norm_gain[j], state_hgrn, j, o, hg_s)
            hg_p.append(s_p)
            w_out = hgrn_w_out
        else:
            proj = _proj(st, hn, ret_w_in, j)
            o, s_p = _ret_scan(st, proj, cos_p, sin_p, log_gamma, ret_norm_gain[j], b_heads, b_dk, b_dv)
            o, rt_s = _ret_step(st, proj, cos_s, sin_s, log_gamma, ret_norm_gain[j], state_ret, j, o, rt_s)
            rt_p.append(s_p)
            w_out = ret_w_out
        x = _proj_residual(st, o, w_out, j, x, mod_row, mod_tok, l, 2)

        w_router = jnp.concatenate(
            [moe_router_group[l], jnp.moveaxis(moe_router_expert[l], 0, 1).reshape(d, n_exp_total)], 1)
        w_router = jnp.pad(w_router, ((0, 0), (0, LANES - w_router.shape[1])))
        hn, eid, ew = _norm_mod_router(st, x, mod_row, mod_tok, l, 3, 4, w_router, n_groups, n_exp)
        row_token, row_w, pos, tile_expert, tile_src, n_active = _dispatch_plan(eid, ew, n_exp_total)
        y_sorted = _experts(_take_rows(hn, row_token), row_w, tile_expert + l * n_exp_total, tile_src,
                            n_active, w_gate, w_up, w_down)
        x = _combine(st, x, _take_rows(y_sorted, pos[:, 0]), _take_rows(y_sorted, pos[:, 1]),
                     mod_row, mod_tok, l, 5, final_norm_gain, l == depth - 1)

    y_prompt = x[:st.n_prompt].reshape(batch, seq, d)
    y_sample = x[st.n_prompt:].reshape(dec_batch, dec_seq, d)
    return (y_prompt, y_sample, jnp.stack(hg_p), jnp.stack(rt_p), hg_s, rt_s)
```

```python
import functools
import math

import numpy as np
import jax
import jax.numpy as jnp
from jax import lax
from jax.experimental import pallas as pl
from jax.experimental.pallas import tpu as pltpu

F32 = jnp.float32
BF16 = jnp.bfloat16
EPS = 1e-6
PAST_LEN = 16384
ROPE_BASE = 10000.0

V7X_VMEM_LIMIT_BYTES = 56 * 1024 * 1024
LANES = 128
BF16_SUBLANES = 16

HGRN_CHUNK = 128
HGRN_HEADS_PER_STEP = 8
RET_CHUNK = 256
EXPERT_TILE = 256
PROJ_TN = 1024
PROJ_IN_ROW_TILES = 8
ADA_TN = 1024


def _cparams(*sem):
    return pltpu.CompilerParams(dimension_semantics=sem, vmem_limit_bytes=V7X_VMEM_LIMIT_BYTES)


def _dot(a, b):
    return jnp.dot(a, b, preferred_element_type=F32)


def _dot_nt(a, b):
    return lax.dot_general(a, b, (((1,), (1,)), ((), ())), preferred_element_type=F32)


def _dot2(m, x):
    hi = x.astype(BF16)
    lo = (x - hi.astype(F32)).astype(BF16)
    return _dot(m, hi) + _dot(m, lo)


def _dot2_wide(mm, x):
    hi = x.astype(BF16)
    lo = (x - hi.astype(F32)).astype(BF16)
    return _dot(mm, jnp.concatenate([hi, lo], axis=0))


def _dot2_lhs(x, m):
    hi = x.astype(BF16)
    lo = (x - hi.astype(F32)).astype(BF16)
    return _dot(hi, m) + _dot(lo, m)


def _silu(x):
    return x * jax.nn.sigmoid(x)


def _rms(x):
    return x * lax.rsqrt(jnp.mean(x * x, axis=-1, keepdims=True) + EPS)


def _ada_kernel(c_ref, w_ref, b_ref, o_ref):
    c = c_ref[...]
    o_ref[0] = _dot(_silu(c).astype(BF16), w_ref[0].astype(BF16)) + b_ref[0]


def _ada_mod(c_all, ada_w, ada_b):
    depth, d, n6 = ada_w.shape
    r = c_all.shape[0]
    tn = min(ADA_TN, n6)
    return pl.pallas_call(
        _ada_kernel,
        grid=(depth, n6 // tn),
        in_specs=[pl.BlockSpec((r, d), lambda l, n: (0, 0)),
                  pl.BlockSpec((1, d, tn), lambda l, n: (l, 0, n)),
                  pl.BlockSpec((1, 1, tn), lambda l, n: (l, 0, n))],
        out_specs=pl.BlockSpec((1, r, tn), lambda l, n: (l, 0, n)),
        out_shape=jax.ShapeDtypeStruct((depth, r, n6), F32),
        compiler_params=_cparams("parallel", "parallel"),
        name="ada_mod",
    )(c_all, ada_w, ada_b.reshape(depth, 1, n6))


class _Stream:
    def __init__(self, batch, seq, dec_batch, dec_seq, d):
        self.batch, self.seq, self.dec_batch, self.dec_seq, self.d = batch, seq, dec_batch, dec_seq, d
        self.n_prompt = batch * seq
        self.n_sample = dec_batch * dec_seq
        self.tm = self.n_sample
        assert seq % self.tm == 0 and self.tm % BF16_SUBLANES == 0
        self.tiles_per_seq = seq // self.tm
        self.n_prompt_tiles = self.n_prompt // self.tm
        self.n_tiles = self.n_prompt_tiles + 1
        self.n = self.n_prompt + self.n_sample

    def mod_specs(self, layer, width, col_block, tile_axis=0):
        tps, last, batch = self.tiles_per_seq, self.batch - 1, self.batch

        def row_map(*idx):
            return (layer * batch + jnp.minimum(idx[tile_axis] // tps, last), 0, col_block(*idx))
        return (pl.BlockSpec((1, 1, width), row_map),
                pl.BlockSpec((1, self.tm, width), lambda *idx: (layer, 0, col_block(*idx))))


def _pick(is_sample, row_ref, tok_ref):
    return jnp.where(is_sample, tok_ref[0], row_ref[0])


def _norm_mod_kernel(n_prompt_tiles, x_ref, shr, sht, scr, sct, o_ref):
    is_s = pl.program_id(0) >= n_prompt_tiles
    h = _rms(x_ref[...]) * (1.0 + _pick(is_s, scr, sct)) + _pick(is_s, shr, sht)
    o_ref[...] = h.astype(BF16)


def _norm_mod(st, x, mod_row, mod_tok, layer, j_shift, j_scale):
    d = st.d
    return pl.pallas_call(
        functools.partial(_norm_mod_kernel, st.n_prompt_tiles),
        grid=(st.n_tiles,),
        in_specs=[pl.BlockSpec((st.tm, d), lambda i: (i, 0)),
                  *st.mod_specs(layer, d, lambda i: j_shift), *st.mod_specs(layer, d, lambda i: j_scale)],
        out_specs=pl.BlockSpec((st.tm, d), lambda i: (i, 0)),
        out_shape=jax.ShapeDtypeStruct((st.n, d), BF16),
        compiler_params=_cparams("parallel"),
        name="norm_mod",
    )(x, mod_row, mod_tok, mod_row, mod_tok)


def _norm_mod_router_kernel(n_prompt_tiles, n_groups, n_experts, x_ref, shr, sht, scr, sct, wr_ref,
                            o_ref, eid_ref, ew_ref):
    is_s = pl.program_id(0) >= n_prompt_tiles
    h = _rms(x_ref[...]) * (1.0 + _pick(is_s, scr, sct)) + _pick(is_s, shr, sht)
    o_ref[...] = h

    logits = _dot(h.astype(BF16), wr_ref[...].astype(BF16))
    tm = logits.shape[0]
    lane = lax.broadcasted_iota(jnp.int32, (tm, LANES), 1)
    neg_inf = jnp.float32(-jnp.inf)

    def first_index_of_max(vals, vmax):
        return jnp.min(jnp.where(vals == vmax, lane, LANES), axis=1, keepdims=True)

    is_g = lane < n_groups
    lg = jnp.where(is_g, logits, neg_inf)
    mg = jnp.max(lg, axis=1, keepdims=True)
    eg = jnp.exp(lg - mg)
    pg = eg / jnp.sum(eg, axis=1, keepdims=True)
    gi = first_index_of_max(lg, mg)
    wg = jnp.sum(jnp.where(lane == gi, pg, 0.0), axis=1, keepdims=True)

    lo = n_groups + gi * n_experts
    sel = (lane >= lo) & (lane < lo + n_experts)
    le = jnp.where(sel, logits, neg_inf)
    me = jnp.max(le, axis=1, keepdims=True)
    ee = jnp.exp(le - me)
    pe = jnp.where(sel, ee / jnp.sum(ee, axis=1, keepdims=True), -1.0)
    p1 = jnp.max(pe, axis=1, keepdims=True)
    i1 = first_index_of_max(pe, p1)
    pe2 = jnp.where(lane == i1, -1.0, pe)
    p2 = jnp.max(pe2, axis=1, keepdims=True)
    i2 = first_index_of_max(pe2, p2)
    denom = p1 + p2
    col = lax.broadcasted_iota(jnp.int32, (tm, 2), 1)
    eid_ref[...] = jnp.where(col == 0, i1, i2) - n_groups
    ew_ref[...] = wg * jnp.where(col == 0, p1 / denom, p2 / denom)


def _norm_mod_router(st, x, mod_row, mod_tok, layer, j_shift, j_scale, w_router, n_groups, n_experts):
    d = st.d
    return pl.pallas_call(
        functools.partial(_norm_mod_router_kernel, st.n_prompt_tiles, n_groups, n_experts),
        grid=(st.n_tiles,),
        in_specs=[pl.BlockSpec((st.tm, d), lambda i: (i, 0)),
                  *st.mod_specs(layer, d, lambda i: j_shift), *st.mod_specs(layer, d, lambda i: j_scale),
                  pl.BlockSpec((d, LANES), lambda i: (0, 0))],
        out_specs=[pl.BlockSpec((st.tm, d), lambda i: (i, 0)),
                   pl.BlockSpec((st.tm, 2), lambda i: (i, 0)),
                   pl.BlockSpec((st.tm, 2), lambda i: (i, 0))],
        out_shape=[jax.ShapeDtypeStruct((st.n, d), F32),
                   jax.ShapeDtypeStruct((st.n, 2), jnp.int32),
                   jax.ShapeDtypeStruct((st.n, 2), F32)],
        compiler_params=_cparams("parallel"),
        name="norm_mod_router",
    )(x, mod_row, mod_tok, mod_row, mod_tok, w_router)


def _proj_kernel(a_ref, w_ref, o_ref, wb_ref):
    @pl.when(pl.program_id(1) == 0)
    def _():
        wb_ref[...] = w_ref[0].astype(BF16)
    o_ref[...] = _dot(a_ref[...], wb_ref[...])


def _proj(st, a, w_all, j):
    _, k, n_out = w_all.shape
    tn = min(PROJ_TN, n_out)
    tm = st.n // PROJ_IN_ROW_TILES if st.n % (PROJ_IN_ROW_TILES * BF16_SUBLANES) == 0 else st.tm
    return pl.pallas_call(
        _proj_kernel,
        grid=(n_out // tn, st.n // tm),
        in_specs=[pl.BlockSpec((tm, k), lambda n, m: (m, 0)),
                  pl.BlockSpec((1, k, tn), lambda n, m: (j, 0, n))],
        out_specs=pl.BlockSpec((tm, tn), lambda n, m: (m, n)),
        out_shape=jax.ShapeDtypeStruct((st.n, n_out), F32),
        scratch_shapes=[pltpu.VMEM((k, tn), BF16)],
        compiler_params=_cparams("parallel", "arbitrary"),
        name="proj_in",
    )(a, w_all)


def _proj_res_kernel(n_prompt_tiles, a_ref, w_ref, x_ref, gr, gt, o_ref, wb_ref):
    @pl.when(pl.program_id(1) == 0)
    def _():
        wb_ref[...] = w_ref[0].astype(BF16)
    gate = _pick(pl.program_id(1) >= n_prompt_tiles, gr, gt)
    o_ref[...] = x_ref[...] + gate * _dot(a_ref[...], wb_ref[...])


def _proj_residual(st, a, w_all, j, x, mod_row, mod_tok, layer, j_gate):
    _, k, d = w_all.shape
    tn = min(PROJ_TN // 2, d)
    nb = d // tn
    return pl.pallas_call(
        functools.partial(_proj_res_kernel, st.n_prompt_tiles),
        grid=(nb, st.n_tiles),
        in_specs=[pl.BlockSpec((st.tm, k), lambda n, m: (m, 0)),
                  pl.BlockSpec((1, k, tn), lambda n, m: (j, 0, n)),
                  pl.BlockSpec((st.tm, tn), lambda n, m: (m, n)),
                  *st.mod_specs(layer, tn, lambda n, m: j_gate * nb + n, tile_axis=1)],
        out_specs=pl.BlockSpec((st.tm, tn), lambda n, m: (m, n)),
        out_shape=jax.ShapeDtypeStruct((st.n, d), F32),
        scratch_shapes=[pltpu.VMEM((k, tn), BF16)],
        compiler_params=_cparams("parallel", "arbitrary"),
        name="proj_out",
    )(a, w_all, x, mod_row, mod_tok)


def _hgrn_tables(c):
    nbits = int(math.log2(c))
    assert 1 << nbits == c
    r = np.arange(c)[:, None]
    j = np.arange(c)[None, :]
    mats = [(j <= r), (j > r)]
    for b in range(nbits):
        mid = ((r >> (b + 1)) << (b + 1)) + (1 << b)
        right = r >= mid
        mats.append(np.where(right, (j >= mid) & (j <= r), (j > r) & (j < mid)))
    level = np.full((c, c), -1, np.int32)
    level[r[:, 0], r[:, 0]] = 0
    diff = r ^ j
    for b in range(nbits):
        level[(j < r) & ((diff >> b) == 1)] = b + 1
    return np.concatenate(mats, 0).astype(np.float32), level, nbits


def _hgrn_kernel(c, nbits, hb, kd, q_ref, z_ref, i_ref, g_ref, lb_ref, gain_ref, ms_ref, lv_ref,
                 o_ref, so_ref, s_scr):
    ci = pl.program_id(2)

    @pl.when(ci == 0)
    def _():
        s_scr[...] = jnp.zeros_like(s_scr)

    lb = lb_ref[...]
    f = lb + (1.0 - lb) * jax.nn.sigmoid(z_ref[...])
    logf = jnp.log(f)
    k = 1.0 - f
    q = _silu(q_ref[...])
    v = i_ref[...].astype(BF16)
    e = _dot2_wide(ms_ref[...], logf)
    cum = e[0:c]
    qd = (q * jnp.exp(cum)).astype(BF16)
    kdec = k * jnp.exp(e[c:2 * c])
    whole_row = cum[c - 1:c]
    lv = lv_ref[...]
    masks = [lv == b for b in range(nbits + 1)]
    qb, kb = q.astype(BF16), k.astype(BF16)
    levels = [(qb, kb)]
    for b in range(nbits):
        gl = jnp.exp(e[(2 + b) * c:(3 + b) * c]).astype(BF16)
        levels.append((qb * gl, kb * gl))
    gate = gain_ref[...] * _silu(g_ref[...])

    for h in range(hb):
        sl = slice(h * kd, (h + 1) * kd)
        s = s_scr[h]
        o = _dot(qd[:, sl], s.astype(BF16))
        att = jnp.zeros((c, c), F32)
        for mask, (ql, kl) in zip(masks, levels):
            att = jnp.where(mask, _dot_nt(ql[:, sl], kl[:, sl]), att)
        o = o + _dot(att.astype(BF16), v[:, sl])
        whole = jnp.exp(jnp.broadcast_to(whole_row[:, sl], (s.shape[1], s.shape[0])).T)
        s_scr[h] = whole * s + _dot(kdec[:, sl].T.astype(BF16), v[:, sl])
        o = o * lax.rsqrt(jnp.mean(o * o, axis=-1, keepdims=True) + EPS)
        o_ref[:, sl] = (o * gate[:, sl]).astype(BF16)

    @pl.when(ci == pl.num_programs(2) - 1)
    def _():
        so_ref[0] = s_scr[...]


def _hgrn_scan(st, proj, lb, gain, heads, kd, vd):
    width = heads * kd
    c = min(HGRN_CHUNK, st.seq)
    hb = min(HGRN_HEADS_PER_STEP, heads)
    n_chunks = st.seq // c
    groups = heads // hb
    mstack, level, nbits = _hgrn_tables(c)

    def seg(j):
        return pl.BlockSpec((c, hb * kd), lambda b, h, ci: (b * n_chunks + ci, j * groups + h))

    head_row = pl.BlockSpec((1, hb * kd), lambda b, h, ci: (0, h))
    return pl.pallas_call(
        functools.partial(_hgrn_kernel, c, nbits, hb, kd),
        grid=(st.batch, groups, n_chunks),
        in_specs=[seg(0), seg(1), seg(2), seg(3), head_row, head_row,
                  pl.BlockSpec((mstack.shape[0], 2 * c), lambda b, h, ci: (0, 0)),
                  pl.BlockSpec(level.shape, lambda b, h, ci: (0, 0))],
        out_specs=[pl.BlockSpec((c, hb * vd), lambda b, h, ci: (b * n_chunks + ci, h)),
                   pl.BlockSpec((1, hb, kd, vd), lambda b, h, ci: (b, h, 0, 0))],
        out_shape=[jax.ShapeDtypeStruct((st.n, heads * vd), BF16),
                   jax.ShapeDtypeStruct((st.batch, heads, kd, vd), F32)],
        scratch_shapes=[pltpu.VMEM((hb, kd, vd), F32)],
        compiler_params=_cparams("parallel", "parallel", "arbitrary"),
        name="hgrn_scan",
    )(proj, proj, proj, proj, lb.reshape(1, width), gain.reshape(1, width),
      jnp.asarray(np.concatenate([mstack, mstack], 1), BF16), jnp.asarray(level))


def _step_tables(bb, t):
    rows = bb * t
    r = np.arange(rows)
    grp, pos = r // t, r % t
    same = grp[:, None] == grp[None, :]
    i, j = r[:, None], r[None, :]
    sums = [same & (j <= i), same & (j > i), same]
    for d in range(1, t):
        sums.append(same & (j <= i) & (j > i - d) & (pos[:, None] >= d))
    shifts = [same & (j == i - d) for d in range(1, t)]
    return (np.concatenate(sums, 0).astype(np.float32), np.concatenate(shifts, 0).astype(np.float32))


def _hgrn_step_kernel(bb, t, heads, kd, q_ref, z_ref, i_ref, g_ref, lb_ref, gain_ref, ms_ref, sh_ref,
                      seg_ref, segt_ref, s_ref, *rest):
    o_ref, so_ref = rest[-2:]
    rows = bb * t
    lb = lb_ref[...]
    f = lb + (1.0 - lb) * jax.nn.sigmoid(z_ref[...])
    logf = jnp.log(f)
    k = 1.0 - f
    q = _silu(q_ref[...])
    v = i_ref[...]
    e = _dot2(ms_ref[...], logf)
    qd = (q * jnp.exp(e[0:rows])).astype(BF16)
    kdec = k * jnp.exp(e[rows:2 * rows])
    whole = jnp.exp(e[2 * rows:3 * rows])

    seg, segt = seg_ref[...], segt_ref[...]
    o = jnp.zeros_like(q)
    for d in range(t):
        if d == 0:
            ks, vs, w = k, v, q * k
        else:
            sh = sh_ref[(d - 1) * rows:d * rows]
            ks, vs = _dot2(sh, k), _dot2(sh, v)
            w = q * ks * jnp.exp(e[(2 + d) * rows:(3 + d) * rows])
        att = _dot2_lhs(w, seg)
        o = o + _dot2_lhs(att, segt) * vs

    gate = gain_ref[...] * _silu(g_ref[...])
    vb = v.astype(BF16)
    row_seq = lax.broadcasted_iota(jnp.int32, (rows, kd), 0) // t
    lane_seq = lax.broadcasted_iota(jnp.int32, (kd, rows), 1) // t
    for h in range(heads):
        sl = slice(h * kd, (h + 1) * kd)
        k_t = kdec[:, sl].T
        w_t = whole[:, sl].T
        o_h = o[:, sl]
        for b in range(bb):
            s = s_ref[0, b, h]
            o_h = o_h + jnp.where(row_seq == b, _dot(qd[:, sl], s.astype(BF16)), 0.0)
            k_b = jnp.where(lane_seq == b, k_t, 0.0).astype(BF16)
            so_ref[0, b, h] = w_t[:, b * t:b * t + 1] * s + _dot(k_b, vb[:, sl])
        o_h = o_h * lax.rsqrt(jnp.mean(o_h * o_h, axis=-1, keepdims=True) + EPS)
        o_ref[:, sl] = (o_h * gate[:, sl]).astype(BF16)


def _hgrn_step(st, proj, lb, gain, state_all, j, o_stream, so_prev):
    _, _, heads, kd, vd = state_all.shape
    width = heads * kd
    t = st.dec_seq
    bb = BF16_SUBLANES // math.gcd(BF16_SUBLANES, t)
    rows = bb * t
    assert st.dec_batch % bb == 0 and st.n_prompt % rows == 0 and heads <= LANES
    rb0 = st.n_prompt // rows
    sums, shifts = _step_tables(bb, t)
    seg = (np.arange(width)[:, None] // kd == np.arange(LANES)[None, :]).astype(np.float32)

    def col(jj):
        return pl.BlockSpec((rows, width), lambda i: (rb0 + i, jj))

    full = lambda a: pl.BlockSpec(a.shape, lambda i: (0,) * a.ndim)
    state = pl.BlockSpec((1, bb, heads, kd, vd), lambda i: (j, i, 0, 0, 0))
    anyspec = pl.BlockSpec(memory_space=pl.ANY)
    args = [proj, proj, proj, proj, lb.reshape(1, width), gain.reshape(1, width),
            jnp.asarray(sums, BF16), jnp.asarray(shifts, BF16), jnp.asarray(seg, BF16),
            jnp.asarray(seg.T, BF16), state_all, o_stream]
    in_specs = [col(0), col(1), col(2), col(3), pl.BlockSpec((1, width), lambda i: (0, 0)),
                pl.BlockSpec((1, width), lambda i: (0, 0)), full(sums), full(shifts), full(seg),
                full(seg.T), state, anyspec]
    aliases = {11: 0}
    if so_prev is not None:
        args.append(so_prev)
        in_specs.append(anyspec)
        aliases[12] = 1
    return pl.pallas_call(
        functools.partial(_hgrn_step_kernel, bb, t, heads, kd),
        grid=(st.dec_batch // bb,),
        in_specs=in_specs,
        out_specs=[pl.BlockSpec((rows, width), lambda i: (rb0 + i, 0)), state],
        out_shape=[jax.ShapeDtypeStruct(o_stream.shape, BF16),
                   jax.ShapeDtypeStruct(state_all.shape, F32)],
        input_output_aliases=aliases,
        compiler_params=_cparams("parallel"),
        name="hgrn_step",
    )(*args)


def _rotate(x, cos, sin):
    half = x.shape[-1] // 2
    x1, x2 = x[:, :half], x[:, half:]
    return jnp.concatenate([x1 * cos - x2 * sin, x1 * sin + x2 * cos], axis=-1)


def _group_norm_gate(o, gain, g):
    mu = jnp.mean(o, axis=-1, keepdims=True)
    var = jnp.mean(jnp.square(o - mu), axis=-1, keepdims=True)
    return (_silu(g) * ((o - mu) * lax.rsqrt(var + EPS) * gain)).astype(BF16)


def _ret_kernel(c, lg_ref, q_ref, k_ref, v_ref, g_ref, cos_ref, sin_ref, gain_ref, o_ref, so_ref, s_scr):
    ci = pl.program_id(2)

    @pl.when(ci == 0)
    def _():
        s_scr[...] = jnp.zeros_like(s_scr)

    lg = lg_ref[pl.program_id(1)]
    cos, sin = cos_ref[...], sin_ref[...]
    q = _rotate(q_ref[...], cos, sin)
    k = _rotate(k_ref[...], cos, sin) * (q.shape[-1] ** -0.5)
    v = v_ref[...].astype(BF16)

    t_col = lax.broadcasted_iota(jnp.int32, (c, 1), 0).astype(F32)
    inter = jnp.exp((t_col + 1.0) * lg)
    tail = jnp.exp((c - 1.0 - t_col) * lg)
    rel = (lax.broadcasted_iota(jnp.int32, (c, c), 0) - lax.broadcasted_iota(jnp.int32, (c, c), 1)).astype(F32)
    intra = jnp.where(rel >= 0.0, jnp.exp(jnp.maximum(rel, 0.0) * lg), 0.0)
    whole = jnp.exp(jnp.full((1, 1), c, F32) * lg)

    s = s_scr[...]
    att = _dot_nt(q.astype(BF16), k.astype(BF16)) * intra
    o = _dot(att.astype(BF16), v) + _dot((q * inter).astype(BF16), s.astype(BF16))
    s_new = whole * s + _dot((k * tail).T.astype(BF16), v)
    s_scr[...] = s_new

    @pl.when(ci == pl.num_programs(2) - 1)
    def _():
        so_ref[0, 0] = s_new

    o_ref[...] = _group_norm_gate(o, gain_ref[...], g_ref[...])


def _ret_scan(st, proj, cos, sin, log_gamma, gain, heads, dk, dv):
    c = min(RET_CHUNK, st.seq)
    n_chunks = st.seq // c
    v_off = 2 * heads * dk // dv
    g_off = v_off + heads

    def qk(j):
        return pl.BlockSpec((c, dk), lambda b, h, ci: (b * n_chunks + ci, j * heads + h))

    def vg(off):
        return pl.BlockSpec((c, dv), lambda b, h, ci: (b * n_chunks + ci, off + h))

    rope = pl.BlockSpec((c, dk // 2), lambda b, h, ci: (ci, 0))
    return pl.pallas_call(
        functools.partial(_ret_kernel, c),
        grid=(st.batch, heads, n_chunks),
        in_specs=[pl.BlockSpec(memory_space=pltpu.SMEM),
                  qk(0), qk(1), vg(v_off), vg(g_off), rope, rope,
                  pl.BlockSpec((1, dv), lambda b, h, ci: (0, h))],
        out_specs=[pl.BlockSpec((c, dv), lambda b, h, ci: (b * n_chunks + ci, h)),
                   pl.BlockSpec((1, 1, dk, dv), lambda b, h, ci: (b, h, 0, 0))],
        out_shape=[jax.ShapeDtypeStruct((st.n, heads * dv), BF16),
                   jax.ShapeDtypeStruct((st.batch, heads, dk, dv), F32)],
        scratch_shapes=[pltpu.VMEM((dk, dv), F32)],
        compiler_params=_cparams("parallel", "parallel", "arbitrary"),
        name="ret_scan",
    )(log_gamma, proj, proj, proj, proj, cos, sin, gain.reshape(1, heads * dv))


def _ret_step_kernel(bb, t, lg_ref, q_ref, k_ref, v_ref, g_ref, cos_ref, sin_ref, pos_ref, rel_ref,
                     gain_ref, s_ref, *rest):
    o_ref, so_ref = rest[-2:]
    rows = bb * t
    lg = lg_ref[pl.program_id(1)]
    cos, sin = cos_ref[...], sin_ref[...]
    q = _rotate(q_ref[...], cos, sin)
    k = _rotate(k_ref[...], cos, sin) * (q.shape[-1] ** -0.5)
    v = v_ref[...].astype(BF16)

    t_col = pos_ref[...][:, :1]
    inter = jnp.exp((t_col + 1.0) * lg)
    tail = jnp.exp((t - 1.0 - t_col) * lg)
    rel = rel_ref[...]
    intra = jnp.where(rel >= 0.0, jnp.exp(jnp.maximum(rel, 0.0) * lg), 0.0)
    whole = jnp.exp(jnp.full((1, 1), t, F32) * lg)

    att = _dot_nt(q.astype(BF16), k.astype(BF16)) * intra
    o = _dot(att.astype(BF16), v)
    qd = (q * inter).astype(BF16)
    k_t = (k * tail).T
    row_seq = lax.broadcasted_iota(jnp.int32, o.shape, 0) // t
    lane_seq = lax.broadcasted_iota(jnp.int32, k_t.shape, 1) // t
    for b in range(bb):
        s = s_ref[0, b, 0]
        o = o + jnp.where(row_seq == b, _dot(qd, s.astype(BF16)), 0.0)
        k_b = jnp.where(lane_seq == b, k_t, 0.0).astype(BF16)
        so_ref[0, b, 0] = whole * s + _dot(k_b, v)
    o_ref[...] = _group_norm_gate(o, gain_ref[...], g_ref[...])


def _ret_step(st, proj, cos, sin, log_gamma, gain, state_all, j, o_stream, so_prev):
    _, _, heads, dk, dv = state_all.shape
    t = st.dec_seq
    bb = BF16_SUBLANES // math.gcd(BF16_SUBLANES, t)
    rows = bb * t
    assert st.dec_batch % bb == 0 and st.n_prompt % rows == 0
    rb0 = st.n_prompt // rows
    v_off = 2 * heads * dk // dv
    g_off = v_off + heads
    r = np.arange(rows)
    same = (r[:, None] // t) == (r[None, :] // t)
    rel = np.where(same & (r[None, :] <= r[:, None]), r[:, None] - r[None, :], -1).astype(np.float32)
    pos = np.broadcast_to((r % t).astype(np.float32)[:, None], (rows, LANES))

    full = lambda a: pl.BlockSpec(a.shape, lambda i, h: (0,) * a.ndim)
    state = pl.BlockSpec((1, bb, 1, dk, dv), lambda i, h: (j, i, h, 0, 0))
    anyspec = pl.BlockSpec(memory_space=pl.ANY)
    cos_r, sin_r = jnp.tile(cos, (bb, 1)), jnp.tile(sin, (bb, 1))
    args = [log_gamma, proj, proj, proj, proj, cos_r, sin_r, jnp.asarray(pos), jnp.asarray(rel),
            gain.reshape(1, heads * dv), state_all, o_stream]
    in_specs = [pl.BlockSpec(memory_space=pltpu.SMEM),
                pl.BlockSpec((rows, dk), lambda i, h: (rb0 + i, h)),
                pl.BlockSpec((rows, dk), lambda i, h: (rb0 + i, heads + h)),
                pl.BlockSpec((rows, dv), lambda i, h: (rb0 + i, v_off + h)),
                pl.BlockSpec((rows, dv), lambda i, h: (rb0 + i, g_off + h)),
                full(cos_r), full(sin_r), full(pos), full(rel),
                pl.BlockSpec((1, dv), lambda i, h: (0, h)), state, anyspec]
    aliases = {11: 0}
    if so_prev is not None:
        args.append(so_prev)
        in_specs.append(anyspec)
        aliases[12] = 1
    return pl.pallas_call(
        functools.partial(_ret_step_kernel, bb, t),
        grid=(st.dec_batch // bb, heads),
        in_specs=in_specs,
        out_specs=[pl.BlockSpec((rows, dv), lambda i, h: (rb0 + i, h)), state],
        out_shape=[jax.ShapeDtypeStruct(o_stream.shape, BF16),
                   jax.ShapeDtypeStruct(state_all.shape, F32)],
        input_output_aliases=aliases,
        compiler_params=_cparams("parallel", "parallel"),
        name="ret_step",
    )(*args)


def _rope_tables(pos, half):
    inv = 1.0 / (ROPE_BASE ** jnp.linspace(0.0, 1.0, half, dtype=F32))
    ang = pos[:, None] * inv[None]
    return jnp.cos(ang), jnp.sin(ang)


def _expert_kernel(te_ref, ts_ref, na_ref, x_ref, w_ref, wg_ref, wu_ref, wd_ref, o_ref,
                   wgb, wub, wdb):
    t = pl.program_id(0)
    active = t < na_ref[0]
    new_expert = (t == 0) | (te_ref[t] != te_ref[jnp.maximum(t - 1, 0)])

    @pl.when(active & new_expert)
    def _():
        wgb[...] = wg_ref[0].astype(BF16)
        wub[...] = wu_ref[0].astype(BF16)
        wdb[...] = wd_ref[0].astype(BF16)

    @pl.when(active)
    def _():
        x = x_ref[...].astype(BF16)
        hg = _silu(_dot(x, wgb[...])) * _dot(x, wub[...])
        o_ref[...] = _dot((hg * w_ref[...]).astype(BF16), wdb[...])

    @pl.when(jnp.logical_not(active))
    def _():
        o_ref[...] = jnp.zeros_like(o_ref)


def _experts(x_sorted, w_sorted, tile_expert, tile_src, n_active, w_gate, w_up, w_down):
    rows, d = x_sorted.shape
    _, _, ff = w_gate.shape
    tm = EXPERT_TILE
    grid_spec = pltpu.PrefetchScalarGridSpec(
        num_scalar_prefetch=3,
        grid=(rows // tm,),
        in_specs=[pl.BlockSpec((tm, d), lambda t, te, ts, na: (ts[t], 0)),
                  pl.BlockSpec((tm, 1), lambda t, te, ts, na: (ts[t], 0)),
                  pl.BlockSpec((1, d, ff), lambda t, te, ts, na: (te[t], 0, 0)),
                  pl.BlockSpec((1, d, ff), lambda t, te, ts, na: (te[t], 0, 0)),
                  pl.BlockSpec((1, ff, d), lambda t, te, ts, na: (te[t], 0, 0))],
        out_specs=pl.BlockSpec((tm, d), lambda t, te, ts, na: (t, 0)),
        scratch_shapes=[pltpu.VMEM((d, ff), BF16), pltpu.VMEM((d, ff), BF16), pltpu.VMEM((ff, d), BF16)],
    )
    return pl.pallas_call(
        _expert_kernel,
        grid_spec=grid_spec,
        out_shape=jax.ShapeDtypeStruct((rows, d), F32),
        compiler_params=_cparams("arbitrary"),
        name="experts",
    )(tile_expert, tile_src, n_active, x_sorted, w_sorted, w_gate, w_up, w_down)


def _take_rows(a, idx):
    return a.at[idx].get(mode="promise_in_bounds")


def _step_lookup(table, starts, x):
    jumps = table[1:] - table[:-1]
    return table[0] + jnp.sum(jnp.where(x[None, :] >= starts[1:, None], jumps[:, None], 0), axis=0)


def _dispatch_plan(eid, ew, n_experts_total):
    n, top_k = eid.shape
    tm = EXPERT_TILE
    a = n * top_k
    n_tiles = a // tm + n_experts_total
    rows = n_tiles * tm
    i32 = jnp.int32
    q = jnp.arange(a, dtype=i32)
    sorted_e, order, sorted_w = lax.sort((eid.reshape(a), q, ew.reshape(a)), num_keys=1, is_stable=True)
    bounds = jnp.searchsorted(sorted_e, jnp.arange(n_experts_total + 1, dtype=i32), side="left").astype(i32)
    off, end = bounds[:-1], bounds[1:]
    padded = (end - off + tm - 1) // tm * tm
    pad_end = jnp.cumsum(padded)
    pad_off = pad_end - padded
    delta = pad_off - off
    _, pos = lax.sort((order, q + _step_lookup(delta, off, q)), num_keys=1)
    n_active = (pad_end[-1] // tm).astype(i32)
    tile_src = jnp.minimum(jnp.arange(n_tiles, dtype=i32), n_active - 1)
    tile_expert = _step_lookup(jnp.arange(n_experts_total, dtype=i32), pad_off, tile_src * tm)
    p = jnp.arange(rows, dtype=i32)
    src = p - _step_lookup(delta, pad_off, p)
    valid = (src < _step_lookup(end, pad_off, p)) & (p < pad_end[-1])
    src = jnp.clip(src, 0, a - 1)
    row_token = jnp.where(valid, _take_rows(order, src) // top_k, p % n)
    row_w = jnp.where(valid, _take_rows(sorted_w, src), 0.0)
    return row_token, row_w.reshape(rows, 1), pos.reshape(n, top_k), tile_expert, tile_src, n_active.reshape(1)


def _combine_kernel(n_prompt_tiles, final, x_ref, ya_ref, yb_ref, gr, gt, fg_ref, o_ref):
    gate = _pick(pl.program_id(0) >= n_prompt_tiles, gr, gt)
    x = x_ref[...] + gate * (ya_ref[...] + yb_ref[...])
    if final:
        x = _rms(x) * fg_ref[...]
    o_ref[...] = x


def _combine(st, x, ya, yb, mod_row, mod_tok, layer, j_gate, final_gain, final):
    d = st.d
    tile = pl.BlockSpec((st.tm, d), lambda i: (i, 0))
    return pl.pallas_call(
        functools.partial(_combine_kernel, st.n_prompt_tiles, final),
        grid=(st.n_tiles,),
        in_specs=[tile, tile, tile, *st.mod_specs(layer, d, lambda i: j_gate),
                  pl.BlockSpec((1, d), lambda i: (0, 0))],
        out_specs=tile,
        out_shape=jax.ShapeDtypeStruct((st.n, d), F32),
        compiler_params=_cparams("parallel"),
        name="moe_combine",
    )(x, ya, yb, mod_row, mod_tok, final_gain.reshape(1, d))


def kernel(x_prompt, x_sample, state_hgrn, state_ret, c_prompt, c_sample, ada_w, ada_b, hgrn_w_in, hgrn_lb_logits, hgrn_norm_gain, hgrn_w_out, ret_w_in, ret_norm_gain, ret_w_out, moe_router_group, moe_router_expert, moe_w_gate, moe_w_up, moe_w_down, final_norm_gain):
    batch, seq, d = x_prompt.shape
    dec_batch, dec_seq, _ = x_sample.shape
    depth = ada_w.shape[0]
    _, _, a_heads, a_kd, a_vd = state_hgrn.shape
    _, _, b_heads, b_dk, b_dv = state_ret.shape
    _, n_groups, _, n_exp = moe_router_expert.shape
    ff = moe_w_gate.shape[-1]
    n_exp_total = n_groups * n_exp
    st = _Stream(batch, seq, dec_batch, dec_seq, d)

    p = jax.nn.softmax(hgrn_lb_logits.astype(F32), axis=0)
    lb_all = jnp.cumsum(p, axis=0) - p[0:1]
    log_gamma = jnp.log1p(-jnp.exp2(-5.0 - jnp.arange(b_heads, dtype=F32)))
    cos_p, sin_p = _rope_tables(jnp.arange(seq, dtype=F32), b_dk // 2)
    cos_s, sin_s = _rope_tables(jnp.arange(dec_seq, dtype=F32) + PAST_LEN, b_dk // 2)
    w_gate = moe_w_gate.reshape(depth * n_exp_total, d, ff)
    w_up = moe_w_up.reshape(depth * n_exp_total, d, ff)
    w_down = moe_w_down.reshape(depth * n_exp_total, ff, d)

    n_c = batch + dec_batch
    c_rows = -(-n_c // BF16_SUBLANES) * BF16_SUBLANES
    c_all = jnp.pad(jnp.concatenate([c_prompt, c_sample], 0), ((0, c_rows - n_c), (0, 0)))
    mod = _ada_mod(c_all, ada_w, ada_b)
    mod_row = mod[:, :batch].reshape(depth * batch, 1, 6 * d)
    mod_tok = jnp.repeat(mod[:, batch:n_c], dec_seq, axis=1)

    x = jnp.concatenate([x_prompt.reshape(-1, d), x_sample.reshape(-1, d)], 0)
    hg_p, rt_p, hg_s, rt_s = [], [], None, None
    for l in range(depth):
        j = l // 2
        hn = _norm_mod(st, x, mod_row, mod_tok, l, 0, 1)
        if l % 2 == 0:
            proj = _proj(st, hn, hgrn_w_in, j)
            o, s_p = _hgrn_scan(st, proj, lb_all[j], hgrn_norm_gain[j], a_heads, a_kd, a_vd)
            o, hg_s = _hgrn_step(st, proj, lb_all[j], hgrn_norm_gain[j], state_hgrn, j, o, hg_s)
            hg_p.append(s_p)
            w_out = hgrn_w_out
        else:
            proj = _proj(st, hn, ret_w_in, j)
            o, s_p = _ret_scan(st, proj, cos_p, sin_p, log_gamma, ret_norm_gain[j], b_heads, b_dk, b_dv)
            o, rt_s = _ret_step(st, proj, cos_s, sin_s, log_gamma, ret_norm_gain[j], state_ret, j, o, rt_s)
            rt_p.append(s_p)
            w_out = ret_w_out
        x = _proj_residual(st, o, w_out, j, x, mod_row, mod_tok, l, 2)

        w_router = jnp.concatenate(
            [moe_router_group[l], jnp.moveaxis(moe_router_expert[l], 0, 1).reshape(d, n_exp_total)], 1)
        w_router = jnp.pad(w_router, ((0, 0), (0, LANES - w_router.shape[1])))
        hn, eid, ew = _norm_mod_router(st, x, mod_row, mod_tok, l, 3, 4, w_router, n_groups, n_exp)
        row_token, row_w, pos, tile_expert, tile_src, n_active = _dispatch_plan(eid, ew, n_exp_total)
        y_sorted = _experts(_take_rows(hn, row_token), row_w, tile_expert + l * n_exp_total, tile_src,
                            n_active, w_gate, w_up, w_down)
        x = _combine(st, x, _take_rows(y_sorted, pos[:, 0]), _take_rows(y_sorted, pos[:, 1]),
                     mod_row, mod_tok, l, 5, final_norm_gain, l == depth - 1)

    y_prompt = x[:st.n_prompt].reshape(batch, seq, d)
    y_sample = x[st.n_prompt:].reshape(dec_batch, dec_seq, d)
    return (y_prompt, y_sample, jnp.stack(hg_p), jnp.stack(rt_p), hg_s, rt_s)
```

```python
import functools
import math

import numpy as np
import jax
import jax.numpy as jnp
from jax import lax
from jax.experimental import pallas as pl
from jax.experimental.pallas import tpu as pltpu

F32 = jnp.float32
BF16 = jnp.bfloat16
EPS = 1e-6
PAST_LEN = 16384
ROPE_BASE = 10000.0

V7X_VMEM_LIMIT_BYTES = 56 * 1024 * 1024
LANES = 128
BF16_SUBLANES = 16

HGRN_CHUNK = 128
HGRN_HEADS_PER_STEP = 8
RET_CHUNK = 256
RET_HEADS_PER_STEP = 2
EXPERT_TILE = 256
PROJ_TN = 1024
PROJ_IN_ROW_TILES = 8
ADA_TN = 1024


def _cparams(*sem):
    return pltpu.CompilerParams(dimension_semantics=sem, vmem_limit_bytes=V7X_VMEM_LIMIT_BYTES)


def _dot(a, b):
    return jnp.dot(a, b, preferred_element_type=F32)


def _dot_nt(a, b):
    return lax.dot_general(a, b, (((1,), (1,)), ((), ())), preferred_element_type=F32)


def _dot2(m, x):
    hi = x.astype(BF16)
    lo = (x - hi.astype(F32)).astype(BF16)
    return _dot(m, hi) + _dot(m, lo)


def _dot2_wide(mm, x):
    hi = x.astype(BF16)
    lo = (x - hi.astype(F32)).astype(BF16)
    return _dot(mm, jnp.concatenate([hi, lo], axis=0))


def _dot2_lhs(x, m):
    hi = x.astype(BF16)
    lo = (x - hi.astype(F32)).astype(BF16)
    return _dot(hi, m) + _dot(lo, m)


def _silu(x):
    return x * jax.nn.sigmoid(x)


def _rms(x):
    return x * lax.rsqrt(jnp.mean(x * x, axis=-1, keepdims=True) + EPS)


def _ada_kernel(c_ref, w_ref, b_ref, o_ref):
    c = c_ref[...]
    o_ref[0] = _dot(_silu(c).astype(BF16), w_ref[0].astype(BF16)) + b_ref[0]


def _ada_mod(c_all, ada_w, ada_b):
    depth, d, n6 = ada_w.shape
    r = c_all.shape[0]
    tn = min(ADA_TN, n6)
    return pl.pallas_call(
        _ada_kernel,
        grid=(depth, n6 // tn),
        in_specs=[pl.BlockSpec((r, d), lambda l, n: (0, 0)),
                  pl.BlockSpec((1, d, tn), lambda l, n: (l, 0, n)),
                  pl.BlockSpec((1, 1, tn), lambda l, n: (l, 0, n))],
        out_specs=pl.BlockSpec((1, r, tn), lambda l, n: (l, 0, n)),
        out_shape=jax.ShapeDtypeStruct((depth, r, n6), F32),
        compiler_params=_cparams("parallel", "parallel"),
        name="ada_mod",
    )(c_all, ada_w, ada_b.reshape(depth, 1, n6))


class _Stream:
    def __init__(self, batch, seq, dec_batch, dec_seq, d):
        self.batch, self.seq, self.dec_batch, self.dec_seq, self.d = batch, seq, dec_batch, dec_seq, d
        self.n_prompt = batch * seq
        self.n_sample = dec_batch * dec_seq
        self.tm = self.n_sample
        assert seq % self.tm == 0 and self.tm % BF16_SUBLANES == 0
        self.tiles_per_seq = seq // self.tm
        self.n_prompt_tiles = self.n_prompt // self.tm
        self.n_tiles = self.n_prompt_tiles + 1
        self.n = self.n_prompt + self.n_sample

    def mod_specs(self, layer, width, col_block, tile_axis=0):
        tps, last, batch = self.tiles_per_seq, self.batch - 1, self.batch

        def row_map(*idx):
            return (layer * batch + jnp.minimum(idx[tile_axis] // tps, last), 0, col_block(*idx))
        return (pl.BlockSpec((1, 1, width), row_map),
                pl.BlockSpec((1, self.tm, width), lambda *idx: (layer, 0, col_block(*idx))))


def _pick(is_sample, row_ref, tok_ref):
    return jnp.where(is_sample, tok_ref[0], row_ref[0])


def _norm_mod_kernel(n_prompt_tiles, x_ref, shr, sht, scr, sct, o_ref):
    is_s = pl.program_id(0) >= n_prompt_tiles
    h = _rms(x_ref[...]) * (1.0 + _pick(is_s, scr, sct)) + _pick(is_s, shr, sht)
    o_ref[...] = h.astype(BF16)


def _norm_mod(st, x, mod_row, mod_tok, layer, j_shift, j_scale):
    d = st.d
    return pl.pallas_call(
        functools.partial(_norm_mod_kernel, st.n_prompt_tiles),
        grid=(st.n_tiles,),
        in_specs=[pl.BlockSpec((st.tm, d), lambda i: (i, 0)),
                  *st.mod_specs(layer, d, lambda i: j_shift), *st.mod_specs(layer, d, lambda i: j_scale)],
        out_specs=pl.BlockSpec((st.tm, d), lambda i: (i, 0)),
        out_shape=jax.ShapeDtypeStruct((st.n, d), BF16),
        compiler_params=_cparams("parallel"),
        name="norm_mod",
    )(x, mod_row, mod_tok, mod_row, mod_tok)


def _norm_mod_router_kernel(n_prompt_tiles, n_groups, n_experts, x_ref, shr, sht, scr, sct, wr_ref,
                            o_ref, eid_ref, ew_ref):
    is_s = pl.program_id(0) >= n_prompt_tiles
    h = _rms(x_ref[...]) * (1.0 + _pick(is_s, scr, sct)) + _pick(is_s, shr, sht)
    o_ref[...] = h

    logits = _dot(h.astype(BF16), wr_ref[...].astype(BF16))
    tm = logits.shape[0]
    lane = lax.broadcasted_iota(jnp.int32, (tm, LANES), 1)
    neg_inf = jnp.float32(-jnp.inf)

    def first_index_of_max(vals, vmax):
        return jnp.min(jnp.where(vals == vmax, lane, LANES), axis=1, keepdims=True)

    is_g = lane < n_groups
    lg = jnp.where(is_g, logits, neg_inf)
    mg = jnp.max(lg, axis=1, keepdims=True)
    eg = jnp.exp(lg - mg)
    pg = eg / jnp.sum(eg, axis=1, keepdims=True)
    gi = first_index_of_max(lg, mg)
    wg = jnp.sum(jnp.where(lane == gi, pg, 0.0), axis=1, keepdims=True)

    lo = n_groups + gi * n_experts
    sel = (lane >= lo) & (lane < lo + n_experts)
    le = jnp.where(sel, logits, neg_inf)
    me = jnp.max(le, axis=1, keepdims=True)
    ee = jnp.exp(le - me)
    pe = jnp.where(sel, ee / jnp.sum(ee, axis=1, keepdims=True), -1.0)
    p1 = jnp.max(pe, axis=1, keepdims=True)
    i1 = first_index_of_max(pe, p1)
    pe2 = jnp.where(lane == i1, -1.0, pe)
    p2 = jnp.max(pe2, axis=1, keepdims=True)
    i2 = first_index_of_max(pe2, p2)
    denom = p1 + p2
    col = lax.broadcasted_iota(jnp.int32, (tm, 2), 1)
    eid_ref[...] = jnp.where(col == 0, i1, i2) - n_groups
    ew_ref[...] = wg * jnp.where(col == 0, p1 / denom, p2 / denom)


def _norm_mod_router(st, x, mod_row, mod_tok, layer, j_shift, j_scale, w_router, n_groups, n_experts):
    d = st.d
    return pl.pallas_call(
        functools.partial(_norm_mod_router_kernel, st.n_prompt_tiles, n_groups, n_experts),
        grid=(st.n_tiles,),
        in_specs=[pl.BlockSpec((st.tm, d), lambda i: (i, 0)),
                  *st.mod_specs(layer, d, lambda i: j_shift), *st.mod_specs(layer, d, lambda i: j_scale),
                  pl.BlockSpec((d, LANES), lambda i: (0, 0))],
        out_specs=[pl.BlockSpec((st.tm, d), lambda i: (i, 0)),
                   pl.BlockSpec((st.tm, 2), lambda i: (i, 0)),
                   pl.BlockSpec((st.tm, 2), lambda i: (i, 0))],
        out_shape=[jax.ShapeDtypeStruct((st.n, d), F32),
                   jax.ShapeDtypeStruct((st.n, 2), jnp.int32),
                   jax.ShapeDtypeStruct((st.n, 2), F32)],
        compiler_params=_cparams("parallel"),
        name="norm_mod_router",
    )(x, mod_row, mod_tok, mod_row, mod_tok, w_router)


def _proj_kernel(a_ref, w_ref, o_ref, wb_ref):
    @pl.when(pl.program_id(1) == 0)
    def _():
        wb_ref[...] = w_ref[0].astype(BF16)
    o_ref[...] = _dot(a_ref[...], wb_ref[...])


def _proj(st, a, w_all, j):
    _, k, n_out = w_all.shape
    tn = min(PROJ_TN, n_out)
    tm = st.n // PROJ_IN_ROW_TILES if st.n % (PROJ_IN_ROW_TILES * BF16_SUBLANES) == 0 else st.tm
    return pl.pallas_call(
        _proj_kernel,
        grid=(n_out // tn, st.n // tm),
        in_specs=[pl.BlockSpec((tm, k), lambda n, m: (m, 0)),
                  pl.BlockSpec((1, k, tn), lambda n, m: (j, 0, n))],
        out_specs=pl.BlockSpec((tm, tn), lambda n, m: (m, n)),
        out_shape=jax.ShapeDtypeStruct((st.n, n_out), F32),
        scratch_shapes=[pltpu.VMEM((k, tn), BF16)],
        compiler_params=_cparams("parallel", "arbitrary"),
        name="proj_in",
    )(a, w_all)


def _proj_res_kernel(n_prompt_tiles, a_ref, w_ref, x_ref, gr, gt, o_ref, wb_ref):
    @pl.when(pl.program_id(1) == 0)
    def _():
        wb_ref[...] = w_ref[0].astype(BF16)
    gate = _pick(pl.program_id(1) >= n_prompt_tiles, gr, gt)
    o_ref[...] = x_ref[...] + gate * _dot(a_ref[...], wb_ref[...])


def _proj_residual(st, a, w_all, j, x, mod_row, mod_tok, layer, j_gate):
    _, k, d = w_all.shape
    tn = min(PROJ_TN // 2, d)
    nb = d // tn
    return pl.pallas_call(
        functools.partial(_proj_res_kernel, st.n_prompt_tiles),
        grid=(nb, st.n_tiles),
        in_specs=[pl.BlockSpec((st.tm, k), lambda n, m: (m, 0)),
                  pl.BlockSpec((1, k, tn), lambda n, m: (j, 0, n)),
                  pl.BlockSpec((st.tm, tn), lambda n, m: (m, n)),
                  *st.mod_specs(layer, tn, lambda n, m: j_gate * nb + n, tile_axis=1)],
        out_specs=pl.BlockSpec((st.tm, tn), lambda n, m: (m, n)),
        out_shape=jax.ShapeDtypeStruct((st.n, d), F32),
        scratch_shapes=[pltpu.VMEM((k, tn), BF16)],
        compiler_params=_cparams("parallel", "arbitrary"),
        name="proj_out",
    )(a, w_all, x, mod_row, mod_tok)


def _hgrn_tables(c):
    nbits = int(math.log2(c))
    assert 1 << nbits == c
    r = np.arange(c)[:, None]
    j = np.arange(c)[None, :]
    mats = [(j <= r), (j > r)]
    for b in range(nbits):
        mid = ((r >> (b + 1)) << (b + 1)) + (1 << b)
        right = r >= mid
        mats.append(np.where(right, (j >= mid) & (j <= r), (j > r) & (j < mid)))
    level = np.full((c, c), -1, np.int32)
    level[r[:, 0], r[:, 0]] = 0
    diff = r ^ j
    for b in range(nbits):
        level[(j < r) & ((diff >> b) == 1)] = b + 1
    return np.concatenate(mats, 0).astype(np.float32), level, nbits


def _hgrn_kernel(c, nbits, hb, kd, q_ref, z_ref, i_ref, g_ref, lb_ref, gain_ref, ms_ref, lv_ref,
                 o_ref, so_ref, s_scr):
    ci = pl.program_id(2)

    @pl.when(ci == 0)
    def _():
        s_scr[...] = jnp.zeros_like(s_scr)

    lb = lb_ref[...]
    f = lb + (1.0 - lb) * jax.nn.sigmoid(z_ref[...])
    logf = jnp.log(f)
    k = 1.0 - f
    q = _silu(q_ref[...])
    v = i_ref[...].astype(BF16)
    e = _dot2_wide(ms_ref[...], logf)
    cum = e[0:c]
    qd = (q * jnp.exp(cum)).astype(BF16)
    kdec = k * jnp.exp(e[c:2 * c])
    whole_row = cum[c - 1:c]
    lv = lv_ref[...]
    masks = [lv == b for b in range(nbits + 1)]
    qb, kb = q.astype(BF16), k.astype(BF16)
    levels = [(qb, kb)]
    for b in range(nbits):
        gl = jnp.exp(e[(2 + b) * c:(3 + b) * c]).astype(BF16)
        levels.append((qb * gl, kb * gl))
    gate = gain_ref[...] * _silu(g_ref[...])

    for h in range(hb):
        sl = slice(h * kd, (h + 1) * kd)
        s = s_scr[h]
        o = _dot(qd[:, sl], s.astype(BF16))
        att = jnp.zeros((c, c), F32)
        for mask, (ql, kl) in zip(masks, levels):
            att = jnp.where(mask, _dot_nt(ql[:, sl], kl[:, sl]), att)
        o = o + _dot(att.astype(BF16), v[:, sl])
        whole = jnp.exp(jnp.broadcast_to(whole_row[:, sl], (s.shape[1], s.shape[0])).T)
        s_scr[h] = whole * s + _dot(kdec[:, sl].T.astype(BF16), v[:, sl])
        o = o * lax.rsqrt(jnp.mean(o * o, axis=-1, keepdims=True) + EPS)
        o_ref[:, sl] = (o * gate[:, sl]).astype(BF16)

    @pl.when(ci == pl.num_programs(2) - 1)
    def _():
        so_ref[0] = s_scr[...]


def _hgrn_scan(st, proj, lb, gain, heads, kd, vd):
    width = heads * kd
    c = min(HGRN_CHUNK, st.seq)
    hb = min(HGRN_HEADS_PER_STEP, heads)
    n_chunks = st.seq // c
    groups = heads // hb
    mstack, level, nbits = _hgrn_tables(c)

    def seg(j):
        return pl.BlockSpec((c, hb * kd), lambda b, h, ci: (b * n_chunks + ci, j * groups + h))

    head_row = pl.BlockSpec((1, hb * kd), lambda b, h, ci: (0, h))
    return pl.pallas_call(
        functools.partial(_hgrn_kernel, c, nbits, hb, kd),
        grid=(st.batch, groups, n_chunks),
        in_specs=[seg(0), seg(1), seg(2), seg(3), head_row, head_row,
                  pl.BlockSpec((mstack.shape[0], 2 * c), lambda b, h, ci: (0, 0)),
                  pl.BlockSpec(level.shape, lambda b, h, ci: (0, 0))],
        out_specs=[pl.BlockSpec((c, hb * vd), lambda b, h, ci: (b * n_chunks + ci, h)),
                   pl.BlockSpec((1, hb, kd, vd), lambda b, h, ci: (b, h, 0, 0))],
        out_shape=[jax.ShapeDtypeStruct((st.n, heads * vd), BF16),
                   jax.ShapeDtypeStruct((st.batch, heads, kd, vd), F32)],
        scratch_shapes=[pltpu.VMEM((hb, kd, vd), F32)],
        compiler_params=_cparams("parallel", "parallel", "arbitrary"),
        name="hgrn_scan",
    )(proj, proj, proj, proj, lb.reshape(1, width), gain.reshape(1, width),
      jnp.asarray(np.concatenate([mstack, mstack], 1), BF16), jnp.asarray(level))


def _step_tables(bb, t):
    rows = bb * t
    r = np.arange(rows)
    grp, pos = r // t, r % t
    same = grp[:, None] == grp[None, :]
    i, j = r[:, None], r[None, :]
    sums = [same & (j <= i), same & (j > i), same]
    for d in range(1, t):
        sums.append(same & (j <= i) & (j > i - d) & (pos[:, None] >= d))
    shifts = [same & (j == i - d) for d in range(1, t)]
    return (np.concatenate(sums, 0).astype(np.float32), np.concatenate(shifts, 0).astype(np.float32))


def _hgrn_step_kernel(bb, t, heads, kd, q_ref, z_ref, i_ref, g_ref, lb_ref, gain_ref, ms_ref, sh_ref,
                      seg_ref, segt_ref, s_ref, *rest):
    o_ref, so_ref = rest[-2:]
    rows = bb * t
    lb = lb_ref[...]
    f = lb + (1.0 - lb) * jax.nn.sigmoid(z_ref[...])
    logf = jnp.log(f)
    k = 1.0 - f
    q = _silu(q_ref[...])
    v = i_ref[...]
    e = _dot2(ms_ref[...], logf)
    qd = (q * jnp.exp(e[0:rows])).astype(BF16)
    kdec = k * jnp.exp(e[rows:2 * rows])
    whole = jnp.exp(e[2 * rows:3 * rows])

    seg, segt = seg_ref[...], segt_ref[...]
    o = jnp.zeros_like(q)
    for d in range(t):
        if d == 0:
            ks, vs, w = k, v, q * k
        else:
            sh = sh_ref[(d - 1) * rows:d * rows]
            ks, vs = _dot2(sh, k), _dot2(sh, v)
            w = q * ks * jnp.exp(e[(2 + d) * rows:(3 + d) * rows])
        att = _dot2_lhs(w, seg)
        o = o + _dot2_lhs(att, segt) * vs

    gate = gain_ref[...] * _silu(g_ref[...])
    vb = v.astype(BF16)
    row_seq = lax.broadcasted_iota(jnp.int32, (rows, kd), 0) // t
    lane_seq = lax.broadcasted_iota(jnp.int32, (kd, rows), 1) // t
    for h in range(heads):
        sl = slice(h * kd, (h + 1) * kd)
        k_t = kdec[:, sl].T
        w_t = whole[:, sl].T
        o_h = o[:, sl]
        for b in range(bb):
            s = s_ref[0, b, h]
            o_h = o_h + jnp.where(row_seq == b, _dot(qd[:, sl], s.astype(BF16)), 0.0)
            k_b = jnp.where(lane_seq == b, k_t, 0.0).astype(BF16)
            so_ref[0, b, h] = w_t[:, b * t:b * t + 1] * s + _dot(k_b, vb[:, sl])
        o_h = o_h * lax.rsqrt(jnp.mean(o_h * o_h, axis=-1, keepdims=True) + EPS)
        o_ref[:, sl] = (o_h * gate[:, sl]).astype(BF16)


def _hgrn_step(st, proj, lb, gain, state_all, j, o_stream, so_prev):
    _, _, heads, kd, vd = state_all.shape
    width = heads * kd
    t = st.dec_seq
    bb = BF16_SUBLANES // math.gcd(BF16_SUBLANES, t)
    rows = bb * t
    assert st.dec_batch % bb == 0 and st.n_prompt % rows == 0 and heads <= LANES
    rb0 = st.n_prompt // rows
    sums, shifts = _step_tables(bb, t)
    seg = (np.arange(width)[:, None] // kd == np.arange(LANES)[None, :]).astype(np.float32)

    def col(jj):
        return pl.BlockSpec((rows, width), lambda i: (rb0 + i, jj))

    full = lambda a: pl.BlockSpec(a.shape, lambda i: (0,) * a.ndim)
    state = pl.BlockSpec((1, bb, heads, kd, vd), lambda i: (j, i, 0, 0, 0))
    anyspec = pl.BlockSpec(memory_space=pl.ANY)
    args = [proj, proj, proj, proj, lb.reshape(1, width), gain.reshape(1, width),
            jnp.asarray(sums, BF16), jnp.asarray(shifts, BF16), jnp.asarray(seg, BF16),
            jnp.asarray(seg.T, BF16), state_all, o_stream]
    in_specs = [col(0), col(1), col(2), col(3), pl.BlockSpec((1, width), lambda i: (0, 0)),
                pl.BlockSpec((1, width), lambda i: (0, 0)), full(sums), full(shifts), full(seg),
                full(seg.T), state, anyspec]
    aliases = {11: 0}
    if so_prev is not None:
        args.append(so_prev)
        in_specs.append(anyspec)
        aliases[12] = 1
    return pl.pallas_call(
        functools.partial(_hgrn_step_kernel, bb, t, heads, kd),
        grid=(st.dec_batch // bb,),
        in_specs=in_specs,
        out_specs=[pl.BlockSpec((rows, width), lambda i: (rb0 + i, 0)), state],
        out_shape=[jax.ShapeDtypeStruct(o_stream.shape, BF16),
                   jax.ShapeDtypeStruct(state_all.shape, F32)],
        input_output_aliases=aliases,
        compiler_params=_cparams("parallel"),
        name="hgrn_step",
    )(*args)


def _rotate(x, cos, sin):
    half = x.shape[-1] // 2
    x1, x2 = x[:, :half], x[:, half:]
    return jnp.concatenate([x1 * cos - x2 * sin, x1 * sin + x2 * cos], axis=-1)


def _group_norm_gate(o, gain, g):
    mu = jnp.mean(o, axis=-1, keepdims=True)
    var = jnp.mean(jnp.square(o - mu), axis=-1, keepdims=True)
    return (_silu(g) * ((o - mu) * lax.rsqrt(var + EPS) * gain)).astype(BF16)


def _ret_kernel(c, hb, dk, dv, lg_ref, q_ref, k_ref, v_ref, g_ref, cos_ref, sin_ref, gain_ref, o_ref, so_ref,
                s_scr, intra_scr):
    ci = pl.program_id(2)
    t_col = lax.broadcasted_iota(jnp.int32, (c, 1), 0).astype(F32)
    lgs = [lg_ref[pl.program_id(1) * hb + h] for h in range(hb)]

    @pl.when(ci == 0)
    def _():
        s_scr[...] = jnp.zeros_like(s_scr)
        rel = (lax.broadcasted_iota(jnp.int32, (c, c), 0)
               - lax.broadcasted_iota(jnp.int32, (c, c), 1)).astype(F32)
        for h, lg in enumerate(lgs):
            intra_scr[h] = jnp.where(rel >= 0.0, jnp.exp(jnp.maximum(rel, 0.0) * lg), 0.0)

    cos, sin = cos_ref[...], sin_ref[...]
    for h, lg in enumerate(lgs):
        q = _rotate(q_ref[:, h * dk:(h + 1) * dk], cos, sin)
        k = _rotate(k_ref[:, h * dk:(h + 1) * dk], cos, sin) * (dk ** -0.5)
        v = v_ref[:, h * dv:(h + 1) * dv].astype(BF16)
        inter = jnp.exp((t_col + 1.0) * lg)
        tail = jnp.exp((c - 1.0 - t_col) * lg)
        whole = jnp.exp(jnp.full((1, 1), c, F32) * lg)
        s = s_scr[h]
        att = _dot_nt(q.astype(BF16), k.astype(BF16)) * intra_scr[h]
        o = _dot(att.astype(BF16), v) + _dot((q * inter).astype(BF16), s.astype(BF16))
        s_scr[h] = whole * s + _dot((k * tail).T.astype(BF16), v)
        o_ref[:, h * dv:(h + 1) * dv] = _group_norm_gate(
            o, gain_ref[:, h * dv:(h + 1) * dv], g_ref[:, h * dv:(h + 1) * dv])

    @pl.when(ci == pl.num_programs(2) - 1)
    def _():
        so_ref[0] = s_scr[...]


def _ret_scan(st, proj, cos, sin, log_gamma, gain, heads, dk, dv):
    c = min(RET_CHUNK, st.seq)
    hb = min(RET_HEADS_PER_STEP, heads)
    groups = heads // hb
    n_chunks = st.seq // c
    v_off = 2 * heads * dk // (hb * dv)
    g_off = v_off + groups

    def qk(j):
        return pl.BlockSpec((c, hb * dk), lambda b, h, ci: (b * n_chunks + ci, j * groups + h))

    def vg(off):
        return pl.BlockSpec((c, hb * dv), lambda b, h, ci: (b * n_chunks + ci, off + h))

    rope = pl.BlockSpec((c, dk // 2), lambda b, h, ci: (ci, 0))
    return pl.pallas_call(
        functools.partial(_ret_kernel, c, hb, dk, dv),
        grid=(st.batch, groups, n_chunks),
        in_specs=[pl.BlockSpec(memory_space=pltpu.SMEM),
                  qk(0), qk(1), vg(v_off), vg(g_off), rope, rope,
                  pl.BlockSpec((1, hb * dv), lambda b, h, ci: (0, h))],
        out_specs=[pl.BlockSpec((c, hb * dv), lambda b, h, ci: (b * n_chunks + ci, h)),
                   pl.BlockSpec((1, hb, dk, dv), lambda b, h, ci: (b, h, 0, 0))],
        out_shape=[jax.ShapeDtypeStruct((st.n, heads * dv), BF16),
                   jax.ShapeDtypeStruct((st.batch, heads, dk, dv), F32)],
        scratch_shapes=[pltpu.VMEM((hb, dk, dv), F32), pltpu.VMEM((hb, c, c), F32)],
        compiler_params=_cparams("parallel", "parallel", "arbitrary"),
        name="ret_scan",
    )(log_gamma, proj, proj, proj, proj, cos, sin, gain.reshape(1, heads * dv))


def _ret_step_kernel(bb, t, hb, dk, dv, lg_ref, q_ref, k_ref, v_ref, g_ref, cos_ref, sin_ref, pos_ref, rel_ref,
                     gain_ref, s_ref, *rest):
    o_ref, so_ref = rest[-2:]
    rows = bb * t
    cos, sin = cos_ref[...], sin_ref[...]
    t_col = pos_ref[...][:, :1]
    rel = rel_ref[...]
    row_seq = lax.broadcasted_iota(jnp.int32, (rows, dv), 0) // t
    lane_seq = lax.broadcasted_iota(jnp.int32, (dk, rows), 1) // t
    for h in range(hb):
        lg = lg_ref[pl.program_id(1) * hb + h]
        q = _rotate(q_ref[:, h * dk:(h + 1) * dk], cos, sin)
        k = _rotate(k_ref[:, h * dk:(h + 1) * dk], cos, sin) * (dk ** -0.5)
        v = v_ref[:, h * dv:(h + 1) * dv].astype(BF16)
        inter = jnp.exp((t_col + 1.0) * lg)
        tail = jnp.exp((t - 1.0 - t_col) * lg)
        intra = jnp.where(rel >= 0.0, jnp.exp(jnp.maximum(rel, 0.0) * lg), 0.0)
        whole = jnp.exp(jnp.full((1, 1), t, F32) * lg)

        att = _dot_nt(q.astype(BF16), k.astype(BF16)) * intra
        o = _dot(att.astype(BF16), v)
        qd = (q * inter).astype(BF16)
        k_t = (k * tail).T
        for b in range(bb):
            s = s_ref[0, b, h]
            o = o + jnp.where(row_seq == b, _dot(qd, s.astype(BF16)), 0.0)
            k_b = jnp.where(lane_seq == b, k_t, 0.0).astype(BF16)
            so_ref[0, b, h] = whole * s + _dot(k_b, v)
        o_ref[:, h * dv:(h + 1) * dv] = _group_norm_gate(
            o, gain_ref[:, h * dv:(h + 1) * dv], g_ref[:, h * dv:(h + 1) * dv])


def _ret_step(st, proj, cos, sin, log_gamma, gain, state_all, j, o_stream, so_prev):
    _, _, heads, dk, dv = state_all.shape
    t = st.dec_seq
    bb = BF16_SUBLANES // math.gcd(BF16_SUBLANES, t)
    rows = bb * t
    assert st.dec_batch % bb == 0 and st.n_prompt % rows == 0
    hb = min(RET_HEADS_PER_STEP, heads)
    groups = heads // hb
    rb0 = st.n_prompt // rows
    v_off = 2 * heads * dk // (hb * dv)
    g_off = v_off + groups
    r = np.arange(rows)
    same = (r[:, None] // t) == (r[None, :] // t)
    rel = np.where(same & (r[None, :] <= r[:, None]), r[:, None] - r[None, :], -1).astype(np.float32)
    pos = np.broadcast_to((r % t).astype(np.float32)[:, None], (rows, LANES))

    full = lambda a: pl.BlockSpec(a.shape, lambda i, h: (0,) * a.ndim)
    state = pl.BlockSpec((1, bb, hb, dk, dv), lambda i, h: (j, i, h, 0, 0))
    anyspec = pl.BlockSpec(memory_space=pl.ANY)
    cos_r, sin_r = jnp.tile(cos, (bb, 1)), jnp.tile(sin, (bb, 1))
    args = [log_gamma, proj, proj, proj, proj, cos_r, sin_r, jnp.asarray(pos), jnp.asarray(rel),
            gain.reshape(1, heads * dv), state_all, o_stream]
    in_specs = [pl.BlockSpec(memory_space=pltpu.SMEM),
                pl.BlockSpec((rows, hb * dk), lambda i, h: (rb0 + i, h)),
                pl.BlockSpec((rows, hb * dk), lambda i, h: (rb0 + i, groups + h)),
                pl.BlockSpec((rows, hb * dv), lambda i, h: (rb0 + i, v_off + h)),
                pl.BlockSpec((rows, hb * dv), lambda i, h: (rb0 + i, g_off + h)),
                full(cos_r), full(sin_r), full(pos), full(rel),
                pl.BlockSpec((1, hb * dv), lambda i, h: (0, h)), state, anyspec]
    aliases = {11: 0}
    if so_prev is not None:
        args.append(so_prev)
        in_specs.append(anyspec)
        aliases[12] = 1
    return pl.pallas_call(
        functools.partial(_ret_step_kernel, bb, t, hb, dk, dv),
        grid=(st.dec_batch // bb, groups),
        in_specs=in_specs,
        out_specs=[pl.BlockSpec((rows, hb * dv), lambda i, h: (rb0 + i, h)), state],
        out_shape=[jax.ShapeDtypeStruct(o_stream.shape, BF16),
                   jax.ShapeDtypeStruct(state_all.shape, F32)],
        input_output_aliases=aliases,
        compiler_params=_cparams("parallel", "parallel"),
        name="ret_step",
    )(*args)


def _rope_tables(pos, half):
    inv = 1.0 / (ROPE_BASE ** jnp.linspace(0.0, 1.0, half, dtype=F32))
    ang = pos[:, None] * inv[None]
    return jnp.cos(ang), jnp.sin(ang)


def _expert_kernel(ts_ref, na_ref, tk_ref, eseq_ref, nk_ref, x_ref, w_ref, wg_hbm, wu_hbm, wd_hbm, o_ref,
                   wgb, wub, wdb, stage_g, stage_u, stage_d, sems):
    t = pl.program_id(0)
    active = t < na_ref[0]
    k = tk_ref[t]
    new_expert = (t == 0) | (k != tk_ref[jnp.maximum(t - 1, 0)])

    def weight_copies(kk, slot):
        e = eseq_ref[kk]
        return (pltpu.make_async_copy(wg_hbm.at[e], stage_g.at[slot], sems.at[0, slot]),
                pltpu.make_async_copy(wu_hbm.at[e], stage_u.at[slot], sems.at[1, slot]),
                pltpu.make_async_copy(wd_hbm.at[e], stage_d.at[slot], sems.at[2, slot]))

    @pl.when(t == 0)
    def _():
        for cp in weight_copies(0, 0):
            cp.start()

    @pl.when(active & new_expert)
    def _():
        slot = lax.rem(k, 2)

        @pl.when(k + 1 < nk_ref[0])
        def _():
            for cp in weight_copies(k + 1, 1 - slot):
                cp.start()

        for cp in weight_copies(k, slot):
            cp.wait()
        wgb[...] = stage_g[slot].astype(BF16)
        wub[...] = stage_u[slot].astype(BF16)
        wdb[...] = stage_d[slot].astype(BF16)

    @pl.when(active)
    def _():
        x = x_ref[...].astype(BF16)
        hg = _silu(_dot(x, wgb[...])) * _dot(x, wub[...])
        o_ref[...] = _dot((hg * w_ref[...]).astype(BF16), wdb[...])

    @pl.when(jnp.logical_not(active))
    def _():
        o_ref[...] = jnp.zeros_like(o_ref)


def _experts(x_sorted, w_sorted, tile_src, n_active, tile_pos, expert_seq, n_used, w_gate, w_up, w_down):
    rows = x_sorted.shape[0]
    _, d, ff = w_gate.shape
    tm = EXPERT_TILE
    hbm = pl.BlockSpec(memory_space=pl.ANY)
    grid_spec = pltpu.PrefetchScalarGridSpec(
        num_scalar_prefetch=5,
        grid=(rows // tm,),
        in_specs=[pl.BlockSpec((tm, d), lambda t, ts, *_: (ts[t], 0)),
                  pl.BlockSpec((tm, 1), lambda t, ts, *_: (ts[t], 0)),
                  hbm, hbm, hbm],
        out_specs=pl.BlockSpec((tm, d), lambda t, *_: (t, 0)),
        scratch_shapes=[pltpu.VMEM((d, ff), BF16), pltpu.VMEM((d, ff), BF16), pltpu.VMEM((ff, d), BF16),
                        pltpu.VMEM((2, d, ff), F32), pltpu.VMEM((2, d, ff), F32), pltpu.VMEM((2, ff, d), F32),
                        pltpu.SemaphoreType.DMA((3, 2))],
    )
    return pl.pallas_call(
        _expert_kernel,
        grid_spec=grid_spec,
        out_shape=jax.ShapeDtypeStruct((rows, d), F32),
        compiler_params=_cparams("arbitrary"),
        name="experts",
    )(tile_src, n_active, tile_pos, expert_seq, n_used, x_sorted, w_sorted, w_gate, w_up, w_down)


def _take_rows(a, idx):
    return a.at[idx].get(mode="promise_in_bounds")


def _step_lookup(table, starts, x):
    jumps = table[1:] - table[:-1]
    return table[0] + jnp.sum(jnp.where(x[None, :] >= starts[1:, None], jumps[:, None], 0), axis=0)


def _dispatch_plan(eid, ew, n_experts_total):
    n, top_k = eid.shape
    tm = EXPERT_TILE
    a = n * top_k
    n_tiles = a // tm + n_experts_total
    rows = n_tiles * tm
    i32 = jnp.int32
    q = jnp.arange(a, dtype=i32)
    sorted_e, order, sorted_w = lax.sort((eid.reshape(a), q, ew.reshape(a)), num_keys=1, is_stable=True)
    bounds = jnp.searchsorted(sorted_e, jnp.arange(n_experts_total + 1, dtype=i32), side="left").astype(i32)
    off, end = bounds[:-1], bounds[1:]
    padded = (end - off + tm - 1) // tm * tm
    pad_end = jnp.cumsum(padded)
    pad_off = pad_end - padded
    delta = pad_off - off
    _, pos = lax.sort((order, q + _step_lookup(delta, off, q)), num_keys=1)
    n_active = (pad_end[-1] // tm).astype(i32)
    tile_src = jnp.minimum(jnp.arange(n_tiles, dtype=i32), n_active - 1)
    experts = jnp.arange(n_experts_total, dtype=i32)
    used = padded > 0
    n_used = jnp.sum(used, dtype=i32)
    expert_seq = jnp.minimum(jnp.sort(jnp.where(used, experts, n_experts_total)), n_experts_total - 1)
    tile_pos = _step_lookup(jnp.cumsum(used, dtype=i32) - 1, pad_off, tile_src * tm)
    p = jnp.arange(rows, dtype=i32)
    src = p - _step_lookup(delta, pad_off, p)
    valid = (src < _step_lookup(end, pad_off, p)) & (p < pad_end[-1])
    src = jnp.clip(src, 0, a - 1)
    row_token = jnp.where(valid, _take_rows(order, src) // top_k, p % n)
    row_w = jnp.where(valid, _take_rows(sorted_w, src), 0.0)
    return (row_token, row_w.reshape(rows, 1), pos.reshape(n, top_k), tile_src, n_active.reshape(1),
            tile_pos, expert_seq, n_used.reshape(1))


def _combine_kernel(n_prompt_tiles, x_ref, y_ref, gr, gt, shr, sht, scr, sct, o_ref, hn_ref):
    is_s = pl.program_id(0) >= n_prompt_tiles
    d = x_ref.shape[1]
    x = x_ref[...] + _pick(is_s, gr, gt) * (y_ref[:, :d] + y_ref[:, d:])
    o_ref[...] = x
    hn_ref[...] = (_rms(x) * (1.0 + _pick(is_s, scr, sct)) + _pick(is_s, shr, sht)).astype(BF16)


def _combine_final_kernel(n_prompt_tiles, x_ref, y_ref, gr, gt, fg_ref, o_ref):
    d = x_ref.shape[1]
    x = x_ref[...] + _pick(pl.program_id(0) >= n_prompt_tiles, gr, gt) * (y_ref[:, :d] + y_ref[:, d:])
    o_ref[...] = _rms(x) * fg_ref[...]


def _combine(st, x, y_pairs, mod_row, mod_tok, layer, j_gate, final_gain, final):
    d = st.d
    tile = pl.BlockSpec((st.tm, d), lambda i: (i, 0))
    common = [tile, pl.BlockSpec((st.tm, 2 * d), lambda i: (i, 0)), *st.mod_specs(layer, d, lambda i: j_gate)]
    if final:
        return pl.pallas_call(
            functools.partial(_combine_final_kernel, st.n_prompt_tiles),
            grid=(st.n_tiles,),
            in_specs=[*common, pl.BlockSpec((1, d), lambda i: (0, 0))],
            out_specs=tile,
            out_shape=jax.ShapeDtypeStruct((st.n, d), F32),
            compiler_params=_cparams("parallel"),
            name="moe_combine_final",
        )(x, y_pairs, mod_row, mod_tok, final_gain.reshape(1, d)), None
    return pl.pallas_call(
        functools.partial(_combine_kernel, st.n_prompt_tiles),
        grid=(st.n_tiles,),
        in_specs=[*common, *st.mod_specs(layer + 1, d, lambda i: 0), *st.mod_specs(layer + 1, d, lambda i: 1)],
        out_specs=[tile, tile],
        out_shape=[jax.ShapeDtypeStruct((st.n, d), F32), jax.ShapeDtypeStruct((st.n, d), BF16)],
        compiler_params=_cparams("parallel"),
        name="moe_combine",
    )(x, y_pairs, mod_row, mod_tok, mod_row, mod_tok, mod_row, mod_tok)


def kernel(x_prompt, x_sample, state_hgrn, state_ret, c_prompt, c_sample, ada_w, ada_b, hgrn_w_in, hgrn_lb_logits, hgrn_norm_gain, hgrn_w_out, ret_w_in, ret_norm_gain, ret_w_out, moe_router_group, moe_router_expert, moe_w_gate, moe_w_up, moe_w_down, final_norm_gain):
    batch, seq, d = x_prompt.shape
    dec_batch, dec_seq, _ = x_sample.shape
    depth = ada_w.shape[0]
    _, _, a_heads, a_kd, a_vd = state_hgrn.shape
    _, _, b_heads, b_dk, b_dv = state_ret.shape
    _, n_groups, _, n_exp = moe_router_expert.shape
    ff = moe_w_gate.shape[-1]
    n_exp_total = n_groups * n_exp
    st = _Stream(batch, seq, dec_batch, dec_seq, d)

    p = jax.nn.softmax(hgrn_lb_logits.astype(F32), axis=0)
    lb_all = jnp.cumsum(p, axis=0) - p[0:1]
    log_gamma = jnp.log1p(-jnp.exp2(-5.0 - jnp.arange(b_heads, dtype=F32)))
    cos_p, sin_p = _rope_tables(jnp.arange(seq, dtype=F32), b_dk // 2)
    cos_s, sin_s = _rope_tables(jnp.arange(dec_seq, dtype=F32) + PAST_LEN, b_dk // 2)
    w_gate = moe_w_gate.reshape(depth * n_exp_total, d, ff)
    w_up = moe_w_up.reshape(depth * n_exp_total, d, ff)
    w_down = moe_w_down.reshape(depth * n_exp_total, ff, d)

    n_c = batch + dec_batch
    c_rows = -(-n_c // BF16_SUBLANES) * BF16_SUBLANES
    c_all = jnp.pad(jnp.concatenate([c_prompt, c_sample], 0), ((0, c_rows - n_c), (0, 0)))
    mod = _ada_mod(c_all, ada_w, ada_b)
    mod_row = mod[:, :batch].reshape(depth * batch, 1, 6 * d)
    mod_tok = jnp.repeat(mod[:, batch:n_c], dec_seq, axis=1)

    x = jnp.concatenate([x_prompt.reshape(-1, d), x_sample.reshape(-1, d)], 0)
    hg_p, rt_p, hg_s, rt_s = [], [], None, None
    hn = _norm_mod(st, x, mod_row, mod_tok, 0, 0, 1)
    for l in range(depth):
        j = l // 2
        if l % 2 == 0:
            proj = _proj(st, hn, hgrn_w_in, j)
            o, s_p = _hgrn_scan(st, proj, lb_all[j], hgrn_norm_gain[j], a_heads, a_kd, a_vd)
            o, hg_s = _hgrn_step(st, proj, lb_all[j], hgrn_norm_gain[j], state_hgrn, j, o, hg_s)
            hg_p.append(s_p)
            w_out = hgrn_w_out
        else:
            proj = _proj(st, hn, ret_w_in, j)
            o, s_p = _ret_scan(st, proj, cos_p, sin_p, log_gamma, ret_norm_gain[j], b_heads, b_dk, b_dv)
            o, rt_s = _ret_step(st, proj, cos_s, sin_s, log_gamma, ret_norm_gain[j], state_ret, j, o, rt_s)
            rt_p.append(s_p)
            w_out = ret_w_out
        x = _proj_residual(st, o, w_out, j, x, mod_row, mod_tok, l, 2)

        w_router = jnp.concatenate(
            [moe_router_group[l], jnp.moveaxis(moe_router_expert[l], 0, 1).reshape(d, n_exp_total)], 1)
        w_router = jnp.pad(w_router, ((0, 0), (0, LANES - w_router.shape[1])))
        hx, eid, ew = _norm_mod_router(st, x, mod_row, mod_tok, l, 3, 4, w_router, n_groups, n_exp)
        row_token, row_w, pos, tile_src, n_active, tile_pos, expert_seq, n_used = _dispatch_plan(
            eid, ew, n_exp_total)
        y_sorted = _experts(_take_rows(hx, row_token), row_w, tile_src, n_active, tile_pos,
                            expert_seq + l * n_exp_total, n_used, w_gate, w_up, w_down)
        y_pairs = _take_rows(y_sorted, pos.reshape(-1)).reshape(st.n, 2 * d)
        x, hn = _combine(st, x, y_pairs, mod_row, mod_tok, l, 5, final_norm_gain, l == depth - 1)

    y_prompt = x[:st.n_prompt].reshape(batch, seq, d)
    y_sample = x[st.n_prompt:].reshape(dec_batch, dec_seq, d)
    return (y_prompt, y_sample, jnp.stack(hg_p), jnp.stack(rt_p), hg_s, rt_s)
```

```python
import functools
import math

import numpy as np
import jax
import jax.numpy as jnp
from jax import lax
from jax.experimental import pallas as pl
from jax.experimental.pallas import tpu as pltpu

F32 = jnp.float32
BF16 = jnp.bfloat16
EPS = 1e-6
PAST_LEN = 16384
ROPE_BASE = 10000.0

V7X_VMEM_LIMIT_BYTES = 56 * 1024 * 1024
LANES = 128
BF16_SUBLANES = 16

HGRN_CHUNK = 128
HGRN_HEADS_PER_STEP = 8
HGRN_HEADS_PER_GROUP = 8
RET_CHUNK = 256
RET_HEADS_PER_STEP = 2
EXPERT_TILE = 256
PROJ_TN = 1024
PROJ_IN_ROW_TILES = 8
ADA_TN = 1024


def _cparams(*sem):
    return pltpu.CompilerParams(dimension_semantics=sem, vmem_limit_bytes=V7X_VMEM_LIMIT_BYTES)


def _dot(a, b):
    return jnp.dot(a, b, preferred_element_type=F32)


def _dot_nt(a, b):
    return lax.dot_general(a, b, (((1,), (1,)), ((), ())), preferred_element_type=F32)


def _dot2(m, x):
    hi = x.astype(BF16)
    lo = (x - hi.astype(F32)).astype(BF16)
    return _dot(m, hi) + _dot(m, lo)


def _dot2_wide(mm, x):
    hi = x.astype(BF16)
    lo = (x - hi.astype(F32)).astype(BF16)
    return _dot(mm, jnp.concatenate([hi, lo], axis=0))


def _dot2_lhs(x, m):
    hi = x.astype(BF16)
    lo = (x - hi.astype(F32)).astype(BF16)
    return _dot(hi, m) + _dot(lo, m)


def _silu(x):
    return x * jax.nn.sigmoid(x)


def _rms(x):
    return x * lax.rsqrt(jnp.mean(x * x, axis=-1, keepdims=True) + EPS)


def _ada_kernel(cr_ref, ct_ref, w_ref, b_ref, or_ref, ot_ref):
    w = w_ref[0].astype(BF16)
    or_ref[0] = _dot(_silu(cr_ref[...]).astype(BF16), w) + b_ref[0]
    ot_ref[0] = _dot(_silu(ct_ref[...]).astype(BF16), w) + b_ref[0]


def _ada_mod(c_row, c_tok, ada_w, ada_b):
    depth, d, n6 = ada_w.shape
    r, t = c_row.shape[0], c_tok.shape[0]
    tn = min(ADA_TN, n6)
    return pl.pallas_call(
        _ada_kernel,
        grid=(depth, n6 // tn),
        in_specs=[pl.BlockSpec((r, d), lambda l, n: (0, 0)),
                  pl.BlockSpec((t, d), lambda l, n: (0, 0)),
                  pl.BlockSpec((1, d, tn), lambda l, n: (l, 0, n)),
                  pl.BlockSpec((1, 1, tn), lambda l, n: (l, 0, n))],
        out_specs=[pl.BlockSpec((1, r, tn), lambda l, n: (l, 0, n)),
                   pl.BlockSpec((1, t, tn), lambda l, n: (l, 0, n))],
        out_shape=[jax.ShapeDtypeStruct((depth, r, n6), F32), jax.ShapeDtypeStruct((depth, t, n6), F32)],
        compiler_params=_cparams("parallel", "parallel"),
        name="ada_mod",
    )(c_row, c_tok, ada_w, ada_b.reshape(depth, 1, n6))


class _Stream:
    def __init__(self, batch, seq, dec_batch, dec_seq, d):
        self.batch, self.seq, self.dec_batch, self.dec_seq, self.d = batch, seq, dec_batch, dec_seq, d
        self.n_prompt = batch * seq
        self.n_sample = dec_batch * dec_seq
        self.tm = self.n_sample
        assert seq % self.tm == 0 and self.tm % BF16_SUBLANES == 0
        self.tiles_per_seq = seq // self.tm
        self.n_prompt_tiles = self.n_prompt // self.tm
        self.n_tiles = self.n_prompt_tiles + 1
        self.n = self.n_prompt + self.n_sample

    def mod_specs(self, layer, width, col_block, tile_axis=0):
        tps, last, batch = self.tiles_per_seq, self.batch - 1, self.batch

        def row_map(*idx):
            return (layer * batch + jnp.minimum(idx[tile_axis] // tps, last), 0, col_block(*idx))
        return (pl.BlockSpec((1, 1, width), row_map),
                pl.BlockSpec((1, self.tm, width), lambda *idx: (layer, 0, col_block(*idx))))


def _pick(is_sample, row_ref, tok_ref):
    return jnp.where(is_sample, tok_ref[0], row_ref[0])


def _norm_mod_kernel(n_prompt_tiles, x_ref, shr, sht, scr, sct, o_ref):
    is_s = pl.program_id(0) >= n_prompt_tiles
    h = _rms(x_ref[...]) * (1.0 + _pick(is_s, scr, sct)) + _pick(is_s, shr, sht)
    o_ref[...] = h.astype(BF16)


def _norm_mod(st, x, mod_row, mod_tok, layer, j_shift, j_scale):
    d = st.d
    return pl.pallas_call(
        functools.partial(_norm_mod_kernel, st.n_prompt_tiles),
        grid=(st.n_tiles,),
        in_specs=[pl.BlockSpec((st.tm, d), lambda i: (i, 0)),
                  *st.mod_specs(layer, d, lambda i: j_shift), *st.mod_specs(layer, d, lambda i: j_scale)],
        out_specs=pl.BlockSpec((st.tm, d), lambda i: (i, 0)),
        out_shape=jax.ShapeDtypeStruct((st.n, d), BF16),
        compiler_params=_cparams("parallel"),
        name="norm_mod",
    )(x, mod_row, mod_tok, mod_row, mod_tok)


def _norm_mod_router_kernel(n_prompt_tiles, n_groups, n_experts, x_ref, shr, sht, scr, sct, wr_ref,
                            o_ref, eid_ref, ew_ref):
    is_s = pl.program_id(0) >= n_prompt_tiles
    h = _rms(x_ref[...]) * (1.0 + _pick(is_s, scr, sct)) + _pick(is_s, shr, sht)
    o_ref[...] = h

    logits = _dot(h.astype(BF16), wr_ref[...].astype(BF16))
    tm = logits.shape[0]
    lane = lax.broadcasted_iota(jnp.int32, (tm, LANES), 1)
    neg_inf = jnp.float32(-jnp.inf)

    def first_index_of_max(vals, vmax):
        return jnp.min(jnp.where(vals == vmax, lane, LANES), axis=1, keepdims=True)

    is_g = lane < n_groups
    lg = jnp.where(is_g, logits, neg_inf)
    mg = jnp.max(lg, axis=1, keepdims=True)
    eg = jnp.exp(lg - mg)
    pg = eg / jnp.sum(eg, axis=1, keepdims=True)
    gi = first_index_of_max(lg, mg)
    wg = jnp.sum(jnp.where(lane == gi, pg, 0.0), axis=1, keepdims=True)

    lo = n_groups + gi * n_experts
    sel = (lane >= lo) & (lane < lo + n_experts)
    le = jnp.where(sel, logits, neg_inf)
    me = jnp.max(le, axis=1, keepdims=True)
    ee = jnp.exp(le - me)
    pe = jnp.where(sel, ee / jnp.sum(ee, axis=1, keepdims=True), -1.0)
    p1 = jnp.max(pe, axis=1, keepdims=True)
    i1 = first_index_of_max(pe, p1)
    pe2 = jnp.where(lane == i1, -1.0, pe)
    p2 = jnp.max(pe2, axis=1, keepdims=True)
    i2 = first_index_of_max(pe2, p2)
    denom = p1 + p2
    col = lax.broadcasted_iota(jnp.int32, (tm, 2), 1)
    eid_ref[...] = jnp.where(col == 0, i1, i2) - n_groups
    ew_ref[...] = wg * jnp.where(col == 0, p1 / denom, p2 / denom)


def _norm_mod_router(st, x, mod_row, mod_tok, layer, j_shift, j_scale, w_router, n_groups, n_experts):
    d = st.d
    return pl.pallas_call(
        functools.partial(_norm_mod_router_kernel, st.n_prompt_tiles, n_groups, n_experts),
        grid=(st.n_tiles,),
        in_specs=[pl.BlockSpec((st.tm, d), lambda i: (i, 0)),
                  *st.mod_specs(layer, d, lambda i: j_shift), *st.mod_specs(layer, d, lambda i: j_scale),
                  pl.BlockSpec((d, LANES), lambda i: (0, 0))],
        out_specs=[pl.BlockSpec((st.tm, d), lambda i: (i, 0)),
                   pl.BlockSpec((st.tm, 2), lambda i: (i, 0)),
                   pl.BlockSpec((st.tm, 2), lambda i: (i, 0))],
        out_shape=[jax.ShapeDtypeStruct((st.n, d), F32),
                   jax.ShapeDtypeStruct((st.n, 2), jnp.int32),
                   jax.ShapeDtypeStruct((st.n, 2), F32)],
        compiler_params=_cparams("parallel"),
        name="norm_mod_router",
    )(x, mod_row, mod_tok, mod_row, mod_tok, w_router)


def _proj_kernel(a_ref, w_ref, o_ref, wb_ref):
    @pl.when(pl.program_id(1) == 0)
    def _():
        wb_ref[...] = w_ref[0].astype(BF16)
    o_ref[...] = _dot(a_ref[...], wb_ref[...])


def _proj(st, a, w_all, j):
    _, k, n_out = w_all.shape
    tn = min(PROJ_TN, n_out)
    tm = st.n // PROJ_IN_ROW_TILES if st.n % (PROJ_IN_ROW_TILES * BF16_SUBLANES) == 0 else st.tm
    return pl.pallas_call(
        _proj_kernel,
        grid=(n_out // tn, st.n // tm),
        in_specs=[pl.BlockSpec((tm, k), lambda n, m: (m, 0)),
                  pl.BlockSpec((1, k, tn), lambda n, m: (j, 0, n))],
        out_specs=pl.BlockSpec((tm, tn), lambda n, m: (m, n)),
        out_shape=jax.ShapeDtypeStruct((st.n, n_out), F32),
        scratch_shapes=[pltpu.VMEM((k, tn), BF16)],
        compiler_params=_cparams("parallel", "arbitrary"),
        name="proj_in",
    )(a, w_all)


def _proj_res_kernel(n_prompt_tiles, a_ref, w_ref, x_ref, gr, gt, o_ref, wb_ref):
    @pl.when(pl.program_id(1) == 0)
    def _():
        wb_ref[...] = w_ref[0].astype(BF16)
    gate = _pick(pl.program_id(1) >= n_prompt_tiles, gr, gt)
    o_ref[...] = x_ref[...] + gate * _dot(a_ref[...], wb_ref[...])


def _proj_residual(st, a, w_all, j, x, mod_row, mod_tok, layer, j_gate):
    _, k, d = w_all.shape
    tn = min(PROJ_TN // 2, d)
    nb = d // tn
    return pl.pallas_call(
        functools.partial(_proj_res_kernel, st.n_prompt_tiles),
        grid=(nb, st.n_tiles),
        in_specs=[pl.BlockSpec((st.tm, k), lambda n, m: (m, 0)),
                  pl.BlockSpec((1, k, tn), lambda n, m: (j, 0, n)),
                  pl.BlockSpec((st.tm, tn), lambda n, m: (m, n)),
                  *st.mod_specs(layer, tn, lambda n, m: j_gate * nb + n, tile_axis=1)],
        out_specs=pl.BlockSpec((st.tm, tn), lambda n, m: (m, n)),
        out_shape=jax.ShapeDtypeStruct((st.n, d), F32),
        scratch_shapes=[pltpu.VMEM((k, tn), BF16)],
        compiler_params=_cparams("parallel", "arbitrary"),
        name="proj_out",
    )(a, w_all, x, mod_row, mod_tok)


def _hgrn_tables(c):
    nbits = int(math.log2(c))
    assert 1 << nbits == c
    r = np.arange(c)[:, None]
    j = np.arange(c)[None, :]
    mats = [(j <= r), (j > r)]
    for b in range(nbits):
        mid = ((r >> (b + 1)) << (b + 1)) + (1 << b)
        right = r >= mid
        mats.append(np.where(right, (j >= mid) & (j <= r), (j > r) & (j < mid)))
    level = np.full((c, c), -1, np.int32)
    level[r[:, 0], r[:, 0]] = 0
    diff = r ^ j
    for b in range(nbits):
        level[(j < r) & ((diff >> b) == 1)] = b + 1
    return np.concatenate(mats, 0).astype(np.float32), level, nbits


def _hgrn_kernel(c, nbits, hb, kd, q_ref, z_ref, i_ref, g_ref, lb_ref, gain_ref, ms_ref, lv_ref,
                 o_ref, so_ref, s_scr):
    ci = pl.program_id(2)

    @pl.when(ci == 0)
    def _():
        s_scr[...] = jnp.zeros_like(s_scr)

    lv = lv_ref[...]
    hg = min(HGRN_HEADS_PER_GROUP, hb)
    for g0 in range(0, hb, hg):
        gs = slice(g0 * kd, (g0 + hg) * kd)
        lb = lb_ref[:, gs]
        f = lb + (1.0 - lb) * jax.nn.sigmoid(z_ref[:, gs])
        logf = jnp.log(f)
        k = 1.0 - f
        q = _silu(q_ref[:, gs])
        v = i_ref[:, gs].astype(BF16)
        gate = gain_ref[:, gs] * _silu(g_ref[:, gs])
        hi = logf.astype(BF16)
        hl = jnp.concatenate([hi, (logf - hi.astype(F32)).astype(BF16)], axis=0)

        def decay_sum(r):
            return _dot(ms_ref[r * c:(r + 1) * c, :], hl)

        cum = decay_sum(0)
        qd = (q * jnp.exp(cum)).astype(BF16)
        kdec = k * jnp.exp(decay_sum(1))
        qb, kb = q.astype(BF16), k.astype(BF16)
        heads = [slice(i * kd, (i + 1) * kd) for i in range(hg)]
        att = [jnp.where(lv == 0, _dot_nt(qb[:, sl], kb[:, sl]), 0.0) for sl in heads]
        for b in range(nbits):
            gl = jnp.exp(decay_sum(2 + b)).astype(BF16)
            ql, kl = qb * gl, kb * gl
            att = [jnp.where(lv == b + 1, _dot_nt(ql[:, sl], kl[:, sl]), a) for sl, a in zip(heads, att)]
        for i, sl in enumerate(heads):
            s = s_scr[g0 + i]
            o = _dot(qd[:, sl], s.astype(BF16)) + _dot(att[i].astype(BF16), v[:, sl])
            whole = jnp.exp(jnp.broadcast_to(cum[c - 1:c, sl], (s.shape[1], s.shape[0])).T)
            s_scr[g0 + i] = whole * s + _dot(kdec[:, sl].T.astype(BF16), v[:, sl])
            o = o * lax.rsqrt(jnp.mean(o * o, axis=-1, keepdims=True) + EPS)
            o_ref[:, (g0 + i) * kd:(g0 + i + 1) * kd] = (o * gate[:, sl]).astype(BF16)

    @pl.when(ci == pl.num_programs(2) - 1)
    def _():
        so_ref[0] = s_scr[...]


def _hgrn_scan(st, proj, lb, gain, heads, kd, vd):
    width = heads * kd
    c = min(HGRN_CHUNK, st.seq)
    hb = min(HGRN_HEADS_PER_STEP, heads)
    n_chunks = st.seq // c
    groups = heads // hb
    mstack, level, nbits = _hgrn_tables(c)

    def seg(j):
        return pl.BlockSpec((c, hb * kd), lambda b, h, ci: (b * n_chunks + ci, j * groups + h))

    head_row = pl.BlockSpec((1, hb * kd), lambda b, h, ci: (0, h))
    return pl.pallas_call(
        functools.partial(_hgrn_kernel, c, nbits, hb, kd),
        grid=(st.batch, groups, n_chunks),
        in_specs=[seg(0), seg(1), seg(2), seg(3), head_row, head_row,
                  pl.BlockSpec((mstack.shape[0], 2 * c), lambda b, h, ci: (0, 0)),
                  pl.BlockSpec(level.shape, lambda b, h, ci: (0, 0))],
        out_specs=[pl.BlockSpec((c, hb * vd), lambda b, h, ci: (b * n_chunks + ci, h)),
                   pl.BlockSpec((1, hb, kd, vd), lambda b, h, ci: (b, h, 0, 0))],
        out_shape=[jax.ShapeDtypeStruct((st.n, heads * vd), BF16),
                   jax.ShapeDtypeStruct((st.batch, heads, kd, vd), F32)],
        scratch_shapes=[pltpu.VMEM((hb, kd, vd), F32)],
        compiler_params=_cparams("parallel", "parallel", "arbitrary"),
        name="hgrn_scan",
    )(proj, proj, proj, proj, lb.reshape(1, width), gain.reshape(1, width),
      jnp.asarray(np.concatenate([mstack, mstack], 1), BF16), jnp.asarray(level))


def _step_tables(bb, t):
    rows = bb * t
    r = np.arange(rows)
    grp, pos = r // t, r % t
    same = grp[:, None] == grp[None, :]
    i, j = r[:, None], r[None, :]
    sums = [same & (j <= i), same & (j > i), same]
    for d in range(1, t):
        sums.append(same & (j <= i) & (j > i - d) & (pos[:, None] >= d))
    shifts = [same & (j == i - d) for d in range(1, t)]
    return (np.concatenate(sums, 0).astype(np.float32), np.concatenate(shifts, 0).astype(np.float32))


def _hgrn_step_kernel(bb, t, heads, kd, q_ref, z_ref, i_ref, g_ref, lb_ref, gain_ref, ms_ref, sh_ref,
                      seg_ref, segt_ref, s_ref, *rest):
    o_ref, so_ref = rest[-2:]
    rows = bb * t
    lb = lb_ref[...]
    f = lb + (1.0 - lb) * jax.nn.sigmoid(z_ref[...])
    logf = jnp.log(f)
    k = 1.0 - f
    q = _silu(q_ref[...])
    v = i_ref[...]
    e = _dot2(ms_ref[...], logf)
    qd = (q * jnp.exp(e[0:rows])).astype(BF16)
    kdec = k * jnp.exp(e[rows:2 * rows])
    whole = jnp.exp(e[2 * rows:3 * rows])

    seg, segt = seg_ref[...], segt_ref[...]
    o = jnp.zeros_like(q)
    for d in range(t):
        if d == 0:
            ks, vs, w = k, v, q * k
        else:
            sh = sh_ref[(d - 1) * rows:d * rows]
            ks, vs = _dot2(sh, k), _dot2(sh, v)
            w = q * ks * jnp.exp(e[(2 + d) * rows:(3 + d) * rows])
        att = _dot2_lhs(w, seg)
        o = o + _dot2_lhs(att, segt) * vs

    gate = gain_ref[...] * _silu(g_ref[...])
    vb = v.astype(BF16)
    row_seq = lax.broadcasted_iota(jnp.int32, (rows, kd), 0) // t
    lane_seq = lax.broadcasted_iota(jnp.int32, (kd, rows), 1) // t
    for h in range(heads):
        sl = slice(h * kd, (h + 1) * kd)
        k_t = kdec[:, sl].T
        w_t = whole[:, sl].T
        o_h = o[:, sl]
        for b in range(bb):
            s = s_ref[0, b, h]
            o_h = o_h + jnp.where(row_seq == b, _dot(qd[:, sl], s.astype(BF16)), 0.0)
            k_b = jnp.where(lane_seq == b, k_t, 0.0).astype(BF16)
            so_ref[0, b, h] = w_t[:, b * t:b * t + 1] * s + _dot(k_b, vb[:, sl])
        o_h = o_h * lax.rsqrt(jnp.mean(o_h * o_h, axis=-1, keepdims=True) + EPS)
        o_ref[:, sl] = (o_h * gate[:, sl]).astype(BF16)


def _hgrn_step(st, proj, lb, gain, state_all, j, o_stream, so_prev):
    _, _, heads, kd, vd = state_all.shape
    width = heads * kd
    t = st.dec_seq
    bb = BF16_SUBLANES // math.gcd(BF16_SUBLANES, t)
    rows = bb * t
    assert st.dec_batch % bb == 0 and st.n_prompt % rows == 0 and heads <= LANES
    rb0 = st.n_prompt // rows
    sums, shifts = _step_tables(bb, t)
    seg = (np.arange(width)[:, None] // kd == np.arange(LANES)[None, :]).astype(np.float32)

    def col(jj):
        return pl.BlockSpec((rows, width), lambda i: (rb0 + i, jj))

    full = lambda a: pl.BlockSpec(a.shape, lambda i: (0,) * a.ndim)
    state = pl.BlockSpec((1, bb, heads, kd, vd), lambda i: (j, i, 0, 0, 0))
    anyspec = pl.BlockSpec(memory_space=pl.ANY)
    args = [proj, proj, proj, proj, lb.reshape(1, width), gain.reshape(1, width),
            jnp.asarray(sums, BF16), jnp.asarray(shifts, BF16), jnp.asarray(seg, BF16),
            jnp.asarray(seg.T, BF16), state_all, o_stream]
    in_specs = [col(0), col(1), col(2), col(3), pl.BlockSpec((1, width), lambda i: (0, 0)),
                pl.BlockSpec((1, width), lambda i: (0, 0)), full(sums), full(shifts), full(seg),
                full(seg.T), state, anyspec]
    aliases = {11: 0}
    if so_prev is not None:
        args.append(so_prev)
        in_specs.append(anyspec)
        aliases[12] = 1
    return pl.pallas_call(
        functools.partial(_hgrn_step_kernel, bb, t, heads, kd),
        grid=(st.dec_batch // bb,),
        in_specs=in_specs,
        out_specs=[pl.BlockSpec((rows, width), lambda i: (rb0 + i, 0)), state],
        out_shape=[jax.ShapeDtypeStruct(o_stream.shape, BF16),
                   jax.ShapeDtypeStruct(state_all.shape, F32)],
        input_output_aliases=aliases,
        compiler_params=_cparams("parallel"),
        name="hgrn_step",
    )(*args)


def _rotate(x, cos, sin):
    half = x.shape[-1] // 2
    x1, x2 = x[:, :half], x[:, half:]
    return jnp.concatenate([x1 * cos - x2 * sin, x1 * sin + x2 * cos], axis=-1)


def _group_norm_gate(o, gain, g):
    mu = jnp.mean(o, axis=-1, keepdims=True)
    var = jnp.mean(jnp.square(o - mu), axis=-1, keepdims=True)
    return (_silu(g) * ((o - mu) * lax.rsqrt(var + EPS) * gain)).astype(BF16)


def _ret_kernel(c, hb, dk, dv, lg_ref, q_ref, k_ref, v_ref, g_ref, cos_ref, sin_ref, gain_ref, o_ref, so_ref,
                s_scr, intra_scr):
    ci = pl.program_id(2)
    t_col = lax.broadcasted_iota(jnp.int32, (c, 1), 0).astype(F32)
    lgs = [lg_ref[pl.program_id(1) * hb + h] for h in range(hb)]

    @pl.when(ci == 0)
    def _():
        s_scr[...] = jnp.zeros_like(s_scr)
        rel = (lax.broadcasted_iota(jnp.int32, (c, c), 0)
               - lax.broadcasted_iota(jnp.int32, (c, c), 1)).astype(F32)
        for h, lg in enumerate(lgs):
            intra_scr[h] = jnp.where(rel >= 0.0, jnp.exp(jnp.maximum(rel, 0.0) * lg), 0.0)

    cos, sin = cos_ref[...], sin_ref[...]
    for h, lg in enumerate(lgs):
        q = _rotate(q_ref[:, h * dk:(h + 1) * dk], cos, sin)
        k = _rotate(k_ref[:, h * dk:(h + 1) * dk], cos, sin) * (dk ** -0.5)
        v = v_ref[:, h * dv:(h + 1) * dv].astype(BF16)
        inter = jnp.exp((t_col + 1.0) * lg)
        tail = jnp.exp((c - 1.0 - t_col) * lg)
        whole = jnp.exp(jnp.full((1, 1), c, F32) * lg)
        s = s_scr[h]
        att = _dot_nt(q.astype(BF16), k.astype(BF16)) * intra_scr[h]
        o = _dot(att.astype(BF16), v) + _dot((q * inter).astype(BF16), s.astype(BF16))
        s_scr[h] = whole * s + _dot((k * tail).T.astype(BF16), v)
        o_ref[:, h * dv:(h + 1) * dv] = _group_norm_gate(
            o, gain_ref[:, h * dv:(h + 1) * dv], g_ref[:, h * dv:(h + 1) * dv])

    @pl.when(ci == pl.num_programs(2) - 1)
    def _():
        so_ref[0] = s_scr[...]


def _ret_scan(st, proj, cos, sin, log_gamma, gain, heads, dk, dv):
    c = min(RET_CHUNK, st.seq)
    hb = min(RET_HEADS_PER_STEP, heads)
    groups = heads // hb
    n_chunks = st.seq // c
    v_off = 2 * heads * dk // (hb * dv)
    g_off = v_off + groups

    def qk(j):
        return pl.BlockSpec((c, hb * dk), lambda b, h, ci: (b * n_chunks + ci, j * groups + h))

    def vg(off):
        return pl.BlockSpec((c, hb * dv), lambda b, h, ci: (b * n_chunks + ci, off + h))

    rope = pl.BlockSpec((c, dk // 2), lambda b, h, ci: (ci, 0))
    return pl.pallas_call(
        functools.partial(_ret_kernel, c, hb, dk, dv),
        grid=(st.batch, groups, n_chunks),
        in_specs=[pl.BlockSpec(memory_space=pltpu.SMEM),
                  qk(0), qk(1), vg(v_off), vg(g_off), rope, rope,
                  pl.BlockSpec((1, hb * dv), lambda b, h, ci: (0, h))],
        out_specs=[pl.BlockSpec((c, hb * dv), lambda b, h, ci: (b * n_chunks + ci, h)),
                   pl.BlockSpec((1, hb, dk, dv), lambda b, h, ci: (b, h, 0, 0))],
        out_shape=[jax.ShapeDtypeStruct((st.n, heads * dv), BF16),
                   jax.ShapeDtypeStruct((st.batch, heads, dk, dv), F32)],
        scratch_shapes=[pltpu.VMEM((hb, dk, dv), F32), pltpu.VMEM((hb, c, c), F32)],
        compiler_params=_cparams("parallel", "parallel", "arbitrary"),
        name="ret_scan",
    )(log_gamma, proj, proj, proj, proj, cos, sin, gain.reshape(1, heads * dv))


def _ret_step_kernel(bb, t, hb, dk, dv, lg_ref, q_ref, k_ref, v_ref, g_ref, cos_ref, sin_ref, pos_ref, rel_ref,
                     gain_ref, s_ref, *rest):
    o_ref, so_ref = rest[-2:]
    rows = bb * t
    cos, sin = cos_ref[...], sin_ref[...]
    t_col = pos_ref[...][:, :1]
    rel = rel_ref[...]
    row_seq = lax.broadcasted_iota(jnp.int32, (rows, dv), 0) // t
    lane_seq = lax.broadcasted_iota(jnp.int32, (dk, rows), 1) // t
    for h in range(hb):
        lg = lg_ref[pl.program_id(1) * hb + h]
        q = _rotate(q_ref[:, h * dk:(h + 1) * dk], cos, sin)
        k = _rotate(k_ref[:, h * dk:(h + 1) * dk], cos, sin) * (dk ** -0.5)
        v = v_ref[:, h * dv:(h + 1) * dv].astype(BF16)
        inter = jnp.exp((t_col + 1.0) * lg)
        tail = jnp.exp((t - 1.0 - t_col) * lg)
        intra = jnp.where(rel >= 0.0, jnp.exp(jnp.maximum(rel, 0.0) * lg), 0.0)
        whole = jnp.exp(jnp.full((1, 1), t, F32) * lg)

        att = _dot_nt(q.astype(BF16), k.astype(BF16)) * intra
        o = _dot(att.astype(BF16), v)
        qd = (q * inter).astype(BF16)
        k_t = (k * tail).T
        for b in range(bb):
            s = s_ref[0, b, h]
            o = o + jnp.where(row_seq == b, _dot(qd, s.astype(BF16)), 0.0)
            k_b = jnp.where(lane_seq == b, k_t, 0.0).astype(BF16)
            so_ref[0, b, h] = whole * s + _dot(k_b, v)
        o_ref[:, h * dv:(h + 1) * dv] = _group_norm_gate(
            o, gain_ref[:, h * dv:(h + 1) * dv], g_ref[:, h * dv:(h + 1) * dv])


def _ret_step(st, proj, cos, sin, log_gamma, gain, state_all, j, o_stream, so_prev):
    _, _, heads, dk, dv = state_all.shape
    t = st.dec_seq
    bb = BF16_SUBLANES // math.gcd(BF16_SUBLANES, t)
    rows = bb * t
    assert st.dec_batch % bb == 0 and st.n_prompt % rows == 0
    hb = min(RET_HEADS_PER_STEP, heads)
    groups = heads // hb
    rb0 = st.n_prompt // rows
    v_off = 2 * heads * dk // (hb * dv)
    g_off = v_off + groups
    r = np.arange(rows)
    same = (r[:, None] // t) == (r[None, :] // t)
    rel = np.where(same & (r[None, :] <= r[:, None]), r[:, None] - r[None, :], -1).astype(np.float32)
    pos = np.broadcast_to((r % t).astype(np.float32)[:, None], (rows, LANES))

    full = lambda a: pl.BlockSpec(a.shape, lambda i, h: (0,) * a.ndim)
    state = pl.BlockSpec((1, bb, hb, dk, dv), lambda i, h: (j, i, h, 0, 0))
    anyspec = pl.BlockSpec(memory_space=pl.ANY)
    cos_r, sin_r = jnp.tile(cos, (bb, 1)), jnp.tile(sin, (bb, 1))
    args = [log_gamma, proj, proj, proj, proj, cos_r, sin_r, jnp.asarray(pos), jnp.asarray(rel),
            gain.reshape(1, heads * dv), state_all, o_stream]
    in_specs = [pl.BlockSpec(memory_space=pltpu.SMEM),
                pl.BlockSpec((rows, hb * dk), lambda i, h: (rb0 + i, h)),
                pl.BlockSpec((rows, hb * dk), lambda i, h: (rb0 + i, groups + h)),
                pl.BlockSpec((rows, hb * dv), lambda i, h: (rb0 + i, v_off + h)),
                pl.BlockSpec((rows, hb * dv), lambda i, h: (rb0 + i, g_off + h)),
                full(cos_r), full(sin_r), full(pos), full(rel),
                pl.BlockSpec((1, hb * dv), lambda i, h: (0, h)), state, anyspec]
    aliases = {11: 0}
    if so_prev is not None:
        args.append(so_prev)
        in_specs.append(anyspec)
        aliases[12] = 1
    return pl.pallas_call(
        functools.partial(_ret_step_kernel, bb, t, hb, dk, dv),
        grid=(st.dec_batch // bb, groups),
        in_specs=in_specs,
        out_specs=[pl.BlockSpec((rows, hb * dv), lambda i, h: (rb0 + i, h)), state],
        out_shape=[jax.ShapeDtypeStruct(o_stream.shape, BF16),
                   jax.ShapeDtypeStruct(state_all.shape, F32)],
        input_output_aliases=aliases,
        compiler_params=_cparams("parallel", "parallel"),
        name="ret_step",
    )(*args)


def _rope_tables(pos, half):
    inv = 1.0 / (ROPE_BASE ** jnp.linspace(0.0, 1.0, half, dtype=F32))
    ang = pos[:, None] * inv[None]
    return jnp.cos(ang), jnp.sin(ang)


def _expert_kernel(ts_ref, na_ref, tk_ref, eseq_ref, nk_ref, x_ref, w_ref, wg_hbm, wu_hbm, wd_hbm, o_ref,
                   wgb, wub, wdb, stage_g, stage_u, stage_d, sems):
    t = pl.program_id(0)
    active = t < na_ref[0]
    k = tk_ref[t]
    new_expert = (t == 0) | (k != tk_ref[jnp.maximum(t - 1, 0)])

    def weight_copies(kk, slot):
        e = eseq_ref[kk]
        return (pltpu.make_async_copy(wg_hbm.at[e], stage_g.at[slot], sems.at[0, slot]),
                pltpu.make_async_copy(wu_hbm.at[e], stage_u.at[slot], sems.at[1, slot]),
                pltpu.make_async_copy(wd_hbm.at[e], stage_d.at[slot], sems.at[2, slot]))

    @pl.when(t == 0)
    def _():
        for cp in weight_copies(0, 0):
            cp.start()

    @pl.when(active & new_expert)
    def _():
        slot = lax.rem(k, 2)

        @pl.when(k + 1 < nk_ref[0])
        def _():
            for cp in weight_copies(k + 1, 1 - slot):
                cp.start()

        for cp in weight_copies(k, slot):
            cp.wait()
        wgb[...] = stage_g[slot].astype(BF16)
        wub[...] = stage_u[slot].astype(BF16)
        wdb[...] = stage_d[slot].astype(BF16)

    @pl.when(active)
    def _():
        x = x_ref[...].astype(BF16)
        hg = _silu(_dot(x, wgb[...])) * _dot(x, wub[...])
        o_ref[...] = _dot((hg * w_ref[...]).astype(BF16), wdb[...])

    @pl.when(jnp.logical_not(active))
    def _():
        o_ref[...] = jnp.zeros_like(o_ref)


def _experts(x_sorted, w_sorted, tile_src, n_active, tile_pos, expert_seq, n_used, w_gate, w_up, w_down):
    rows = x_sorted.shape[0]
    _, d, ff = w_gate.shape
    tm = EXPERT_TILE
    hbm = pl.BlockSpec(memory_space=pl.ANY)
    grid_spec = pltpu.PrefetchScalarGridSpec(
        num_scalar_prefetch=5,
        grid=(rows // tm,),
        in_specs=[pl.BlockSpec((tm, d), lambda t, ts, *_: (ts[t], 0)),
                  pl.BlockSpec((tm, 1), lambda t, ts, *_: (ts[t], 0)),
                  hbm, hbm, hbm],
        out_specs=pl.BlockSpec((tm, d), lambda t, *_: (t, 0)),
        scratch_shapes=[pltpu.VMEM((d, ff), BF16), pltpu.VMEM((d, ff), BF16), pltpu.VMEM((ff, d), BF16),
                        pltpu.VMEM((2, d, ff), F32), pltpu.VMEM((2, d, ff), F32), pltpu.VMEM((2, ff, d), F32),
                        pltpu.SemaphoreType.DMA((3, 2))],
    )
    return pl.pallas_call(
        _expert_kernel,
        grid_spec=grid_spec,
        out_shape=jax.ShapeDtypeStruct((rows, d), F32),
        compiler_params=_cparams("arbitrary"),
        name="experts",
    )(tile_src, n_active, tile_pos, expert_seq, n_used, x_sorted, w_sorted, w_gate, w_up, w_down)


def _take_rows(a, idx):
    return a.at[idx].get(mode="promise_in_bounds")


def _step_lookup(table, starts, x):
    jumps = table[1:] - table[:-1]
    return table[0] + jnp.sum(jnp.where(x[None, :] >= starts[1:, None], jumps[:, None], 0), axis=0)


def _dispatch_plan(eid, ew, n_experts_total):
    n, top_k = eid.shape
    tm = EXPERT_TILE
    a = n * top_k
    n_tiles = a // tm + n_experts_total
    rows = n_tiles * tm
    i32 = jnp.int32
    q = jnp.arange(a, dtype=i32)
    sorted_e, order, sorted_w = lax.sort((eid.reshape(a), q, ew.reshape(a)), num_keys=1, is_stable=True)
    bounds = jnp.searchsorted(sorted_e, jnp.arange(n_experts_total + 1, dtype=i32), side="left").astype(i32)
    off, end = bounds[:-1], bounds[1:]
    padded = (end - off + tm - 1) // tm * tm
    pad_end = jnp.cumsum(padded)
    pad_off = pad_end - padded
    delta = pad_off - off
    _, pos = lax.sort((order, q + _step_lookup(delta, off, q)), num_keys=1)
    n_active = (pad_end[-1] // tm).astype(i32)
    tile_src = jnp.minimum(jnp.arange(n_tiles, dtype=i32), n_active - 1)
    experts = jnp.arange(n_experts_total, dtype=i32)
    used = padded > 0
    n_used = jnp.sum(used, dtype=i32)
    expert_seq = jnp.minimum(jnp.sort(jnp.where(used, experts, n_experts_total)), n_experts_total - 1)
    tile_pos = _step_lookup(jnp.cumsum(used, dtype=i32) - 1, pad_off, tile_src * tm)
    p = jnp.arange(rows, dtype=i32)
    src = p - _step_lookup(delta, pad_off, p)
    valid = (src < _step_lookup(end, pad_off, p)) & (p < pad_end[-1])
    src = jnp.clip(src, 0, a - 1)
    row_token = jnp.where(valid, _take_rows(order, src) // top_k, p % n)
    row_w = jnp.where(valid, _take_rows(sorted_w, src), 0.0)
    return (row_token, row_w.reshape(rows, 1), pos.reshape(n, top_k), tile_src, n_active.reshape(1),
            tile_pos, expert_seq, n_used.reshape(1))


def _combine_kernel(n_prompt_tiles, x_ref, ya_ref, yb_ref, gr, gt, shr, sht, scr, sct, o_ref, hn_ref):
    is_s = pl.program_id(0) >= n_prompt_tiles
    x = x_ref[...] + _pick(is_s, gr, gt) * (ya_ref[...] + yb_ref[...])
    o_ref[...] = x
    hn_ref[...] = (_rms(x) * (1.0 + _pick(is_s, scr, sct)) + _pick(is_s, shr, sht)).astype(BF16)


def _combine_final_kernel(n_prompt_tiles, x_ref, ya_ref, yb_ref, gr, gt, fg_ref, op_ref, os_ref):
    is_s = pl.program_id(0) >= n_prompt_tiles
    x = x_ref[...] + _pick(is_s, gr, gt) * (ya_ref[...] + yb_ref[...])
    y = _rms(x) * fg_ref[...]

    @pl.when(jnp.logical_not(is_s))
    def _():
        op_ref[...] = y

    @pl.when(is_s)
    def _():
        os_ref[...] = y


def _combine(st, x, y_pairs, mod_row, mod_tok, layer, j_gate, final_gain, final):
    d = st.d
    tile = pl.BlockSpec((st.tm, d), lambda i: (i, 0))
    n_tiles = st.n_tiles
    common = [tile, tile, pl.BlockSpec((st.tm, d), lambda i: (n_tiles + i, 0)),
              *st.mod_specs(layer, d, lambda i: j_gate)]
    if final:
        return pl.pallas_call(
            functools.partial(_combine_final_kernel, st.n_prompt_tiles),
            grid=(st.n_tiles,),
            in_specs=[*common, pl.BlockSpec((1, d), lambda i: (0, 0))],
            out_specs=[pl.BlockSpec((st.tm, d), lambda i: (jnp.minimum(i, n_tiles - 2), 0)),
                       pl.BlockSpec((st.tm, d), lambda i: (0, 0))],
            out_shape=[jax.ShapeDtypeStruct((st.n_prompt, d), F32), jax.ShapeDtypeStruct((st.n_sample, d), F32)],
            compiler_params=_cparams("arbitrary"),
            name="moe_combine_final",
        )(x, y_pairs, y_pairs, mod_row, mod_tok, final_gain.reshape(1, d)), None
    return pl.pallas_call(
        functools.partial(_combine_kernel, st.n_prompt_tiles),
        grid=(st.n_tiles,),
        in_specs=[*common, *st.mod_specs(layer + 1, d, lambda i: 0), *st.mod_specs(layer + 1, d, lambda i: 1)],
        out_specs=[tile, tile],
        out_shape=[jax.ShapeDtypeStruct((st.n, d), F32), jax.ShapeDtypeStruct((st.n, d), BF16)],
        compiler_params=_cparams("parallel"),
        name="moe_combine",
    )(x, y_pairs, y_pairs, mod_row, mod_tok, mod_row, mod_tok, mod_row, mod_tok)


def kernel(x_prompt, x_sample, state_hgrn, state_ret, c_prompt, c_sample, ada_w, ada_b, hgrn_w_in, hgrn_lb_logits, hgrn_norm_gain, hgrn_w_out, ret_w_in, ret_norm_gain, ret_w_out, moe_router_group, moe_router_expert, moe_w_gate, moe_w_up, moe_w_down, final_norm_gain):
    batch, seq, d = x_prompt.shape
    dec_batch, dec_seq, _ = x_sample.shape
    depth = ada_w.shape[0]
    _, _, a_heads, a_kd, a_vd = state_hgrn.shape
    _, _, b_heads, b_dk, b_dv = state_ret.shape
    _, n_groups, _, n_exp = moe_router_expert.shape
    ff = moe_w_gate.shape[-1]
    n_exp_total = n_groups * n_exp
    st = _Stream(batch, seq, dec_batch, dec_seq, d)

    p = jax.nn.softmax(hgrn_lb_logits.astype(F32), axis=0)
    lb_all = jnp.cumsum(p, axis=0) - p[0:1]
    log_gamma = jnp.log1p(-jnp.exp2(-5.0 - jnp.arange(b_heads, dtype=F32)))
    cos_p, sin_p = _rope_tables(jnp.arange(seq, dtype=F32), b_dk // 2)
    cos_s, sin_s = _rope_tables(jnp.arange(dec_seq, dtype=F32) + PAST_LEN, b_dk // 2)
    w_gate = moe_w_gate.reshape(depth * n_exp_total, d, ff)
    w_up = moe_w_up.reshape(depth * n_exp_total, d, ff)
    w_down = moe_w_down.reshape(depth * n_exp_total, ff, d)

    c_rows = -(-batch // BF16_SUBLANES) * BF16_SUBLANES
    mod_row, mod_tok = _ada_mod(jnp.pad(c_prompt, ((0, c_rows - batch), (0, 0))),
                                jnp.repeat(c_sample, dec_seq, axis=0), ada_w, ada_b)
    mod_row = mod_row[:, :batch].reshape(depth * batch, 1, 6 * d)

    x = jnp.concatenate([x_prompt.reshape(-1, d), x_sample.reshape(-1, d)], 0)
    hg_p, rt_p, hg_s, rt_s = [], [], None, None
    hn = _norm_mod(st, x, mod_row, mod_tok, 0, 0, 1)
    for l in range(depth):
        j = l // 2
        if l % 2 == 0:
            proj = _proj(st, hn, hgrn_w_in, j)
            o, s_p = _hgrn_scan(st, proj, lb_all[j], hgrn_norm_gain[j], a_heads, a_kd, a_vd)
            o, hg_s = _hgrn_step(st, proj, lb_all[j], hgrn_norm_gain[j], state_hgrn, j, o, hg_s)
            hg_p.append(s_p)
            w_out = hgrn_w_out
        else:
            proj = _proj(st, hn, ret_w_in, j)
            o, s_p = _ret_scan(st, proj, cos_p, sin_p, log_gamma, ret_norm_gain[j], b_heads, b_dk, b_dv)
            o, rt_s = _ret_step(st, proj, cos_s, sin_s, log_gamma, ret_norm_gain[j], state_ret, j, o, rt_s)
            rt_p.append(s_p)
            w_out = ret_w_out
        x = _proj_residual(st, o, w_out, j, x, mod_row, mod_tok, l, 2)

        w_router = jnp.concatenate(
            [moe_router_group[l], jnp.moveaxis(moe_router_expert[l], 0, 1).reshape(d, n_exp_total)], 1)
        w_router = jnp.pad(w_router, ((0, 0), (0, LANES - w_router.shape[1])))
        hx, eid, ew = _norm_mod_router(st, x, mod_row, mod_tok, l, 3, 4, w_router, n_groups, n_exp)
        row_token, row_w, pos, tile_src, n_active, tile_pos, expert_seq, n_used = _dispatch_plan(
            eid, ew, n_exp_total)
        y_sorted = _experts(_take_rows(hx, row_token), row_w, tile_src, n_active, tile_pos,
                            expert_seq + l * n_exp_total, n_used, w_gate, w_up, w_down)
        y_pairs = _take_rows(y_sorted, pos.T.reshape(-1))
        x, hn = _combine(st, x, y_pairs, mod_row, mod_tok, l, 5, final_norm_gain, l == depth - 1)

    y_prompt = x[0].reshape(batch, seq, d)
    y_sample = x[1].reshape(dec_batch, dec_seq, d)
    return (y_prompt, y_sample, jnp.stack(hg_p), jnp.stack(rt_p), hg_s, rt_s)
```

```python
import functools
import math

import numpy as np
import jax
import jax.numpy as jnp
from jax import lax
from jax.experimental import pallas as pl
from jax.experimental.pallas import tpu as pltpu

F32 = jnp.float32
BF16 = jnp.bfloat16
EPS = 1e-6
PAST_LEN = 16384
ROPE_BASE = 10000.0

V7X_VMEM_LIMIT_BYTES = 56 * 1024 * 1024
LANES = 128
BF16_SUBLANES = 16

HGRN_CHUNK = 128
HGRN_HEADS_PER_STEP = 8
HGRN_HEADS_PER_GROUP = 8
RET_CHUNK = 256
RET_HEADS_PER_STEP = 2
EXPERT_TILE = 256
PROJ_TN = 1024
PROJ_IN_ROW_TILES = 8
ADA_TN = 1024


def _cparams(*sem):
    return pltpu.CompilerParams(dimension_semantics=sem, vmem_limit_bytes=V7X_VMEM_LIMIT_BYTES)


def _dot(a, b):
    return jnp.dot(a, b, preferred_element_type=F32)


def _dot_nt(a, b):
    return lax.dot_general(a, b, (((1,), (1,)), ((), ())), preferred_element_type=F32)


def _dot2(m, x):
    hi = x.astype(BF16)
    lo = (x - hi.astype(F32)).astype(BF16)
    return _dot(m, hi) + _dot(m, lo)


def _dot2_wide(mm, x):
    hi = x.astype(BF16)
    lo = (x - hi.astype(F32)).astype(BF16)
    return _dot(mm, jnp.concatenate([hi, lo], axis=0))


def _dot2_lhs(x, m):
    hi = x.astype(BF16)
    lo = (x - hi.astype(F32)).astype(BF16)
    return _dot(hi, m) + _dot(lo, m)


def _silu(x):
    return x * jax.nn.sigmoid(x)


def _rms(x):
    return x * lax.rsqrt(jnp.mean(x * x, axis=-1, keepdims=True) + EPS)


def _ada_kernel(cr_ref, ct_ref, w_ref, b_ref, or_ref, ot_ref):
    w = w_ref[0].astype(BF16)
    or_ref[0] = _dot(_silu(cr_ref[...]).astype(BF16), w) + b_ref[0]
    ot_ref[0] = _dot(_silu(ct_ref[...]).astype(BF16), w) + b_ref[0]


def _ada_mod(c_row, c_tok, ada_w, ada_b):
    depth, d, n6 = ada_w.shape
    r, t = c_row.shape[0], c_tok.shape[0]
    tn = min(ADA_TN, n6)
    return pl.pallas_call(
        _ada_kernel,
        grid=(depth, n6 // tn),
        in_specs=[pl.BlockSpec((r, d), lambda l, n: (0, 0)),
                  pl.BlockSpec((t, d), lambda l, n: (0, 0)),
                  pl.BlockSpec((1, d, tn), lambda l, n: (l, 0, n)),
                  pl.BlockSpec((1, 1, tn), lambda l, n: (l, 0, n))],
        out_specs=[pl.BlockSpec((1, r, tn), lambda l, n: (l, 0, n)),
                   pl.BlockSpec((1, t, tn), lambda l, n: (l, 0, n))],
        out_shape=[jax.ShapeDtypeStruct((depth, r, n6), F32), jax.ShapeDtypeStruct((depth, t, n6), F32)],
        compiler_params=_cparams("parallel", "parallel"),
        name="ada_mod",
    )(c_row, c_tok, ada_w, ada_b.reshape(depth, 1, n6))


class _Stream:
    def __init__(self, batch, seq, dec_batch, dec_seq, d):
        self.batch, self.seq, self.dec_batch, self.dec_seq, self.d = batch, seq, dec_batch, dec_seq, d
        self.n_prompt = batch * seq
        self.n_sample = dec_batch * dec_seq
        self.tm = self.n_sample
        assert seq % self.tm == 0 and self.tm % BF16_SUBLANES == 0
        self.tiles_per_seq = seq // self.tm
        self.n_prompt_tiles = self.n_prompt // self.tm
        self.n_tiles = self.n_prompt_tiles + 1
        self.n = self.n_prompt + self.n_sample

    def mod_specs(self, layer, width, col_block, tile_axis=0):
        tps, last, batch = self.tiles_per_seq, self.batch - 1, self.batch

        def row_map(*idx):
            return (layer * batch + jnp.minimum(idx[tile_axis] // tps, last), 0, col_block(*idx))
        return (pl.BlockSpec((1, 1, width), row_map),
                pl.BlockSpec((1, self.tm, width), lambda *idx: (layer, 0, col_block(*idx))))


def _pick(is_sample, row_ref, tok_ref):
    return jnp.where(is_sample, tok_ref[0], row_ref[0])


def _norm_mod_kernel(n_prompt_tiles, x_ref, shr, sht, scr, sct, o_ref):
    is_s = pl.program_id(0) >= n_prompt_tiles
    h = _rms(x_ref[...]) * (1.0 + _pick(is_s, scr, sct)) + _pick(is_s, shr, sht)
    o_ref[...] = h.astype(BF16)


def _norm_mod(st, x, mod_row, mod_tok, layer, j_shift, j_scale):
    d = st.d
    return pl.pallas_call(
        functools.partial(_norm_mod_kernel, st.n_prompt_tiles),
        grid=(st.n_tiles,),
        in_specs=[pl.BlockSpec((st.tm, d), lambda i: (i, 0)),
                  *st.mod_specs(layer, d, lambda i: j_shift), *st.mod_specs(layer, d, lambda i: j_scale)],
        out_specs=pl.BlockSpec((st.tm, d), lambda i: (i, 0)),
        out_shape=jax.ShapeDtypeStruct((st.n, d), BF16),
        compiler_params=_cparams("parallel"),
        name="norm_mod",
    )(x, mod_row, mod_tok, mod_row, mod_tok)


def _norm_mod_router_kernel(n_prompt_tiles, n_groups, n_experts, x_ref, shr, sht, scr, sct, wr_ref,
                            o_ref, eid_ref, ew_ref):
    is_s = pl.program_id(0) >= n_prompt_tiles
    h = _rms(x_ref[...]) * (1.0 + _pick(is_s, scr, sct)) + _pick(is_s, shr, sht)
    o_ref[...] = h

    logits = _dot(h.astype(BF16), wr_ref[...].astype(BF16))
    tm = logits.shape[0]
    lane = lax.broadcasted_iota(jnp.int32, (tm, LANES), 1)
    neg_inf = jnp.float32(-jnp.inf)

    def first_index_of_max(vals, vmax):
        return jnp.min(jnp.where(vals == vmax, lane, LANES), axis=1, keepdims=True)

    is_g = lane < n_groups
    lg = jnp.where(is_g, logits, neg_inf)
    mg = jnp.max(lg, axis=1, keepdims=True)
    eg = jnp.exp(lg - mg)
    pg = eg / jnp.sum(eg, axis=1, keepdims=True)
    gi = first_index_of_max(lg, mg)
    wg = jnp.sum(jnp.where(lane == gi, pg, 0.0), axis=1, keepdims=True)

    lo = n_groups + gi * n_experts
    sel = (lane >= lo) & (lane < lo + n_experts)
    le = jnp.where(sel, logits, neg_inf)
    me = jnp.max(le, axis=1, keepdims=True)
    ee = jnp.exp(le - me)
    pe = jnp.where(sel, ee / jnp.sum(ee, axis=1, keepdims=True), -1.0)
    p1 = jnp.max(pe, axis=1, keepdims=True)
    i1 = first_index_of_max(pe, p1)
    pe2 = jnp.where(lane == i1, -1.0, pe)
    p2 = jnp.max(pe2, axis=1, keepdims=True)
    i2 = first_index_of_max(pe2, p2)
    denom = p1 + p2
    col = lax.broadcasted_iota(jnp.int32, (tm, 2), 1)
    eid_ref[...] = jnp.where(col == 0, i1, i2) - n_groups
    ew_ref[...] = wg * jnp.where(col == 0, p1 / denom, p2 / denom)


def _norm_mod_router(st, x, mod_row, mod_tok, layer, j_shift, j_scale, w_router, n_groups, n_experts):
    d = st.d
    return pl.pallas_call(
        functools.partial(_norm_mod_router_kernel, st.n_prompt_tiles, n_groups, n_experts),
        grid=(st.n_tiles,),
        in_specs=[pl.BlockSpec((st.tm, d), lambda i: (i, 0)),
                  *st.mod_specs(layer, d, lambda i: j_shift), *st.mod_specs(layer, d, lambda i: j_scale),
                  pl.BlockSpec((d, LANES), lambda i: (0, 0))],
        out_specs=[pl.BlockSpec((st.tm, d), lambda i: (i, 0)),
                   pl.BlockSpec((st.tm, 2), lambda i: (i, 0)),
                   pl.BlockSpec((st.tm, 2), lambda i: (i, 0))],
        out_shape=[jax.ShapeDtypeStruct((st.n, d), F32),
                   jax.ShapeDtypeStruct((st.n, 2), jnp.int32),
                   jax.ShapeDtypeStruct((st.n, 2), F32)],
        compiler_params=_cparams("parallel"),
        name="norm_mod_router",
    )(x, mod_row, mod_tok, mod_row, mod_tok, w_router)


def _proj_kernel(a_ref, w_ref, o_ref, wb_ref):
    @pl.when(pl.program_id(1) == 0)
    def _():
        wb_ref[...] = w_ref[0].astype(BF16)
    o_ref[...] = _dot(a_ref[...], wb_ref[...])


def _proj(st, a, w_all, j):
    _, k, n_out = w_all.shape
    tn = min(PROJ_TN, n_out)
    tm = st.n // PROJ_IN_ROW_TILES if st.n % (PROJ_IN_ROW_TILES * BF16_SUBLANES) == 0 else st.tm
    return pl.pallas_call(
        _proj_kernel,
        grid=(n_out // tn, st.n // tm),
        in_specs=[pl.BlockSpec((tm, k), lambda n, m: (m, 0)),
                  pl.BlockSpec((1, k, tn), lambda n, m: (j, 0, n))],
        out_specs=pl.BlockSpec((tm, tn), lambda n, m: (m, n)),
        out_shape=jax.ShapeDtypeStruct((st.n, n_out), F32),
        scratch_shapes=[pltpu.VMEM((k, tn), BF16)],
        compiler_params=_cparams("parallel", "arbitrary"),
        name="proj_in",
    )(a, w_all)


def _proj_res_kernel(n_prompt_tiles, ap_ref, as_ref, w_ref, x_ref, gr, gt, o_ref, wb_ref):
    @pl.when(pl.program_id(1) == 0)
    def _():
        wb_ref[...] = w_ref[0].astype(BF16)
    is_s = pl.program_id(1) >= n_prompt_tiles
    a = jnp.where(is_s, as_ref[...], ap_ref[...])
    o_ref[...] = x_ref[...] + _pick(is_s, gr, gt) * _dot(a, wb_ref[...])


def _proj_residual(st, a_prompt, a_sample, w_all, j, x, mod_row, mod_tok, layer, j_gate):
    _, k, d = w_all.shape
    tn = min(PROJ_TN // 2, d)
    nb = d // tn
    last_prompt = st.n_prompt_tiles - 1
    return pl.pallas_call(
        functools.partial(_proj_res_kernel, st.n_prompt_tiles),
        grid=(nb, st.n_tiles),
        in_specs=[pl.BlockSpec((st.tm, k), lambda n, m: (jnp.minimum(m, last_prompt), 0)),
                  pl.BlockSpec((st.tm, k), lambda n, m: (0, 0)),
                  pl.BlockSpec((1, k, tn), lambda n, m: (j, 0, n)),
                  pl.BlockSpec((st.tm, tn), lambda n, m: (m, n)),
                  *st.mod_specs(layer, tn, lambda n, m: j_gate * nb + n, tile_axis=1)],
        out_specs=pl.BlockSpec((st.tm, tn), lambda n, m: (m, n)),
        out_shape=jax.ShapeDtypeStruct((st.n, d), F32),
        scratch_shapes=[pltpu.VMEM((k, tn), BF16)],
        compiler_params=_cparams("parallel", "arbitrary"),
        name="proj_out",
    )(a_prompt, a_sample, w_all, x, mod_row, mod_tok)


def _hgrn_tables(c):
    nbits = int(math.log2(c))
    assert 1 << nbits == c
    r = np.arange(c)[:, None]
    j = np.arange(c)[None, :]
    mats = [(j <= r), (j > r)]
    for b in range(nbits):
        mid = ((r >> (b + 1)) << (b + 1)) + (1 << b)
        right = r >= mid
        mats.append(np.where(right, (j >= mid) & (j <= r), (j > r) & (j < mid)))
    level = np.full((c, c), -1, np.int32)
    level[r[:, 0], r[:, 0]] = 0
    diff = r ^ j
    for b in range(nbits):
        level[(j < r) & ((diff >> b) == 1)] = b + 1
    return np.concatenate(mats, 0).astype(np.float32), level, nbits


def _hgrn_kernel(c, nbits, hb, kd, q_ref, z_ref, i_ref, g_ref, lb_ref, gain_ref, ms_ref, lv_ref,
                 o_ref, so_ref, s_scr):
    ci = pl.program_id(2)

    @pl.when(ci == 0)
    def _():
        s_scr[...] = jnp.zeros_like(s_scr)

    lv = lv_ref[...]
    hg = min(HGRN_HEADS_PER_GROUP, hb)
    for g0 in range(0, hb, hg):
        gs = slice(g0 * kd, (g0 + hg) * kd)
        lb = lb_ref[:, gs]
        f = lb + (1.0 - lb) * jax.nn.sigmoid(z_ref[:, gs])
        logf = jnp.log(f)
        k = 1.0 - f
        q = _silu(q_ref[:, gs])
        v = i_ref[:, gs].astype(BF16)
        gate = gain_ref[:, gs] * _silu(g_ref[:, gs])
        hi = logf.astype(BF16)
        hl = jnp.concatenate([hi, (logf - hi.astype(F32)).astype(BF16)], axis=0)

        def decay_sum(r):
            return _dot(ms_ref[r * c:(r + 1) * c, :], hl)

        cum = decay_sum(0)
        qd = (q * jnp.exp(cum)).astype(BF16)
        kdec = k * jnp.exp(decay_sum(1))
        qb, kb = q.astype(BF16), k.astype(BF16)
        heads = [slice(i * kd, (i + 1) * kd) for i in range(hg)]
        att = [jnp.where(lv == 0, _dot_nt(qb[:, sl], kb[:, sl]), 0.0) for sl in heads]
        for b in range(nbits):
            gl = jnp.exp(decay_sum(2 + b)).astype(BF16)
            ql, kl = qb * gl, kb * gl
            att = [jnp.where(lv == b + 1, _dot_nt(ql[:, sl], kl[:, sl]), a) for sl, a in zip(heads, att)]
        for i, sl in enumerate(heads):
            s = s_scr[g0 + i]
            o = _dot(qd[:, sl], s.astype(BF16)) + _dot(att[i].astype(BF16), v[:, sl])
            whole = jnp.exp(jnp.broadcast_to(cum[c - 1:c, sl], (s.shape[1], s.shape[0])).T)
            s_scr[g0 + i] = whole * s + _dot(kdec[:, sl].T.astype(BF16), v[:, sl])
            o = o * lax.rsqrt(jnp.mean(o * o, axis=-1, keepdims=True) + EPS)
            o_ref[:, (g0 + i) * kd:(g0 + i + 1) * kd] = (o * gate[:, sl]).astype(BF16)

    @pl.when(ci == pl.num_programs(2) - 1)
    def _():
        so_ref[0] = s_scr[...]


def _step_tables(bb, t):
    rows = bb * t
    r = np.arange(rows)
    grp, pos = r // t, r % t
    same = grp[:, None] == grp[None, :]
    i, j = r[:, None], r[None, :]
    sums = [same & (j <= i), same & (j > i), same]
    for d in range(1, t):
        sums.append(same & (j <= i) & (j > i - d) & (pos[:, None] >= d))
    shifts = [same & (j == i - d) for d in range(1, t)]
    return (np.concatenate(sums, 0).astype(np.float32), np.concatenate(shifts, 0).astype(np.float32))


def _hgrn_step_kernel(bb, t, heads, kd, q_ref, z_ref, i_ref, g_ref, lb_ref, gain_ref, ms_ref, sh_ref,
                      seg_ref, segt_ref, s_ref, o_ref, so_ref):
    rows = bb * t
    lb = lb_ref[...]
    f = lb + (1.0 - lb) * jax.nn.sigmoid(z_ref[...])
    logf = jnp.log(f)
    k = 1.0 - f
    q = _silu(q_ref[...])
    v = i_ref[...]
    e = _dot2(ms_ref[...], logf)
    qd = (q * jnp.exp(e[0:rows])).astype(BF16)
    kdec = k * jnp.exp(e[rows:2 * rows])
    whole = jnp.exp(e[2 * rows:3 * rows])

    seg, segt = seg_ref[...], segt_ref[...]
    o = jnp.zeros_like(q)
    for d in range(t):
        if d == 0:
            ks, vs, w = k, v, q * k
        else:
            sh = sh_ref[(d - 1) * rows:d * rows]
            ks, vs = _dot2(sh, k), _dot2(sh, v)
            w = q * ks * jnp.exp(e[(2 + d) * rows:(3 + d) * rows])
        att = _dot2_lhs(w, seg)
        o = o + _dot2_lhs(att, segt) * vs

    gate = gain_ref[...] * _silu(g_ref[...])
    vb = v.astype(BF16)
    row_seq = lax.broadcasted_iota(jnp.int32, (rows, kd), 0) // t
    lane_seq = lax.broadcasted_iota(jnp.int32, (kd, rows), 1) // t
    for h in range(heads):
        sl = slice(h * kd, (h + 1) * kd)
        k_t = kdec[:, sl].T
        w_t = whole[:, sl].T
        o_h = o[:, sl]
        for b in range(bb):
            s = s_ref[0, b, h]
            o_h = o_h + jnp.where(row_seq == b, _dot(qd[:, sl], s.astype(BF16)), 0.0)
            k_b = jnp.where(lane_seq == b, k_t, 0.0).astype(BF16)
            so_ref[0, b, h] = w_t[:, b * t:b * t + 1] * s + _dot(k_b, vb[:, sl])
        o_h = o_h * lax.rsqrt(jnp.mean(o_h * o_h, axis=-1, keepdims=True) + EPS)
        o_ref[:, sl] = (o_h * gate[:, sl]).astype(BF16)


def _step_blocking(st, scan_steps):
    t = st.dec_seq
    bb = BF16_SUBLANES // math.gcd(BF16_SUBLANES, t)
    n_blocks = st.dec_batch // bb
    if n_blocks > scan_steps:
        assert n_blocks % scan_steps == 0
        bb, n_blocks = bb * (n_blocks // scan_steps), scan_steps
    assert st.dec_batch == bb * n_blocks and scan_steps % n_blocks == 0 and st.n_prompt % (bb * t) == 0
    return bb, scan_steps // n_blocks


def _hgrn_mixer_kernel(c, nbits, hb, kd, bb, t, stride, n_in, *refs):
    (qp, zp, ip, gp, lb_ref, gain_ref, ms_ref, lv_ref,
     qs, zs, is_, gs, sums_ref, sh_ref, seg_ref, segt_ref, s_ref) = refs[:17]
    o_p, sp_ref, o_s, so_ref, s_scr = refs[n_in:]
    _hgrn_kernel(c, nbits, hb, kd, qp, zp, ip, gp, lb_ref, gain_ref, ms_ref, lv_ref, o_p, sp_ref, s_scr)

    @pl.when(pl.program_id(2) % stride == 0)
    def _():
        _hgrn_step_kernel(bb, t, hb, kd, qs, zs, is_, gs, lb_ref, gain_ref, sums_ref, sh_ref, seg_ref, segt_ref,
                          s_ref, o_s, so_ref)


def _hgrn_mixer(st, proj, lb, gain, state_all, j, so_prev):
    _, _, heads, kd, vd = state_all.shape
    width = heads * kd
    c = min(HGRN_CHUNK, st.seq)
    hb = min(HGRN_HEADS_PER_STEP, heads)
    n_chunks = st.seq // c
    groups = heads // hb
    mstack, level, nbits = _hgrn_tables(c)
    t = st.dec_seq
    bb, stride = _step_blocking(st, st.batch * n_chunks)
    rows = bb * t
    rb0 = st.n_prompt // rows
    sums, shifts = _step_tables(bb, t)
    seg = (np.arange(hb * kd)[:, None] // kd == np.arange(LANES)[None, :]).astype(np.float32)
    assert hb <= LANES

    def sblock(b, ci):
        return (b * n_chunks + ci) // stride

    def pseg(jj):
        return pl.BlockSpec((c, hb * kd), lambda b, h, ci: (b * n_chunks + ci, jj * groups + h))

    def sseg(jj):
        return pl.BlockSpec((rows, hb * kd), lambda b, h, ci: (rb0 + sblock(b, ci), jj * groups + h))

    full = lambda a: pl.BlockSpec(a.shape, lambda b, h, ci: (0,) * a.ndim)
    head_row = pl.BlockSpec((1, hb * kd), lambda b, h, ci: (0, h))
    state = pl.BlockSpec((1, bb, hb, kd, vd), lambda b, h, ci: (j, sblock(b, ci), h, 0, 0))
    args = [proj, proj, proj, proj, lb.reshape(1, width), gain.reshape(1, width),
            jnp.asarray(np.concatenate([mstack, mstack], 1), BF16), jnp.asarray(level),
            proj, proj, proj, proj, jnp.asarray(sums, BF16), jnp.asarray(shifts, BF16),
            jnp.asarray(seg, BF16), jnp.asarray(seg.T, BF16), state_all]
    in_specs = [pseg(0), pseg(1), pseg(2), pseg(3), head_row, head_row,
                pl.BlockSpec((mstack.shape[0], 2 * c), lambda b, h, ci: (0, 0)), full(level),
                sseg(0), sseg(1), sseg(2), sseg(3), full(sums), full(shifts), full(seg), full(seg.T), state]
    aliases = {}
    if so_prev is not None:
        args.append(so_prev)
        in_specs.append(pl.BlockSpec(memory_space=pl.ANY))
        aliases[len(args) - 1] = 3
    return pl.pallas_call(
        functools.partial(_hgrn_mixer_kernel, c, nbits, hb, kd, bb, t, stride, len(args)),
        grid=(st.batch, groups, n_chunks),
        in_specs=in_specs,
        out_specs=[pl.BlockSpec((c, hb * vd), lambda b, h, ci: (b * n_chunks + ci, h)),
                   pl.BlockSpec((1, hb, kd, vd), lambda b, h, ci: (b, h, 0, 0)),
                   pl.BlockSpec((rows, hb * vd), lambda b, h, ci: (sblock(b, ci), h)),
                   state],
        out_shape=[jax.ShapeDtypeStruct((st.n_prompt, heads * vd), BF16),
                   jax.ShapeDtypeStruct((st.batch, heads, kd, vd), F32),
                   jax.ShapeDtypeStruct((st.n_sample, heads * vd), BF16),
                   jax.ShapeDtypeStruct(state_all.shape, F32)],
        scratch_shapes=[pltpu.VMEM((hb, kd, vd), F32)],
        input_output_aliases=aliases,
        compiler_params=_cparams("arbitrary", "arbitrary", "arbitrary"),
        name="hgrn_mixer",
    )(*args)


def _rotate(x, cos, sin):
    half = x.shape[-1] // 2
    x1, x2 = x[:, :half], x[:, half:]
    return jnp.concatenate([x1 * cos - x2 * sin, x1 * sin + x2 * cos], axis=-1)


def _group_norm_gate(o, gain, g):
    mu = jnp.mean(o, axis=-1, keepdims=True)
    var = jnp.mean(jnp.square(o - mu), axis=-1, keepdims=True)
    return (_silu(g) * ((o - mu) * lax.rsqrt(var + EPS) * gain)).astype(BF16)


def _ret_kernel(c, dk, dv, lgs, q_ref, k_ref, v_ref, g_ref, cos_ref, sin_ref, gain_ref, o_ref, so_ref,
                s_scr, intra_scr):
    ci = pl.program_id(2)
    t_col = lax.broadcasted_iota(jnp.int32, (c, 1), 0).astype(F32)

    @pl.when(ci == 0)
    def _():
        s_scr[...] = jnp.zeros_like(s_scr)
        rel = (lax.broadcasted_iota(jnp.int32, (c, c), 0)
               - lax.broadcasted_iota(jnp.int32, (c, c), 1)).astype(F32)
        for h, lg in enumerate(lgs):
            intra_scr[h] = jnp.where(rel >= 0.0, jnp.exp(jnp.maximum(rel, 0.0) * lg), 0.0)

    cos, sin = cos_ref[...], sin_ref[...]
    for h, lg in enumerate(lgs):
        q = _rotate(q_ref[:, h * dk:(h + 1) * dk], cos, sin)
        k = _rotate(k_ref[:, h * dk:(h + 1) * dk], cos, sin) * (dk ** -0.5)
        v = v_ref[:, h * dv:(h + 1) * dv].astype(BF16)
        inter = jnp.exp((t_col + 1.0) * lg)
        tail = jnp.exp((c - 1.0 - t_col) * lg)
        whole = jnp.exp(jnp.full((1, 1), c, F32) * lg)
        s = s_scr[h]
        att = _dot_nt(q.astype(BF16), k.astype(BF16)) * intra_scr[h]
        o = _dot(att.astype(BF16), v) + _dot((q * inter).astype(BF16), s.astype(BF16))
        s_scr[h] = whole * s + _dot((k * tail).T.astype(BF16), v)
        o_ref[:, h * dv:(h + 1) * dv] = _group_norm_gate(
            o, gain_ref[:, h * dv:(h + 1) * dv], g_ref[:, h * dv:(h + 1) * dv])

    @pl.when(ci == pl.num_programs(2) - 1)
    def _():
        so_ref[0] = s_scr[...]


def _ret_step_kernel(bb, t, dk, dv, lgs, q_ref, k_ref, v_ref, g_ref, cos_ref, sin_ref, pos_ref, rel_ref,
                     gain_ref, s_ref, o_ref, so_ref):
    rows = bb * t
    cos, sin = cos_ref[...], sin_ref[...]
    t_col = pos_ref[...][:, :1]
    rel = rel_ref[...]
    row_seq = lax.broadcasted_iota(jnp.int32, (rows, dv), 0) // t
    lane_seq = lax.broadcasted_iota(jnp.int32, (dk, rows), 1) // t
    for h, lg in enumerate(lgs):
        q = _rotate(q_ref[:, h * dk:(h + 1) * dk], cos, sin)
        k = _rotate(k_ref[:, h * dk:(h + 1) * dk], cos, sin) * (dk ** -0.5)
        v = v_ref[:, h * dv:(h + 1) * dv].astype(BF16)
        inter = jnp.exp((t_col + 1.0) * lg)
        tail = jnp.exp((t - 1.0 - t_col) * lg)
        intra = jnp.where(rel >= 0.0, jnp.exp(jnp.maximum(rel, 0.0) * lg), 0.0)
        whole = jnp.exp(jnp.full((1, 1), t, F32) * lg)

        att = _dot_nt(q.astype(BF16), k.astype(BF16)) * intra
        o = _dot(att.astype(BF16), v)
        qd = (q * inter).astype(BF16)
        k_t = (k * tail).T
        for b in range(bb):
            s = s_ref[0, b, h]
            o = o + jnp.where(row_seq == b, _dot(qd, s.astype(BF16)), 0.0)
            k_b = jnp.where(lane_seq == b, k_t, 0.0).astype(BF16)
            so_ref[0, b, h] = whole * s + _dot(k_b, v)
        o_ref[:, h * dv:(h + 1) * dv] = _group_norm_gate(
            o, gain_ref[:, h * dv:(h + 1) * dv], g_ref[:, h * dv:(h + 1) * dv])


def _ret_mixer_kernel(c, hb, dk, dv, bb, t, stride, n_in, *refs):
    (lg_ref, qp, kp, vp, gp, cosp, sinp, gain_ref,
     qs, ks, vs, gs, coss, sins, pos_ref, rel_ref, s_ref) = refs[:17]
    o_p, sp_ref, o_s, so_ref, s_scr, intra_scr = refs[n_in:]
    lgs = [lg_ref[pl.program_id(1) * hb + h] for h in range(hb)]
    _ret_kernel(c, dk, dv, lgs, qp, kp, vp, gp, cosp, sinp, gain_ref, o_p, sp_ref, s_scr, intra_scr)

    @pl.when(pl.program_id(2) % stride == 0)
    def _():
        _ret_step_kernel(bb, t, dk, dv, lgs, qs, ks, vs, gs, coss, sins, pos_ref, rel_ref, gain_ref,
                         s_ref, o_s, so_ref)


def _ret_mixer(st, proj, cos_p, sin_p, cos_s, sin_s, log_gamma, gain, state_all, j, so_prev):
    _, _, heads, dk, dv = state_all.shape
    c = min(RET_CHUNK, st.seq)
    hb = min(RET_HEADS_PER_STEP, heads)
    groups = heads // hb
    n_chunks = st.seq // c
    v_off = 2 * heads * dk // (hb * dv)
    g_off = v_off + groups
    t = st.dec_seq
    bb, stride = _step_blocking(st, st.batch * n_chunks)
    rows = bb * t
    rb0 = st.n_prompt // rows
    r = np.arange(rows)
    same = (r[:, None] // t) == (r[None, :] // t)
    rel = np.where(same & (r[None, :] <= r[:, None]), r[:, None] - r[None, :], -1).astype(np.float32)
    pos = np.broadcast_to((r % t).astype(np.float32)[:, None], (rows, LANES))
    cos_r, sin_r = jnp.tile(cos_s, (bb, 1)), jnp.tile(sin_s, (bb, 1))

    def sblock(b, ci):
        return (b * n_chunks + ci) // stride

    def pcol(width, off):
        return pl.BlockSpec((c, hb * width), lambda b, h, ci: (b * n_chunks + ci, off + h))

    def scol(width, off):
        return pl.BlockSpec((rows, hb * width), lambda b, h, ci: (rb0 + sblock(b, ci), off + h))

    full = lambda a: pl.BlockSpec(a.shape, lambda b, h, ci: (0,) * a.ndim)
    rope = pl.BlockSpec((c, dk // 2), lambda b, h, ci: (ci, 0))
    state = pl.BlockSpec((1, bb, hb, dk, dv), lambda b, h, ci: (j, sblock(b, ci), h, 0, 0))
    args = [log_gamma, proj, proj, proj, proj, cos_p, sin_p, gain.reshape(1, heads * dv),
            proj, proj, proj, proj, cos_r, sin_r, jnp.asarray(pos), jnp.asarray(rel), state_all]
    in_specs = [pl.BlockSpec(memory_space=pltpu.SMEM),
                pcol(dk, 0), pcol(dk, groups), pcol(dv, v_off), pcol(dv, g_off), rope, rope,
                pl.BlockSpec((1, hb * dv), lambda b, h, ci: (0, h)),
                scol(dk, 0), scol(dk, groups), scol(dv, v_off), scol(dv, g_off),
                full(cos_r), full(sin_r), full(pos), full(rel), state]
    aliases = {}
    if so_prev is not None:
        args.append(so_prev)
        in_specs.append(pl.BlockSpec(memory_space=pl.ANY))
        aliases[len(args) - 1] = 3
    return pl.pallas_call(
        functools.partial(_ret_mixer_kernel, c, hb, dk, dv, bb, t, stride, len(args)),
        grid=(st.batch, groups, n_chunks),
        in_specs=in_specs,
        out_specs=[pl.BlockSpec((c, hb * dv), lambda b, h, ci: (b * n_chunks + ci, h)),
                   pl.BlockSpec((1, hb, dk, dv), lambda b, h, ci: (b, h, 0, 0)),
                   pl.BlockSpec((rows, hb * dv), lambda b, h, ci: (sblock(b, ci), h)),
                   state],
        out_shape=[jax.ShapeDtypeStruct((st.n_prompt, heads * dv), BF16),
                   jax.ShapeDtypeStruct((st.batch, heads, dk, dv), F32),
                   jax.ShapeDtypeStruct((st.n_sample, heads * dv), BF16),
                   jax.ShapeDtypeStruct(state_all.shape, F32)],
        scratch_shapes=[pltpu.VMEM((hb, dk, dv), F32), pltpu.VMEM((hb, c, c), F32)],
        input_output_aliases=aliases,
        compiler_params=_cparams("arbitrary", "arbitrary", "arbitrary"),
        name="ret_mixer",
    )(*args)


def _rope_tables(pos, half):
    inv = 1.0 / (ROPE_BASE ** jnp.linspace(0.0, 1.0, half, dtype=F32))
    ang = pos[:, None] * inv[None]
    return jnp.cos(ang), jnp.sin(ang)


def _expert_kernel(ts_ref, na_ref, tk_ref, eseq_ref, nk_ref, x_ref, w_ref, wg_hbm, wu_hbm, wd_hbm, o_ref,
                   wgb, wub, wdb, stage_g, stage_u, stage_d, sems):
    t = pl.program_id(0)
    active = t < na_ref[0]
    k = tk_ref[t]
    new_expert = (t == 0) | (k != tk_ref[jnp.maximum(t - 1, 0)])

    def weight_copies(kk, slot):
        e = eseq_ref[kk]
        return (pltpu.make_async_copy(wg_hbm.at[e], stage_g.at[slot], sems.at[0, slot]),
                pltpu.make_async_copy(wu_hbm.at[e], stage_u.at[slot], sems.at[1, slot]),
                pltpu.make_async_copy(wd_hbm.at[e], stage_d.at[slot], sems.at[2, slot]))

    @pl.when(t == 0)
    def _():
        for cp in weight_copies(0, 0):
            cp.start()

    @pl.when(active & new_expert)
    def _():
        slot = lax.rem(k, 2)

        @pl.when(k + 1 < nk_ref[0])
        def _():
            for cp in weight_copies(k + 1, 1 - slot):
                cp.start()

        for cp in weight_copies(k, slot):
            cp.wait()
        wgb[...] = stage_g[slot].astype(BF16)
        wub[...] = stage_u[slot].astype(BF16)
        wdb[...] = stage_d[slot].astype(BF16)

    @pl.when(active)
    def _():
        x = x_ref[...].astype(BF16)
        hg = _silu(_dot(x, wgb[...])) * _dot(x, wub[...])
        o_ref[...] = _dot((hg * w_ref[...]).astype(BF16), wdb[...])

    @pl.when(jnp.logical_not(active))
    def _():
        o_ref[...] = jnp.zeros_like(o_ref)


def _experts(x_sorted, w_sorted, tile_src, n_active, tile_pos, expert_seq, n_used, w_gate, w_up, w_down):
    rows = x_sorted.shape[0]
    _, d, ff = w_gate.shape
    tm = EXPERT_TILE
    hbm = pl.BlockSpec(memory_space=pl.ANY)
    grid_spec = pltpu.PrefetchScalarGridSpec(
        num_scalar_prefetch=5,
        grid=(rows // tm,),
        in_specs=[pl.BlockSpec((tm, d), lambda t, ts, *_: (ts[t], 0)),
                  pl.BlockSpec((tm, 1), lambda t, ts, *_: (ts[t], 0)),
                  hbm, hbm, hbm],
        out_specs=pl.BlockSpec((tm, d), lambda t, *_: (t, 0)),
        scratch_shapes=[pltpu.VMEM((d, ff), BF16), pltpu.VMEM((d, ff), BF16), pltpu.VMEM((ff, d), BF16),
                        pltpu.VMEM((2, d, ff), F32), pltpu.VMEM((2, d, ff), F32), pltpu.VMEM((2, ff, d), F32),
                        pltpu.SemaphoreType.DMA((3, 2))],
    )
    return pl.pallas_call(
        _expert_kernel,
        grid_spec=grid_spec,
        out_shape=jax.ShapeDtypeStruct((rows, d), F32),
        compiler_params=_cparams("arbitrary"),
        name="experts",
    )(tile_src, n_active, tile_pos, expert_seq, n_used, x_sorted, w_sorted, w_gate, w_up, w_down)


def _take_rows(a, idx):
    return a.at[idx].get(mode="promise_in_bounds")


def _step_lookup(table, starts, x):
    jumps = table[1:] - table[:-1]
    return table[0] + jnp.sum(jnp.where(x[None, :] >= starts[1:, None], jumps[:, None], 0), axis=0)


def _dispatch_plan(eid, ew, n_experts_total):
    n, top_k = eid.shape
    tm = EXPERT_TILE
    a = n * top_k
    n_tiles = a // tm + n_experts_total
    rows = n_tiles * tm
    i32 = jnp.int32
    q = jnp.arange(a, dtype=i32)
    sorted_e, order, sorted_w = lax.sort((eid.reshape(a), q, ew.reshape(a)), num_keys=1, is_stable=True)
    bounds = jnp.searchsorted(sorted_e, jnp.arange(n_experts_total + 1, dtype=i32), side="left").astype(i32)
    off, end = bounds[:-1], bounds[1:]
    padded = (end - off + tm - 1) // tm * tm
    pad_end = jnp.cumsum(padded)
    pad_off = pad_end - padded
    delta = pad_off - off
    _, pos = lax.sort((order, q + _step_lookup(delta, off, q)), num_keys=1)
    n_active = (pad_end[-1] // tm).astype(i32)
    tile_src = jnp.minimum(jnp.arange(n_tiles, dtype=i32), n_active - 1)
    experts = jnp.arange(n_experts_total, dtype=i32)
    used = padded > 0
    n_used = jnp.sum(used, dtype=i32)
    expert_seq = jnp.minimum(jnp.sort(jnp.where(used, experts, n_experts_total)), n_experts_total - 1)
    tile_pos = _step_lookup(jnp.cumsum(used, dtype=i32) - 1, pad_off, tile_src * tm)
    p = jnp.arange(rows, dtype=i32)
    src = p - _step_lookup(delta, pad_off, p)
    valid = (src < _step_lookup(end, pad_off, p)) & (p < pad_end[-1])
    src = jnp.clip(src, 0, a - 1)
    row_token = jnp.where(valid, _take_rows(order, src) // top_k, p % n)
    row_w = jnp.where(valid, _take_rows(sorted_w, src), 0.0)
    return (row_token, row_w.reshape(rows, 1), pos.reshape(n, top_k), tile_src, n_active.reshape(1),
            tile_pos, expert_seq, n_used.reshape(1))


def _combine_kernel(n_prompt_tiles, x_ref, ya_ref, yb_ref, gr, gt, shr, sht, scr, sct, o_ref, hn_ref):
    is_s = pl.program_id(0) >= n_prompt_tiles
    x = x_ref[...] + _pick(is_s, gr, gt) * (ya_ref[...] + yb_ref[...])
    o_ref[...] = x
    hn_ref[...] = (_rms(x) * (1.0 + _pick(is_s, scr, sct)) + _pick(is_s, shr, sht)).astype(BF16)


def _combine_final_kernel(n_prompt_tiles, x_ref, ya_ref, yb_ref, gr, gt, fg_ref, op_ref, os_ref):
    is_s = pl.program_id(0) >= n_prompt_tiles
    x = x_ref[...] + _pick(is_s, gr, gt) * (ya_ref[...] + yb_ref[...])
    y = _rms(x) * fg_ref[...]

    @pl.when(jnp.logical_not(is_s))
    def _():
        op_ref[...] = y

    @pl.when(is_s)
    def _():
        os_ref[...] = y


def _combine(st, x, y_pairs, mod_row, mod_tok, layer, j_gate, final_gain, final):
    d = st.d
    tile = pl.BlockSpec((st.tm, d), lambda i: (i, 0))
    n_tiles = st.n_tiles
    common = [tile, tile, pl.BlockSpec((st.tm, d), lambda i: (n_tiles + i, 0)),
              *st.mod_specs(layer, d, lambda i: j_gate)]
    if final:
        return pl.pallas_call(
            functools.partial(_combine_final_kernel, st.n_prompt_tiles),
            grid=(st.n_tiles,),
            in_specs=[*common, pl.BlockSpec((1, d), lambda i: (0, 0))],
            out_specs=[pl.BlockSpec((st.tm, d), lambda i: (jnp.minimum(i, n_tiles - 2), 0)),
                       pl.BlockSpec((st.tm, d), lambda i: (0, 0))],
            out_shape=[jax.ShapeDtypeStruct((st.n_prompt, d), F32), jax.ShapeDtypeStruct((st.n_sample, d), F32)],
            compiler_params=_cparams("arbitrary"),
            name="moe_combine_final",
        )(x, y_pairs, y_pairs, mod_row, mod_tok, final_gain.reshape(1, d)), None
    return pl.pallas_call(
        functools.partial(_combine_kernel, st.n_prompt_tiles),
        grid=(st.n_tiles,),
        in_specs=[*common, *st.mod_specs(layer + 1, d, lambda i: 0), *st.mod_specs(layer + 1, d, lambda i: 1)],
        out_specs=[tile, tile],
        out_shape=[jax.ShapeDtypeStruct((st.n, d), F32), jax.ShapeDtypeStruct((st.n, d), BF16)],
        compiler_params=_cparams("parallel"),
        name="moe_combine",
    )(x, y_pairs, y_pairs, mod_row, mod_tok, mod_row, mod_tok, mod_row, mod_tok)


def kernel(x_prompt, x_sample, state_hgrn, state_ret, c_prompt, c_sample, ada_w, ada_b, hgrn_w_in, hgrn_lb_logits, hgrn_norm_gain, hgrn_w_out, ret_w_in, ret_norm_gain, ret_w_out, moe_router_group, moe_router_expert, moe_w_gate, moe_w_up, moe_w_down, final_norm_gain):
    batch, seq, d = x_prompt.shape
    dec_batch, dec_seq, _ = x_sample.shape
    depth = ada_w.shape[0]
    _, _, a_heads, a_kd, a_vd = state_hgrn.shape
    _, _, b_heads, b_dk, b_dv = state_ret.shape
    _, n_groups, _, n_exp = moe_router_expert.shape
    ff = moe_w_gate.shape[-1]
    n_exp_total = n_groups * n_exp
    st = _Stream(batch, seq, dec_batch, dec_seq, d)

    p = jax.nn.softmax(hgrn_lb_logits.astype(F32), axis=0)
    lb_all = jnp.cumsum(p, axis=0) - p[0:1]
    log_gamma = jnp.log1p(-jnp.exp2(-5.0 - jnp.arange(b_heads, dtype=F32)))
    cos_p, sin_p = _rope_tables(jnp.arange(seq, dtype=F32), b_dk // 2)
    cos_s, sin_s = _rope_tables(jnp.arange(dec_seq, dtype=F32) + PAST_LEN, b_dk // 2)
    w_gate = moe_w_gate.reshape(depth * n_exp_total, d, ff)
    w_up = moe_w_up.reshape(depth * n_exp_total, d, ff)
    w_down = moe_w_down.reshape(depth * n_exp_total, ff, d)

    c_rows = -(-batch // BF16_SUBLANES) * BF16_SUBLANES
    mod_row, mod_tok = _ada_mod(jnp.pad(c_prompt, ((0, c_rows - batch), (0, 0))),
                                jnp.repeat(c_sample, dec_seq, axis=0), ada_w, ada_b)
    mod_row = mod_row[:, :batch].reshape(depth * batch, 1, 6 * d)

    x = jnp.concatenate([x_prompt.reshape(-1, d), x_sample.reshape(-1, d)], 0)
    hg_p, rt_p, hg_s, rt_s = [], [], None, None
    hn = _norm_mod(st, x, mod_row, mod_tok, 0, 0, 1)
    for l in range(depth):
        j = l // 2
        if l % 2 == 0:
            proj = _proj(st, hn, hgrn_w_in, j)
            o_p, s_p, o_s, hg_s = _hgrn_mixer(st, proj, lb_all[j], hgrn_norm_gain[j], state_hgrn, j, hg_s)
            hg_p.append(s_p)
            w_out = hgrn_w_out
        else:
            proj = _proj(st, hn, ret_w_in, j)
            o_p, s_p, o_s, rt_s = _ret_mixer(st, proj, cos_p, sin_p, cos_s, sin_s, log_gamma,
                                             ret_norm_gain[j], state_ret, j, rt_s)
            rt_p.append(s_p)
            w_out = ret_w_out
        x = _proj_residual(st, o_p, o_s, w_out, j, x, mod_row, mod_tok, l, 2)

        w_router = jnp.concatenate(
            [moe_router_group[l], jnp.moveaxis(moe_router_expert[l], 0, 1).reshape(d, n_exp_total)], 1)
        w_router = jnp.pad(w_router, ((0, 0), (0, LANES - w_router.shape[1])))
        hx, eid, ew = _norm_mod_router(st, x, mod_row, mod_tok, l, 3, 4, w_router, n_groups, n_exp)
        row_token, row_w, pos, tile_src, n_active, tile_pos, expert_seq, n_used = _dispatch_plan(
            eid, ew, n_exp_total)
        y_sorted = _experts(_take_rows(hx, row_token), row_w, tile_src, n_active, tile_pos,
                            expert_seq + l * n_exp_total, n_used, w_gate, w_up, w_down)
        y_pairs = _take_rows(y_sorted, pos.T.reshape(-1))
        x, hn = _combine(st, x, y_pairs, mod_row, mod_tok, l, 5, final_norm_gain, l == depth - 1)

    y_prompt = x[0].reshape(batch, seq, d)
    y_sample = x[1].reshape(dec_batch, dec_seq, d)
    return (y_prompt, y_sample, jnp.stack(hg_p), jnp.stack(rt_p), hg_s, rt_s)
```

```python
import functools
import math

import numpy as np
import jax
import jax.numpy as jnp
from jax import lax
from jax.experimental import pallas as pl
from jax.experimental.pallas import tpu as pltpu

F32 = jnp.float32
BF16 = jnp.bfloat16
EPS = 1e-6
PAST_LEN = 16384
ROPE_BASE = 10000.0

V7X_VMEM_LIMIT_BYTES = 56 * 1024 * 1024
LANES = 128
BF16_SUBLANES = 16

HGRN_CHUNK = 128
HGRN_HEADS_PER_STEP = 8
HGRN_HEADS_PER_GROUP = 8
RET_CHUNK = 256
RET_HEADS_PER_STEP = 2
EXPERT_TILE = 256
PROJ_TN = 1024
PROJ_IN_ROW_TILES = 8
ADA_TN = 1024


def _cparams(*sem):
    return pltpu.CompilerParams(dimension_semantics=sem, vmem_limit_bytes=V7X_VMEM_LIMIT_BYTES)


def _dot(a, b):
    return jnp.dot(a, b, preferred_element_type=F32)


def _dot_nt(a, b):
    return lax.dot_general(a, b, (((1,), (1,)), ((), ())), preferred_element_type=F32)


def _dot2(m, x):
    hi = x.astype(BF16)
    lo = (x - hi.astype(F32)).astype(BF16)
    return _dot(m, hi) + _dot(m, lo)


def _dot2_wide(mm, x):
    hi = x.astype(BF16)
    lo = (x - hi.astype(F32)).astype(BF16)
    return _dot(mm, jnp.concatenate([hi, lo], axis=0))


def _dot2_lhs(x, m):
    hi = x.astype(BF16)
    lo = (x - hi.astype(F32)).astype(BF16)
    return _dot(hi, m) + _dot(lo, m)


def _silu(x):
    return x * jax.nn.sigmoid(x)


def _rms(x):
    return x * lax.rsqrt(jnp.mean(x * x, axis=-1, keepdims=True) + EPS)


def _ada_kernel(cr_ref, ct_ref, w_ref, b_ref, or_ref, ot_ref):
    w = w_ref[0].astype(BF16)
    or_ref[0] = _dot(_silu(cr_ref[...]).astype(BF16), w) + b_ref[0]
    ot_ref[0] = _dot(_silu(ct_ref[...]).astype(BF16), w) + b_ref[0]


def _ada_mod(c_row, c_tok, ada_w, ada_b):
    depth, d, n6 = ada_w.shape
    r, t = c_row.shape[0], c_tok.shape[0]
    tn = min(ADA_TN, n6)
    return pl.pallas_call(
        _ada_kernel,
        grid=(depth, n6 // tn),
        in_specs=[pl.BlockSpec((r, d), lambda l, n: (0, 0)),
                  pl.BlockSpec((t, d), lambda l, n: (0, 0)),
                  pl.BlockSpec((1, d, tn), lambda l, n: (l, 0, n)),
                  pl.BlockSpec((1, 1, tn), lambda l, n: (l, 0, n))],
        out_specs=[pl.BlockSpec((1, r, tn), lambda l, n: (l, 0, n)),
                   pl.BlockSpec((1, t, tn), lambda l, n: (l, 0, n))],
        out_shape=[jax.ShapeDtypeStruct((depth, r, n6), F32), jax.ShapeDtypeStruct((depth, t, n6), F32)],
        compiler_params=_cparams("parallel", "parallel"),
        name="ada_mod",
    )(c_row, c_tok, ada_w, ada_b.reshape(depth, 1, n6))


class _Stream:
    def __init__(self, batch, seq, dec_batch, dec_seq, d):
        self.batch, self.seq, self.dec_batch, self.dec_seq, self.d = batch, seq, dec_batch, dec_seq, d
        self.n_prompt = batch * seq
        self.n_sample = dec_batch * dec_seq
        self.tm = self.n_sample
        assert seq % self.tm == 0 and self.tm % BF16_SUBLANES == 0
        self.tiles_per_seq = seq // self.tm
        self.n_prompt_tiles = self.n_prompt // self.tm
        self.n_tiles = self.n_prompt_tiles + 1
        self.n = self.n_prompt + self.n_sample

    def mod_specs(self, layer, width, col_block, tile_axis=0):
        tps, last, batch = self.tiles_per_seq, self.batch - 1, self.batch

        def row_map(*idx):
            return (layer * batch + jnp.minimum(idx[tile_axis] // tps, last), 0, col_block(*idx))
        return (pl.BlockSpec((1, 1, width), row_map),
                pl.BlockSpec((1, self.tm, width), lambda *idx: (layer, 0, col_block(*idx))))


def _pick(is_sample, row_ref, tok_ref):
    return jnp.where(is_sample, tok_ref[0], row_ref[0])


def _norm_mod_kernel(n_prompt_tiles, x_ref, shr, sht, scr, sct, o_ref):
    is_s = pl.program_id(0) >= n_prompt_tiles
    h = _rms(x_ref[...]) * (1.0 + _pick(is_s, scr, sct)) + _pick(is_s, shr, sht)
    o_ref[...] = h.astype(BF16)


def _norm_mod(st, x, mod_row, mod_tok, layer, j_shift, j_scale):
    d = st.d
    return pl.pallas_call(
        functools.partial(_norm_mod_kernel, st.n_prompt_tiles),
        grid=(st.n_tiles,),
        in_specs=[pl.BlockSpec((st.tm, d), lambda i: (i, 0)),
                  *st.mod_specs(layer, d, lambda i: j_shift), *st.mod_specs(layer, d, lambda i: j_scale)],
        out_specs=pl.BlockSpec((st.tm, d), lambda i: (i, 0)),
        out_shape=jax.ShapeDtypeStruct((st.n, d), BF16),
        compiler_params=_cparams("parallel"),
        name="norm_mod",
    )(x, mod_row, mod_tok, mod_row, mod_tok)


def _norm_mod_router_kernel(n_prompt_tiles, n_groups, n_experts, x_ref, shr, sht, scr, sct, wr_ref,
                            o_ref, eid_ref, ew_ref):
    is_s = pl.program_id(0) >= n_prompt_tiles
    h = _rms(x_ref[...]) * (1.0 + _pick(is_s, scr, sct)) + _pick(is_s, shr, sht)
    o_ref[...] = h

    logits = _dot(h.astype(BF16), wr_ref[...].astype(BF16))
    tm = logits.shape[0]
    lane = lax.broadcasted_iota(jnp.int32, (tm, LANES), 1)
    neg_inf = jnp.float32(-jnp.inf)

    def first_index_of_max(vals, vmax):
        return jnp.min(jnp.where(vals == vmax, lane, LANES), axis=1, keepdims=True)

    is_g = lane < n_groups
    lg = jnp.where(is_g, logits, neg_inf)
    mg = jnp.max(lg, axis=1, keepdims=True)
    eg = jnp.exp(lg - mg)
    pg = eg / jnp.sum(eg, axis=1, keepdims=True)
    gi = first_index_of_max(lg, mg)
    wg = jnp.sum(jnp.where(lane == gi, pg, 0.0), axis=1, keepdims=True)

    lo = n_groups + gi * n_experts
    sel = (lane >= lo) & (lane < lo + n_experts)
    le = jnp.where(sel, logits, neg_inf)
    me = jnp.max(le, axis=1, keepdims=True)
    ee = jnp.exp(le - me)
    pe = jnp.where(sel, ee / jnp.sum(ee, axis=1, keepdims=True), -1.0)
    p1 = jnp.max(pe, axis=1, keepdims=True)
    i1 = first_index_of_max(pe, p1)
    pe2 = jnp.where(lane == i1, -1.0, pe)
    p2 = jnp.max(pe2, axis=1, keepdims=True)
    i2 = first_index_of_max(pe2, p2)
    denom = p1 + p2
    col = lax.broadcasted_iota(jnp.int32, (tm, 2), 1)
    eid_ref[...] = jnp.where(col == 0, i1, i2) - n_groups
    ew_ref[...] = wg * jnp.where(col == 0, p1 / denom, p2 / denom)


def _norm_mod_router(st, x, mod_row, mod_tok, layer, j_shift, j_scale, w_router, n_groups, n_experts):
    d = st.d
    return pl.pallas_call(
        functools.partial(_norm_mod_router_kernel, st.n_prompt_tiles, n_groups, n_experts),
        grid=(st.n_tiles,),
        in_specs=[pl.BlockSpec((st.tm, d), lambda i: (i, 0)),
                  *st.mod_specs(layer, d, lambda i: j_shift), *st.mod_specs(layer, d, lambda i: j_scale),
                  pl.BlockSpec((d, LANES), lambda i: (0, 0))],
        out_specs=[pl.BlockSpec((st.tm, d), lambda i: (i, 0)),
                   pl.BlockSpec((st.tm, 2), lambda i: (i, 0)),
                   pl.BlockSpec((st.tm, 2), lambda i: (i, 0))],
        out_shape=[jax.ShapeDtypeStruct((st.n, d), F32),
                   jax.ShapeDtypeStruct((st.n, 2), jnp.int32),
                   jax.ShapeDtypeStruct((st.n, 2), F32)],
        compiler_params=_cparams("parallel"),
        name="norm_mod_router",
    )(x, mod_row, mod_tok, mod_row, mod_tok, w_router)


def _proj_kernel(a_ref, w_ref, o_ref, wb_ref):
    @pl.when(pl.program_id(1) == 0)
    def _():
        wb_ref[...] = w_ref[0].astype(BF16)
    o_ref[...] = _dot(a_ref[...], wb_ref[...])


def _proj(st, a, w_all, j):
    _, k, n_out = w_all.shape
    tn = min(PROJ_TN, n_out)
    tm = st.n // PROJ_IN_ROW_TILES if st.n % (PROJ_IN_ROW_TILES * BF16_SUBLANES) == 0 else st.tm
    return pl.pallas_call(
        _proj_kernel,
        grid=(n_out // tn, st.n // tm),
        in_specs=[pl.BlockSpec((tm, k), lambda n, m: (m, 0)),
                  pl.BlockSpec((1, k, tn), lambda n, m: (j, 0, n))],
        out_specs=pl.BlockSpec((tm, tn), lambda n, m: (m, n)),
        out_shape=jax.ShapeDtypeStruct((st.n, n_out), F32),
        scratch_shapes=[pltpu.VMEM((k, tn), BF16)],
        compiler_params=_cparams("parallel", "arbitrary"),
        name="proj_in",
    )(a, w_all)


def _proj_res_kernel(n_prompt_tiles, ap_ref, as_ref, w_ref, x_ref, gr, gt, o_ref, wb_ref):
    @pl.when(pl.program_id(1) == 0)
    def _():
        wb_ref[...] = w_ref[0].astype(BF16)
    is_s = pl.program_id(1) >= n_prompt_tiles
    a = jnp.where(is_s, as_ref[...], ap_ref[...])
    o_ref[...] = x_ref[...] + _pick(is_s, gr, gt) * _dot(a, wb_ref[...])


def _proj_residual(st, a_prompt, a_sample, w_all, j, x, mod_row, mod_tok, layer, j_gate):
    _, k, d = w_all.shape
    tn = min(PROJ_TN, d)
    nb = d // tn
    last_prompt = st.n_prompt_tiles - 1
    return pl.pallas_call(
        functools.partial(_proj_res_kernel, st.n_prompt_tiles),
        grid=(nb, st.n_tiles),
        in_specs=[pl.BlockSpec((st.tm, k), lambda n, m: (jnp.minimum(m, last_prompt), 0)),
                  pl.BlockSpec((st.tm, k), lambda n, m: (0, 0)),
                  pl.BlockSpec((1, k, tn), lambda n, m: (j, 0, n), pipeline_mode=pl.Buffered(1)),
                  pl.BlockSpec((st.tm, tn), lambda n, m: (m, n)),
                  *st.mod_specs(layer, tn, lambda n, m: j_gate * nb + n, tile_axis=1)],
        out_specs=pl.BlockSpec((st.tm, tn), lambda n, m: (m, n)),
        out_shape=jax.ShapeDtypeStruct((st.n, d), F32),
        scratch_shapes=[pltpu.VMEM((k, tn), BF16)],
        compiler_params=_cparams("parallel", "arbitrary"),
        name="proj_out",
    )(a_prompt, a_sample, w_all, x, mod_row, mod_tok)


def _hgrn_tables(c):
    nbits = int(math.log2(c))
    assert 1 << nbits == c
    r = np.arange(c)[:, None]
    j = np.arange(c)[None, :]
    mats = [(j <= r), (j > r)]
    for b in range(nbits):
        mid = ((r >> (b + 1)) << (b + 1)) + (1 << b)
        right = r >= mid
        mats.append(np.where(right, (j >= mid) & (j <= r), (j > r) & (j < mid)))
    level = np.full((c, c), -1, np.int32)
    level[r[:, 0], r[:, 0]] = 0
    diff = r ^ j
    for b in range(nbits):
        level[(j < r) & ((diff >> b) == 1)] = b + 1
    return np.concatenate(mats, 0).astype(np.float32), level, nbits


def _hgrn_kernel(c, nbits, hb, kd, q_ref, z_ref, i_ref, g_ref, lb_ref, gain_ref, ms_ref, lv_ref,
                 o_ref, so_ref, s_scr):
    ci = pl.program_id(2)

    @pl.when(ci == 0)
    def _():
        s_scr[...] = jnp.zeros_like(s_scr)

    lv = lv_ref[...]
    hg = min(HGRN_HEADS_PER_GROUP, hb)
    for g0 in range(0, hb, hg):
        gs = slice(g0 * kd, (g0 + hg) * kd)
        lb = lb_ref[:, gs]
        f = lb + (1.0 - lb) * jax.nn.sigmoid(z_ref[:, gs])
        logf = jnp.log(f)
        k = 1.0 - f
        q = _silu(q_ref[:, gs])
        v = i_ref[:, gs].astype(BF16)
        gate = gain_ref[:, gs] * _silu(g_ref[:, gs])
        hi = logf.astype(BF16)
        hl = jnp.concatenate([hi, (logf - hi.astype(F32)).astype(BF16)], axis=0)

        def decay_sum(r):
            return _dot(ms_ref[r * c:(r + 1) * c, :], hl)

        cum = decay_sum(0)
        qd = (q * jnp.exp(cum)).astype(BF16)
        kdec = k * jnp.exp(decay_sum(1))
        qb, kb = q.astype(BF16), k.astype(BF16)
        heads = [slice(i * kd, (i + 1) * kd) for i in range(hg)]
        att = [jnp.where(lv == 0, _dot_nt(qb[:, sl], kb[:, sl]), 0.0) for sl in heads]
        for b in range(nbits):
            gl = jnp.exp(decay_sum(2 + b)).astype(BF16)
            ql, kl = qb * gl, kb * gl
            att = [jnp.where(lv == b + 1, _dot_nt(ql[:, sl], kl[:, sl]), a) for sl, a in zip(heads, att)]
        for i, sl in enumerate(heads):
            s = s_scr[g0 + i]
            o = _dot(qd[:, sl], s.astype(BF16)) + _dot(att[i].astype(BF16), v[:, sl])
            whole = jnp.exp(jnp.broadcast_to(cum[c - 1:c, sl], (s.shape[1], s.shape[0])).T)
            s_scr[g0 + i] = whole * s + _dot(kdec[:, sl].T.astype(BF16), v[:, sl])
            o = o * lax.rsqrt(jnp.mean(o * o, axis=-1, keepdims=True) + EPS)
            o_ref[:, (g0 + i) * kd:(g0 + i + 1) * kd] = (o * gate[:, sl]).astype(BF16)

    @pl.when(ci == pl.num_programs(2) - 1)
    def _():
        so_ref[0] = s_scr[...]


def _step_tables(bb, t):
    rows = bb * t
    r = np.arange(rows)
    grp, pos = r // t, r % t
    same = grp[:, None] == grp[None, :]
    i, j = r[:, None], r[None, :]
    sums = [same & (j <= i), same & (j > i), same]
    for d in range(1, t):
        sums.append(same & (j <= i) & (j > i - d) & (pos[:, None] >= d))
    shifts = [same & (j == i - d) for d in range(1, t)]
    return (np.concatenate(sums, 0).astype(np.float32), np.concatenate(shifts, 0).astype(np.float32))


def _hgrn_step_kernel(bb, t, heads, kd, q_ref, z_ref, i_ref, g_ref, lb_ref, gain_ref, ms_ref, sh_ref,
                      seg_ref, segt_ref, s_ref, o_ref, so_ref):
    rows = bb * t
    lb = lb_ref[...]
    f = lb + (1.0 - lb) * jax.nn.sigmoid(z_ref[...])
    logf = jnp.log(f)
    k = 1.0 - f
    q = _silu(q_ref[...])
    v = i_ref[...]
    e = _dot2(ms_ref[...], logf)
    qd = (q * jnp.exp(e[0:rows])).astype(BF16)
    kdec = k * jnp.exp(e[rows:2 * rows])
    whole = jnp.exp(e[2 * rows:3 * rows])

    seg, segt = seg_ref[...], segt_ref[...]
    o = jnp.zeros_like(q)
    for d in range(t):
        if d == 0:
            ks, vs, w = k, v, q * k
        else:
            sh = sh_ref[(d - 1) * rows:d * rows]
            ks, vs = _dot2(sh, k), _dot2(sh, v)
            w = q * ks * jnp.exp(e[(2 + d) * rows:(3 + d) * rows])
        att = _dot2_lhs(w, seg)
        o = o + _dot2_lhs(att, segt) * vs

    gate = gain_ref[...] * _silu(g_ref[...])
    vb = v.astype(BF16)
    row_seq = lax.broadcasted_iota(jnp.int32, (rows, kd), 0) // t
    lane_seq = lax.broadcasted_iota(jnp.int32, (kd, rows), 1) // t
    for h in range(heads):
        sl = slice(h * kd, (h + 1) * kd)
        k_t = kdec[:, sl].T
        w_t = whole[:, sl].T
        o_h = o[:, sl]
        for b in range(bb):
            s = s_ref[0, b, h]
            o_h = o_h + jnp.where(row_seq == b, _dot(qd[:, sl], s.astype(BF16)), 0.0)
            k_b = jnp.where(lane_seq == b, k_t, 0.0).astype(BF16)
            so_ref[0, b, h] = w_t[:, b * t:b * t + 1] * s + _dot(k_b, vb[:, sl])
        o_h = o_h * lax.rsqrt(jnp.mean(o_h * o_h, axis=-1, keepdims=True) + EPS)
        o_ref[:, sl] = (o_h * gate[:, sl]).astype(BF16)


def _step_blocking(st, scan_steps):
    t = st.dec_seq
    bb = BF16_SUBLANES // math.gcd(BF16_SUBLANES, t)
    n_blocks = st.dec_batch // bb
    if n_blocks > scan_steps:
        assert n_blocks % scan_steps == 0
        bb, n_blocks = bb * (n_blocks // scan_steps), scan_steps
    assert st.dec_batch == bb * n_blocks and scan_steps % n_blocks == 0 and st.n_prompt % (bb * t) == 0
    return bb, scan_steps // n_blocks


def _hgrn_step_call_kernel(bb, t, heads, kd, n_in, *refs):
    _hgrn_step_kernel(bb, t, heads, kd, *refs[:11], *refs[n_in:])


def _hgrn_mixer(st, proj, lb, gain, state_all, j, so_prev):
    _, _, heads, kd, vd = state_all.shape
    width = heads * kd
    c = min(HGRN_CHUNK, st.seq)
    hb = min(HGRN_HEADS_PER_STEP, heads)
    n_chunks = st.seq // c
    groups = heads // hb
    mstack, level, nbits = _hgrn_tables(c)
    lb2, gain2 = lb.reshape(1, width), gain.reshape(1, width)

    def pseg(jj):
        return pl.BlockSpec((c, hb * kd), lambda b, h, ci: (b * n_chunks + ci, jj * groups + h))

    head_row = pl.BlockSpec((1, hb * kd), lambda b, h, ci: (0, h))
    o_p, s_p = pl.pallas_call(
        functools.partial(_hgrn_kernel, c, nbits, hb, kd),
        grid=(st.batch, groups, n_chunks),
        in_specs=[pseg(0), pseg(1), pseg(2), pseg(3), head_row, head_row,
                  pl.BlockSpec((mstack.shape[0], 2 * c), lambda b, h, ci: (0, 0)),
                  pl.BlockSpec(level.shape, lambda b, h, ci: (0, 0))],
        out_specs=[pl.BlockSpec((c, hb * vd), lambda b, h, ci: (b * n_chunks + ci, h)),
                   pl.BlockSpec((1, hb, kd, vd), lambda b, h, ci: (b, h, 0, 0))],
        out_shape=[jax.ShapeDtypeStruct((st.n_prompt, heads * vd), BF16),
                   jax.ShapeDtypeStruct((st.batch, heads, kd, vd), F32)],
        scratch_shapes=[pltpu.VMEM((hb, kd, vd), F32)],
        compiler_params=_cparams("parallel", "parallel", "arbitrary"),
        name="hgrn_scan",
    )(proj, proj, proj, proj, lb2, gain2,
      jnp.asarray(np.concatenate([mstack, mstack], 1), BF16), jnp.asarray(level))

    t = st.dec_seq
    bb, _ = _step_blocking(st, st.dec_batch)
    rows = bb * t
    rb0 = st.n_prompt // rows
    sums, shifts = _step_tables(bb, t)
    seg = (np.arange(width)[:, None] // kd == np.arange(LANES)[None, :]).astype(np.float32)
    assert heads <= LANES

    def sseg(jj):
        return pl.BlockSpec((rows, width), lambda i: (rb0 + i, jj))

    full = lambda a: pl.BlockSpec(a.shape, lambda i: (0,) * a.ndim)
    state = pl.BlockSpec((1, bb, heads, kd, vd), lambda i: (j, i, 0, 0, 0))
    args = [proj, proj, proj, proj, lb2, gain2, jnp.asarray(sums, BF16), jnp.asarray(shifts, BF16),
            jnp.asarray(seg, BF16), jnp.asarray(seg.T, BF16), state_all]
    in_specs = [sseg(0), sseg(1), sseg(2), sseg(3), full(lb2), full(gain2), full(sums), full(shifts),
                full(seg), full(seg.T), state]
    aliases = {}
    if so_prev is not None:
        args.append(so_prev)
        in_specs.append(pl.BlockSpec(memory_space=pl.ANY))
        aliases[len(args) - 1] = 1
    o_s, so = pl.pallas_call(
        functools.partial(_hgrn_step_call_kernel, bb, t, heads, kd, len(args)),
        grid=(st.dec_batch // bb,),
        in_specs=in_specs,
        out_specs=[pl.BlockSpec((rows, width), lambda i: (i, 0)), state],
        out_shape=[jax.ShapeDtypeStruct((st.n_sample, heads * vd), BF16),
                   jax.ShapeDtypeStruct(state_all.shape, F32)],
        input_output_aliases=aliases,
        compiler_params=_cparams("parallel"),
        name="hgrn_step",
    )(*args)
    return o_p, s_p, o_s, so


def _rotate(x, cos, sin):
    half = x.shape[-1] // 2
    x1, x2 = x[:, :half], x[:, half:]
    return jnp.concatenate([x1 * cos - x2 * sin, x1 * sin + x2 * cos], axis=-1)


def _group_norm_gate(o, gain, g):
    mu = jnp.mean(o, axis=-1, keepdims=True)
    var = jnp.mean(jnp.square(o - mu), axis=-1, keepdims=True)
    return (_silu(g) * ((o - mu) * lax.rsqrt(var + EPS) * gain)).astype(BF16)


def _ret_kernel(c, dk, dv, lgs, q_ref, k_ref, v_ref, g_ref, cos_ref, sin_ref, gain_ref, o_ref, so_ref,
                s_scr, intra_scr):
    ci = pl.program_id(2)
    t_col = lax.broadcasted_iota(jnp.int32, (c, 1), 0).astype(F32)

    @pl.when(ci == 0)
    def _():
        s_scr[...] = jnp.zeros_like(s_scr)
        rel = (lax.broadcasted_iota(jnp.int32, (c, c), 0)
               - lax.broadcasted_iota(jnp.int32, (c, c), 1)).astype(F32)
        for h, lg in enumerate(lgs):
            intra_scr[h] = jnp.where(rel >= 0.0, jnp.exp(jnp.maximum(rel, 0.0) * lg), 0.0)

    cos, sin = cos_ref[...], sin_ref[...]
    for h, lg in enumerate(lgs):
        q = _rotate(q_ref[:, h * dk:(h + 1) * dk], cos, sin)
        k = _rotate(k_ref[:, h * dk:(h + 1) * dk], cos, sin) * (dk ** -0.5)
        v = v_ref[:, h * dv:(h + 1) * dv].astype(BF16)
        inter = jnp.exp((t_col + 1.0) * lg)
        tail = jnp.exp((c - 1.0 - t_col) * lg)
        whole = jnp.exp(jnp.full((1, 1), c, F32) * lg)
        s = s_scr[h]
        att = _dot_nt(q.astype(BF16), k.astype(BF16)) * intra_scr[h]
        o = _dot(att.astype(BF16), v) + _dot((q * inter).astype(BF16), s.astype(BF16))
        s_scr[h] = whole * s + _dot((k * tail).T.astype(BF16), v)
        o_ref[:, h * dv:(h + 1) * dv] = _group_norm_gate(
            o, gain_ref[:, h * dv:(h + 1) * dv], g_ref[:, h * dv:(h + 1) * dv])

    @pl.when(ci == pl.num_programs(2) - 1)
    def _():
        so_ref[0] = s_scr[...]


def _ret_step_kernel(bb, t, dk, dv, lgs, q_ref, k_ref, v_ref, g_ref, cos_ref, sin_ref, pos_ref, rel_ref,
                     gain_ref, s_ref, o_ref, so_ref):
    rows = bb * t
    cos, sin = cos_ref[...], sin_ref[...]
    t_col = pos_ref[...][:, :1]
    rel = rel_ref[...]
    row_seq = lax.broadcasted_iota(jnp.int32, (rows, dv), 0) // t
    lane_seq = lax.broadcasted_iota(jnp.int32, (dk, rows), 1) // t
    for h, lg in enumerate(lgs):
        q = _rotate(q_ref[:, h * dk:(h + 1) * dk], cos, sin)
        k = _rotate(k_ref[:, h * dk:(h + 1) * dk], cos, sin) * (dk ** -0.5)
        v = v_ref[:, h * dv:(h + 1) * dv].astype(BF16)
        inter = jnp.exp((t_col + 1.0) * lg)
        tail = jnp.exp((t - 1.0 - t_col) * lg)
        intra = jnp.where(rel >= 0.0, jnp.exp(jnp.maximum(rel, 0.0) * lg), 0.0)
        whole = jnp.exp(jnp.full((1, 1), t, F32) * lg)

        att = _dot_nt(q.astype(BF16), k.astype(BF16)) * intra
        o = _dot(att.astype(BF16), v)
        qd = (q * inter).astype(BF16)
        k_t = (k * tail).T
        for b in range(bb):
            s = s_ref[0, b, h]
            o = o + jnp.where(row_seq == b, _dot(qd, s.astype(BF16)), 0.0)
            k_b = jnp.where(lane_seq == b, k_t, 0.0).astype(BF16)
            so_ref[0, b, h] = whole * s + _dot(k_b, v)
        o_ref[:, h * dv:(h + 1) * dv] = _group_norm_gate(
            o, gain_ref[:, h * dv:(h + 1) * dv], g_ref[:, h * dv:(h + 1) * dv])


def _ret_mixer_kernel(c, hb, dk, dv, bb, t, stride, n_in, *refs):
    (lg_ref, qp, kp, vp, gp, cosp, sinp, gain_ref,
     qs, ks, vs, gs, coss, sins, pos_ref, rel_ref, s_ref) = refs[:17]
    o_p, sp_ref, o_s, so_ref, s_scr, intra_scr = refs[n_in:]
    lgs = [lg_ref[pl.program_id(1) * hb + h] for h in range(hb)]
    _ret_kernel(c, dk, dv, lgs, qp, kp, vp, gp, cosp, sinp, gain_ref, o_p, sp_ref, s_scr, intra_scr)

    @pl.when(pl.program_id(2) % stride == 0)
    def _():
        _ret_step_kernel(bb, t, dk, dv, lgs, qs, ks, vs, gs, coss, sins, pos_ref, rel_ref, gain_ref,
                         s_ref, o_s, so_ref)


def _ret_mixer(st, proj, cos_p, sin_p, cos_s, sin_s, log_gamma, gain, state_all, j, so_prev):
    _, _, heads, dk, dv = state_all.shape
    c = min(RET_CHUNK, st.seq)
    hb = min(RET_HEADS_PER_STEP, heads)
    groups = heads // hb
    n_chunks = st.seq // c
    v_off = 2 * heads * dk // (hb * dv)
    g_off = v_off + groups
    t = st.dec_seq
    bb, stride = _step_blocking(st, st.batch * n_chunks)
    rows = bb * t
    rb0 = st.n_prompt // rows
    r = np.arange(rows)
    same = (r[:, None] // t) == (r[None, :] // t)
    rel = np.where(same & (r[None, :] <= r[:, None]), r[:, None] - r[None, :], -1).astype(np.float32)
    pos = np.broadcast_to((r % t).astype(np.float32)[:, None], (rows, LANES))
    cos_r, sin_r = jnp.tile(cos_s, (bb, 1)), jnp.tile(sin_s, (bb, 1))

    def sblock(b, ci):
        return (b * n_chunks + ci) // stride

    def pcol(width, off):
        return pl.BlockSpec((c, hb * width), lambda b, h, ci: (b * n_chunks + ci, off + h))

    def scol(width, off):
        return pl.BlockSpec((rows, hb * width), lambda b, h, ci: (rb0 + sblock(b, ci), off + h))

    full = lambda a: pl.BlockSpec(a.shape, lambda b, h, ci: (0,) * a.ndim)
    rope = pl.BlockSpec((c, dk // 2), lambda b, h, ci: (ci, 0))
    state = pl.BlockSpec((1, bb, hb, dk, dv), lambda b, h, ci: (j, sblock(b, ci), h, 0, 0))
    args = [log_gamma, proj, proj, proj, proj, cos_p, sin_p, gain.reshape(1, heads * dv),
            proj, proj, proj, proj, cos_r, sin_r, jnp.asarray(pos), jnp.asarray(rel), state_all]
    in_specs = [pl.BlockSpec(memory_space=pltpu.SMEM),
                pcol(dk, 0), pcol(dk, groups), pcol(dv, v_off), pcol(dv, g_off), rope, rope,
                pl.BlockSpec((1, hb * dv), lambda b, h, ci: (0, h)),
                scol(dk, 0), scol(dk, groups), scol(dv, v_off), scol(dv, g_off),
                full(cos_r), full(sin_r), full(pos), full(rel), state]
    aliases = {}
    if so_prev is not None:
        args.append(so_prev)
        in_specs.append(pl.BlockSpec(memory_space=pl.ANY))
        aliases[len(args) - 1] = 3
    return pl.pallas_call(
        functools.partial(_ret_mixer_kernel, c, hb, dk, dv, bb, t, stride, len(args)),
        grid=(st.batch, groups, n_chunks),
        in_specs=in_specs,
        out_specs=[pl.BlockSpec((c, hb * dv), lambda b, h, ci: (b * n_chunks + ci, h)),
                   pl.BlockSpec((1, hb, dk, dv), lambda b, h, ci: (b, h, 0, 0)),
                   pl.BlockSpec((rows, hb * dv), lambda b, h, ci: (sblock(b, ci), h)),
                   state],
        out_shape=[jax.ShapeDtypeStruct((st.n_prompt, heads * dv), BF16),
                   jax.ShapeDtypeStruct((st.batch, heads, dk, dv), F32),
                   jax.ShapeDtypeStruct((st.n_sample, heads * dv), BF16),
                   jax.ShapeDtypeStruct(state_all.shape, F32)],
        scratch_shapes=[pltpu.VMEM((hb, dk, dv), F32), pltpu.VMEM((hb, c, c), F32)],
        input_output_aliases=aliases,
        compiler_params=_cparams("arbitrary", "arbitrary", "arbitrary"),
        name="ret_mixer",
    )(*args)


def _rope_tables(pos, half):
    inv = 1.0 / (ROPE_BASE ** jnp.linspace(0.0, 1.0, half, dtype=F32))
    ang = pos[:, None] * inv[None]
    return jnp.cos(ang), jnp.sin(ang)


def _expert_kernel(ts_ref, na_ref, tk_ref, eseq_ref, nk_ref, x_ref, w_ref, wg_hbm, wu_hbm, wd_hbm, o_ref,
                   wgb, wub, wdb, stage_g, stage_u, stage_d, sems):
    t = pl.program_id(0)
    active = t < na_ref[0]
    k = tk_ref[t]
    new_expert = (t == 0) | (k != tk_ref[jnp.maximum(t - 1, 0)])

    def weight_copies(kk, slot):
        e = eseq_ref[kk]
        return (pltpu.make_async_copy(wg_hbm.at[e], stage_g.at[slot], sems.at[0, slot]),
                pltpu.make_async_copy(wu_hbm.at[e], stage_u.at[slot], sems.at[1, slot]),
                pltpu.make_async_copy(wd_hbm.at[e], stage_d.at[slot], sems.at[2, slot]))

    @pl.when(t == 0)
    def _():
        for cp in weight_copies(0, 0):
            cp.start()

    @pl.when(active & new_expert)
    def _():
        slot = lax.rem(k, 2)

        @pl.when(k + 1 < nk_ref[0])
        def _():
            for cp in weight_copies(k + 1, 1 - slot):
                cp.start()

        for cp in weight_copies(k, slot):
            cp.wait()
        wgb[...] = stage_g[slot].astype(BF16)
        wub[...] = stage_u[slot].astype(BF16)
        wdb[...] = stage_d[slot].astype(BF16)

    @pl.when(active)
    def _():
        x = x_ref[...].astype(BF16)
        hg = _silu(_dot(x, wgb[...])) * _dot(x, wub[...])
        o_ref[...] = _dot((hg * w_ref[...]).astype(BF16), wdb[...])

    @pl.when(jnp.logical_not(active))
    def _():
        o_ref[...] = jnp.zeros_like(o_ref)


def _experts(x_sorted, w_sorted, tile_src, n_active, tile_pos, expert_seq, n_used, w_gate, w_up, w_down):
    rows = x_sorted.shape[0]
    _, d, ff = w_gate.shape
    tm = EXPERT_TILE
    hbm = pl.BlockSpec(memory_space=pl.ANY)
    grid_spec = pltpu.PrefetchScalarGridSpec(
        num_scalar_prefetch=5,
        grid=(rows // tm,),
        in_specs=[pl.BlockSpec((tm, d), lambda t, ts, *_: (ts[t], 0)),
                  pl.BlockSpec((tm, 1), lambda t, ts, *_: (ts[t], 0)),
                  hbm, hbm, hbm],
        out_specs=pl.BlockSpec((tm, d), lambda t, *_: (t, 0)),
        scratch_shapes=[pltpu.VMEM((d, ff), BF16), pltpu.VMEM((d, ff), BF16), pltpu.VMEM((ff, d), BF16),
                        pltpu.VMEM((2, d, ff), F32), pltpu.VMEM((2, d, ff), F32), pltpu.VMEM((2, ff, d), F32),
                        pltpu.SemaphoreType.DMA((3, 2))],
    )
    return pl.pallas_call(
        _expert_kernel,
        grid_spec=grid_spec,
        out_shape=jax.ShapeDtypeStruct((rows, d), F32),
        compiler_params=_cparams("arbitrary"),
        name="experts",
    )(tile_src, n_active, tile_pos, expert_seq, n_used, x_sorted, w_sorted, w_gate, w_up, w_down)


def _take_rows(a, idx):
    return a.at[idx].get(mode="promise_in_bounds")


def _step_lookup(table, starts, x):
    jumps = table[1:] - table[:-1]
    return table[0] + jnp.sum(jnp.where(x[None, :] >= starts[1:, None], jumps[:, None], 0), axis=0)


def _dispatch_plan(eid, ew, n_experts_total):
    n, top_k = eid.shape
    tm = EXPERT_TILE
    a = n * top_k
    n_tiles = a // tm + n_experts_total
    rows = n_tiles * tm
    i32 = jnp.int32
    q = jnp.arange(a, dtype=i32)
    sorted_e, order, sorted_w = lax.sort((eid.reshape(a), q, ew.reshape(a)), num_keys=1, is_stable=True)
    bounds = jnp.searchsorted(sorted_e, jnp.arange(n_experts_total + 1, dtype=i32), side="left").astype(i32)
    off, end = bounds[:-1], bounds[1:]
    padded = (end - off + tm - 1) // tm * tm
    pad_end = jnp.cumsum(padded)
    pad_off = pad_end - padded
    delta = pad_off - off
    _, pos = lax.sort((order, q + _step_lookup(delta, off, q)), num_keys=1)
    n_active = (pad_end[-1] // tm).astype(i32)
    tile_src = jnp.minimum(jnp.arange(n_tiles, dtype=i32), n_active - 1)
    experts = jnp.arange(n_experts_total, dtype=i32)
    used = padded > 0
    n_used = jnp.sum(used, dtype=i32)
    expert_seq = jnp.minimum(jnp.sort(jnp.where(used, experts, n_experts_total)), n_experts_total - 1)
    tile_pos = _step_lookup(jnp.cumsum(used, dtype=i32) - 1, pad_off, tile_src * tm)
    p = jnp.arange(rows, dtype=i32)
    src = p - _step_lookup(delta, pad_off, p)
    valid = (src < _step_lookup(end, pad_off, p)) & (p < pad_end[-1])
    src = jnp.clip(src, 0, a - 1)
    row_token = jnp.where(valid, _take_rows(order, src) // top_k, p % n)
    row_w = jnp.where(valid, _take_rows(sorted_w, src), 0.0)
    return (row_token, row_w.reshape(rows, 1), pos.reshape(n, top_k), tile_src, n_active.reshape(1),
            tile_pos, expert_seq, n_used.reshape(1))


def _combine_kernel(n_prompt_tiles, x_ref, ya_ref, yb_ref, gr, gt, shr, sht, scr, sct, o_ref, hn_ref):
    is_s = pl.program_id(0) >= n_prompt_tiles
    x = x_ref[...] + _pick(is_s, gr, gt) * (ya_ref[...] + yb_ref[...])
    o_ref[...] = x
    hn_ref[...] = (_rms(x) * (1.0 + _pick(is_s, scr, sct)) + _pick(is_s, shr, sht)).astype(BF16)


def _combine_final_kernel(n_prompt_tiles, x_ref, ya_ref, yb_ref, gr, gt, fg_ref, op_ref, os_ref):
    is_s = pl.program_id(0) >= n_prompt_tiles
    x = x_ref[...] + _pick(is_s, gr, gt) * (ya_ref[...] + yb_ref[...])
    y = _rms(x) * fg_ref[...]

    @pl.when(jnp.logical_not(is_s))
    def _():
        op_ref[...] = y

    @pl.when(is_s)
    def _():
        os_ref[...] = y


def _combine(st, x, y_pairs, mod_row, mod_tok, layer, j_gate, final_gain, final):
    d = st.d
    tile = pl.BlockSpec((st.tm, d), lambda i: (i, 0))
    n_tiles = st.n_tiles
    common = [tile, tile, pl.BlockSpec((st.tm, d), lambda i: (n_tiles + i, 0)),
              *st.mod_specs(layer, d, lambda i: j_gate)]
    if final:
        return pl.pallas_call(
            functools.partial(_combine_final_kernel, st.n_prompt_tiles),
            grid=(st.n_tiles,),
            in_specs=[*common, pl.BlockSpec((1, d), lambda i: (0, 0))],
            out_specs=[pl.BlockSpec((st.tm, d), lambda i: (jnp.minimum(i, n_tiles - 2), 0)),
                       pl.BlockSpec((st.tm, d), lambda i: (0, 0))],
            out_shape=[jax.ShapeDtypeStruct((st.n_prompt, d), F32), jax.ShapeDtypeStruct((st.n_sample, d), F32)],
            compiler_params=_cparams("arbitrary"),
            name="moe_combine_final",
        )(x, y_pairs, y_pairs, mod_row, mod_tok, final_gain.reshape(1, d)), None
    return pl.pallas_call(
        functools.partial(_combine_kernel, st.n_prompt_tiles),
        grid=(st.n_tiles,),
        in_specs=[*common, *st.mod_specs(layer + 1, d, lambda i: 0), *st.mod_specs(layer + 1, d, lambda i: 1)],
        out_specs=[tile, tile],
        out_shape=[jax.ShapeDtypeStruct((st.n, d), F32), jax.ShapeDtypeStruct((st.n, d), BF16)],
        compiler_params=_cparams("parallel"),
        name="moe_combine",
    )(x, y_pairs, y_pairs, mod_row, mod_tok, mod_row, mod_tok, mod_row, mod_tok)


def kernel(x_prompt, x_sample, state_hgrn, state_ret, c_prompt, c_sample, ada_w, ada_b, hgrn_w_in, hgrn_lb_logits, hgrn_norm_gain, hgrn_w_out, ret_w_in, ret_norm_gain, ret_w_out, moe_router_group, moe_router_expert, moe_w_gate, moe_w_up, moe_w_down, final_norm_gain):
    batch, seq, d = x_prompt.shape
    dec_batch, dec_seq, _ = x_sample.shape
    depth = ada_w.shape[0]
    _, _, a_heads, a_kd, a_vd = state_hgrn.shape
    _, _, b_heads, b_dk, b_dv = state_ret.shape
    _, n_groups, _, n_exp = moe_router_expert.shape
    ff = moe_w_gate.shape[-1]
    n_exp_total = n_groups * n_exp
    st = _Stream(batch, seq, dec_batch, dec_seq, d)

    p = jax.nn.softmax(hgrn_lb_logits.astype(F32), axis=0)
    lb_all = jnp.cumsum(p, axis=0) - p[0:1]
    log_gamma = jnp.log1p(-jnp.exp2(-5.0 - jnp.arange(b_heads, dtype=F32)))
    cos_p, sin_p = _rope_tables(jnp.arange(seq, dtype=F32), b_dk // 2)
    cos_s, sin_s = _rope_tables(jnp.arange(dec_seq, dtype=F32) + PAST_LEN, b_dk // 2)
    w_gate = moe_w_gate.reshape(depth * n_exp_total, d, ff)
    w_up = moe_w_up.reshape(depth * n_exp_total, d, ff)
    w_down = moe_w_down.reshape(depth * n_exp_total, ff, d)

    c_rows = -(-batch // BF16_SUBLANES) * BF16_SUBLANES
    mod_row, mod_tok = _ada_mod(jnp.pad(c_prompt, ((0, c_rows - batch), (0, 0))),
                                jnp.repeat(c_sample, dec_seq, axis=0), ada_w, ada_b)
    mod_row = mod_row[:, :batch].reshape(depth * batch, 1, 6 * d)

    x = jnp.concatenate([x_prompt.reshape(-1, d), x_sample.reshape(-1, d)], 0)
    hg_p, rt_p, hg_s, rt_s = [], [], None, None
    hn = _norm_mod(st, x, mod_row, mod_tok, 0, 0, 1)
    for l in range(depth):
        j = l // 2
        if l % 2 == 0:
            proj = _proj(st, hn, hgrn_w_in, j)
            o_p, s_p, o_s, hg_s = _hgrn_mixer(st, proj, lb_all[j], hgrn_norm_gain[j], state_hgrn, j, hg_s)
            hg_p.append(s_p)
            w_out = hgrn_w_out
        else:
            proj = _proj(st, hn, ret_w_in, j)
            o_p, s_p, o_s, rt_s = _ret_mixer(st, proj, cos_p, sin_p, cos_s, sin_s, log_gamma,
                                             ret_norm_gain[j], state_ret, j, rt_s)
            rt_p.append(s_p)
            w_out = ret_w_out
        x = _proj_residual(st, o_p, o_s, w_out, j, x, mod_row, mod_tok, l, 2)

        w_router = jnp.concatenate(
            [moe_router_group[l], jnp.moveaxis(moe_router_expert[l], 0, 1).reshape(d, n_exp_total)], 1)
        w_router = jnp.pad(w_router, ((0, 0), (0, LANES - w_router.shape[1])))
        hx, eid, ew = _norm_mod_router(st, x, mod_row, mod_tok, l, 3, 4, w_router, n_groups, n_exp)
        row_token, row_w, pos, tile_src, n_active, tile_pos, expert_seq, n_used = _dispatch_plan(
            eid, ew, n_exp_total)
        y_sorted = _experts(_take_rows(hx, row_token), row_w, tile_src, n_active, tile_pos,
                            expert_seq + l * n_exp_total, n_used, w_gate, w_up, w_down)
        y_pairs = _take_rows(y_sorted, pos.T.reshape(-1))
        x, hn = _combine(st, x, y_pairs, mod_row, mod_tok, l, 5, final_norm_gain, l == depth - 1)

    y_prompt = x[0].reshape(batch, seq, d)
    y_sample = x[1].reshape(dec_batch, dec_seq, d)
    return (y_prompt, y_sample, jnp.stack(hg_p), jnp.stack(rt_p), hg_s, rt_s)
```

```python
import functools
import math

import numpy as np
import jax
import jax.numpy as jnp
from jax import lax
from jax.experimental import pallas as pl
from jax.experimental.pallas import tpu as pltpu

F32 = jnp.float32
BF16 = jnp.bfloat16
EPS = 1e-6
PAST_LEN = 16384
ROPE_BASE = 10000.0

V7X_VMEM_LIMIT_BYTES = 56 * 1024 * 1024
LANES = 128
BF16_SUBLANES = 16

HGRN_CHUNK = 128
HGRN_HEADS_PER_STEP = 8
RET_CHUNK = 256
RET_HEADS_PER_STEP = 2
EXPERT_TILE = 256
PROJ_TN = 1024
PROJ_IN_ROW_TILES = 8
ADA_TN = 1024


def _cparams(*sem):
    return pltpu.CompilerParams(dimension_semantics=sem, vmem_limit_bytes=V7X_VMEM_LIMIT_BYTES)


def _dot(a, b):
    return jnp.dot(a, b, preferred_element_type=F32)


def _dot_nt(a, b):
    return lax.dot_general(a, b, (((1,), (1,)), ((), ())), preferred_element_type=F32)


def _dot2(m, x):
    hi = x.astype(BF16)
    lo = (x - hi.astype(F32)).astype(BF16)
    return _dot(m, hi) + _dot(m, lo)


def _dot2_wide(mm, x):
    hi = x.astype(BF16)
    lo = (x - hi.astype(F32)).astype(BF16)
    return _dot(mm, jnp.concatenate([hi, lo], axis=0))


def _dot2_lhs(x, m):
    hi = x.astype(BF16)
    lo = (x - hi.astype(F32)).astype(BF16)
    return _dot(hi, m) + _dot(lo, m)


def _silu(x):
    return x * jax.nn.sigmoid(x)


def _rms(x):
    return x * lax.rsqrt(jnp.mean(x * x, axis=-1, keepdims=True) + EPS)


def _ada_kernel(cr_ref, ct_ref, w_ref, b_ref, or_ref, ot_ref):
    w = w_ref[0].astype(BF16)
    or_ref[0] = _dot(_silu(cr_ref[...]).astype(BF16), w) + b_ref[0]
    ot_ref[0] = _dot(_silu(ct_ref[...]).astype(BF16), w) + b_ref[0]


def _ada_mod(c_row, c_tok, ada_w, ada_b):
    depth, d, n6 = ada_w.shape
    r, t = c_row.shape[0], c_tok.shape[0]
    tn = min(ADA_TN, n6)
    return pl.pallas_call(
        _ada_kernel,
        grid=(depth, n6 // tn),
        in_specs=[pl.BlockSpec((r, d), lambda l, n: (0, 0)),
                  pl.BlockSpec((t, d), lambda l, n: (0, 0)),
                  pl.BlockSpec((1, d, tn), lambda l, n: (l, 0, n)),
                  pl.BlockSpec((1, 1, tn), lambda l, n: (l, 0, n))],
        out_specs=[pl.BlockSpec((1, r, tn), lambda l, n: (l, 0, n)),
                   pl.BlockSpec((1, t, tn), lambda l, n: (l, 0, n))],
        out_shape=[jax.ShapeDtypeStruct((depth, r, n6), F32), jax.ShapeDtypeStruct((depth, t, n6), F32)],
        compiler_params=_cparams("parallel", "parallel"),
        name="ada_mod",
    )(c_row, c_tok, ada_w, ada_b.reshape(depth, 1, n6))


class _Stream:
    def __init__(self, batch, seq, dec_batch, dec_seq, d):
        self.batch, self.seq, self.dec_batch, self.dec_seq, self.d = batch, seq, dec_batch, dec_seq, d
        self.n_prompt = batch * seq
        self.n_sample = dec_batch * dec_seq
        self.tm = self.n_sample
        assert seq % self.tm == 0 and self.tm % BF16_SUBLANES == 0
        self.tiles_per_seq = seq // self.tm
        self.n_prompt_tiles = self.n_prompt // self.tm
        self.n_tiles = self.n_prompt_tiles + 1
        self.n = self.n_prompt + self.n_sample

    def mod_specs(self, layer, width, col_block, tile_axis=0):
        tps, last, batch = self.tiles_per_seq, self.batch - 1, self.batch

        def row_map(*idx):
            return (layer * batch + jnp.minimum(idx[tile_axis] // tps, last), 0, col_block(*idx))
        return (pl.BlockSpec((1, 1, width), row_map),
                pl.BlockSpec((1, self.tm, width), lambda *idx: (layer, 0, col_block(*idx))))


def _pick(is_sample, row_ref, tok_ref):
    return jnp.where(is_sample, tok_ref[0], row_ref[0])


def _norm_mod_kernel(n_prompt_tiles, x_ref, shr, sht, scr, sct, o_ref):
    is_s = pl.program_id(0) >= n_prompt_tiles
    h = _rms(x_ref[...]) * (1.0 + _pick(is_s, scr, sct)) + _pick(is_s, shr, sht)
    o_ref[...] = h.astype(BF16)


def _norm_mod(st, x, mod_row, mod_tok, layer, j_shift, j_scale):
    d = st.d
    return pl.pallas_call(
        functools.partial(_norm_mod_kernel, st.n_prompt_tiles),
        grid=(st.n_tiles,),
        in_specs=[pl.BlockSpec((st.tm, d), lambda i: (i, 0)),
                  *st.mod_specs(layer, d, lambda i: j_shift), *st.mod_specs(layer, d, lambda i: j_scale)],
        out_specs=pl.BlockSpec((st.tm, d), lambda i: (i, 0)),
        out_shape=jax.ShapeDtypeStruct((st.n, d), BF16),
        compiler_params=_cparams("parallel"),
        name="norm_mod",
    )(x, mod_row, mod_tok, mod_row, mod_tok)


def _norm_mod_router_kernel(n_prompt_tiles, n_groups, n_experts, x_ref, shr, sht, scr, sct, wr_ref,
                            o_ref, eid_ref, ew_ref):
    is_s = pl.program_id(0) >= n_prompt_tiles
    h = _rms(x_ref[...]) * (1.0 + _pick(is_s, scr, sct)) + _pick(is_s, shr, sht)
    o_ref[...] = h

    logits = _dot(h.astype(BF16), wr_ref[...].astype(BF16))
    tm = logits.shape[0]
    lane = lax.broadcasted_iota(jnp.int32, (tm, LANES), 1)
    neg_inf = jnp.float32(-jnp.inf)

    def first_index_of_max(vals, vmax):
        return jnp.min(jnp.where(vals == vmax, lane, LANES), axis=1, keepdims=True)

    is_g = lane < n_groups
    lg = jnp.where(is_g, logits, neg_inf)
    mg = jnp.max(lg, axis=1, keepdims=True)
    eg = jnp.exp(lg - mg)
    pg = eg / jnp.sum(eg, axis=1, keepdims=True)
    gi = first_index_of_max(lg, mg)
    wg = jnp.sum(jnp.where(lane == gi, pg, 0.0), axis=1, keepdims=True)

    lo = n_groups + gi * n_experts
    sel = (lane >= lo) & (lane < lo + n_experts)
    le = jnp.where(sel, logits, neg_inf)
    me = jnp.max(le, axis=1, keepdims=True)
    ee = jnp.exp(le - me)
    pe = jnp.where(sel, ee / jnp.sum(ee, axis=1, keepdims=True), -1.0)
    p1 = jnp.max(pe, axis=1, keepdims=True)
    i1 = first_index_of_max(pe, p1)
    pe2 = jnp.where(lane == i1, -1.0, pe)
    p2 = jnp.max(pe2, axis=1, keepdims=True)
    i2 = first_index_of_max(pe2, p2)
    denom = p1 + p2
    col = lax.broadcasted_iota(jnp.int32, (tm, 2), 1)
    eid_ref[...] = jnp.where(col == 0, i1, i2) - n_groups
    ew_ref[...] = wg * jnp.where(col == 0, p1 / denom, p2 / denom)


def _norm_mod_router(st, x, mod_row, mod_tok, layer, j_shift, j_scale, w_router, n_groups, n_experts):
    d = st.d
    return pl.pallas_call(
        functools.partial(_norm_mod_router_kernel, st.n_prompt_tiles, n_groups, n_experts),
        grid=(st.n_tiles,),
        in_specs=[pl.BlockSpec((st.tm, d), lambda i: (i, 0)),
                  *st.mod_specs(layer, d, lambda i: j_shift), *st.mod_specs(layer, d, lambda i: j_scale),
                  pl.BlockSpec((d, LANES), lambda i: (0, 0))],
        out_specs=[pl.BlockSpec((st.tm, d), lambda i: (i, 0)),
                   pl.BlockSpec((st.tm, 2), lambda i: (i, 0)),
                   pl.BlockSpec((st.tm, 2), lambda i: (i, 0))],
        out_shape=[jax.ShapeDtypeStruct((st.n, d), F32),
                   jax.ShapeDtypeStruct((st.n, 2), jnp.int32),
                   jax.ShapeDtypeStruct((st.n, 2), F32)],
        compiler_params=_cparams("parallel"),
        name="norm_mod_router",
    )(x, mod_row, mod_tok, mod_row, mod_tok, w_router)


def _proj_kernel(a_ref, w_ref, o_ref, wb_ref):
    @pl.when(pl.program_id(1) == 0)
    def _():
        wb_ref[...] = w_ref[0].astype(BF16)
    o_ref[...] = _dot(a_ref[...], wb_ref[...])


def _proj(st, a, w_all, j):
    _, k, n_out = w_all.shape
    tn = min(PROJ_TN, n_out)
    tm = st.n // PROJ_IN_ROW_TILES if st.n % (PROJ_IN_ROW_TILES * BF16_SUBLANES) == 0 else st.tm
    return pl.pallas_call(
        _proj_kernel,
        grid=(n_out // tn, st.n // tm),
        in_specs=[pl.BlockSpec((tm, k), lambda n, m: (m, 0)),
                  pl.BlockSpec((1, k, tn), lambda n, m: (j, 0, n))],
        out_specs=pl.BlockSpec((tm, tn), lambda n, m: (m, n)),
        out_shape=jax.ShapeDtypeStruct((st.n, n_out), F32),
        scratch_shapes=[pltpu.VMEM((k, tn), BF16)],
        compiler_params=_cparams("parallel", "arbitrary"),
        name="proj_in",
    )(a, w_all)


def _proj_res_kernel(n_prompt_tiles, ap_ref, as_ref, w_ref, x_ref, gr, gt, o_ref, wb_ref):
    @pl.when(pl.program_id(1) == 0)
    def _():
        wb_ref[...] = w_ref[0].astype(BF16)
    is_s = pl.program_id(1) >= n_prompt_tiles
    a = jnp.where(is_s, as_ref[...], ap_ref[...])
    o_ref[...] = x_ref[...] + _pick(is_s, gr, gt) * _dot(a, wb_ref[...])


def _proj_residual(st, a_prompt, a_sample, w_all, j, x, mod_row, mod_tok, layer, j_gate):
    _, k, d = w_all.shape
    tn = min(PROJ_TN, d)
    nb = d // tn
    last_prompt = st.n_prompt_tiles - 1
    return pl.pallas_call(
        functools.partial(_proj_res_kernel, st.n_prompt_tiles),
        grid=(nb, st.n_tiles),
        in_specs=[pl.BlockSpec((st.tm, k), lambda n, m: (jnp.minimum(m, last_prompt), 0)),
                  pl.BlockSpec((st.tm, k), lambda n, m: (0, 0)),
                  pl.BlockSpec((1, k, tn), lambda n, m: (j, 0, n), pipeline_mode=pl.Buffered(1)),
                  pl.BlockSpec((st.tm, tn), lambda n, m: (m, n)),
                  *st.mod_specs(layer, tn, lambda n, m: j_gate * nb + n, tile_axis=1)],
        out_specs=pl.BlockSpec((st.tm, tn), lambda n, m: (m, n)),
        out_shape=jax.ShapeDtypeStruct((st.n, d), F32),
        scratch_shapes=[pltpu.VMEM((k, tn), BF16)],
        compiler_params=_cparams("parallel", "arbitrary"),
        name="proj_out",
    )(a_prompt, a_sample, w_all, x, mod_row, mod_tok)


def _hgrn_tables(c):
    nbits = int(math.log2(c))
    assert 1 << nbits == c
    r = np.arange(c)[:, None]
    j = np.arange(c)[None, :]
    mats = [(j <= r), (j > r)]
    for b in range(nbits):
        mid = ((r >> (b + 1)) << (b + 1)) + (1 << b)
        right = r >= mid
        mats.append(np.where(right, (j >= mid) & (j <= r), (j > r) & (j < mid)))
    level = np.full((c, c), -1, np.int32)
    level[r[:, 0], r[:, 0]] = 0
    diff = r ^ j
    for b in range(nbits):
        level[(j < r) & ((diff >> b) == 1)] = b + 1
    return np.concatenate(mats, 0).astype(np.float32), level, nbits


def _hgrn_kernel(c, nbits, hb, kd, q_ref, z_ref, i_ref, g_ref, lb_ref, gain_ref, ms_ref, lv_ref,
                 o_ref, so_ref, s_scr):
    ci = pl.program_id(2)

    @pl.when(ci == 0)
    def _():
        s_scr[...] = jnp.zeros_like(s_scr)

    lb = lb_ref[...]
    f = lb + (1.0 - lb) * jax.nn.sigmoid(z_ref[...])
    logf = jnp.log(f)
    k = 1.0 - f
    q = _silu(q_ref[...])
    v = i_ref[...].astype(BF16)
    e = _dot2_wide(ms_ref[...], logf)
    cum = e[0:c]
    qd = (q * jnp.exp(cum)).astype(BF16)
    kdec = k * jnp.exp(e[c:2 * c])
    whole_row = cum[c - 1:c]
    lv = lv_ref[...]
    masks = [lv == b for b in range(nbits + 1)]
    qb, kb = q.astype(BF16), k.astype(BF16)
    levels = [(qb, kb)]
    for b in range(nbits):
        gl = jnp.exp(e[(2 + b) * c:(3 + b) * c]).astype(BF16)
        levels.append((qb * gl, kb * gl))
    gate = gain_ref[...] * _silu(g_ref[...])

    for h in range(hb):
        sl = slice(h * kd, (h + 1) * kd)
        s = s_scr[h]
        o = _dot(qd[:, sl], s.astype(BF16))
        att = jnp.zeros((c, c), F32)
        for mask, (ql, kl) in zip(masks, levels):
            att = jnp.where(mask, _dot_nt(ql[:, sl], kl[:, sl]), att)
        o = o + _dot(att.astype(BF16), v[:, sl])
        whole = jnp.exp(jnp.broadcast_to(whole_row[:, sl], (s.shape[1], s.shape[0])).T)
        s_scr[h] = whole * s + _dot(kdec[:, sl].T.astype(BF16), v[:, sl])
        o = o * lax.rsqrt(jnp.mean(o * o, axis=-1, keepdims=True) + EPS)
        o_ref[:, sl] = (o * gate[:, sl]).astype(BF16)

    @pl.when(ci == pl.num_programs(2) - 1)
    def _():
        so_ref[0] = s_scr[...]


def _step_tables(bb, t):
    rows = bb * t
    r = np.arange(rows)
    grp, pos = r // t, r % t
    same = grp[:, None] == grp[None, :]
    i, j = r[:, None], r[None, :]
    sums = [same & (j <= i), same & (j > i), same]
    for d in range(1, t):
        sums.append(same & (j <= i) & (j > i - d) & (pos[:, None] >= d))
    shifts = [same & (j == i - d) for d in range(1, t)]
    return (np.concatenate(sums, 0).astype(np.float32), np.concatenate(shifts, 0).astype(np.float32))


def _hgrn_step_kernel(bb, t, heads, kd, q_ref, z_ref, i_ref, g_ref, lb_ref, gain_ref, ms_ref, sh_ref,
                      seg_ref, segt_ref, s_ref, o_ref, so_ref):
    rows = bb * t
    lb = lb_ref[...]
    f = lb + (1.0 - lb) * jax.nn.sigmoid(z_ref[...])
    logf = jnp.log(f)
    k = 1.0 - f
    q = _silu(q_ref[...])
    v = i_ref[...]
    e = _dot2(ms_ref[...], logf)
    qd = (q * jnp.exp(e[0:rows])).astype(BF16)
    kdec = k * jnp.exp(e[rows:2 * rows])
    whole = jnp.exp(e[2 * rows:3 * rows])

    seg, segt = seg_ref[...], segt_ref[...]
    o = jnp.zeros_like(q)
    for d in range(t):
        if d == 0:
            ks, vs, w = k, v, q * k
        else:
            sh = sh_ref[(d - 1) * rows:d * rows]
            ks, vs = _dot2(sh, k), _dot2(sh, v)
            w = q * ks * jnp.exp(e[(2 + d) * rows:(3 + d) * rows])
        att = _dot2_lhs(w, seg)
        o = o + _dot2_lhs(att, segt) * vs

    gate = gain_ref[...] * _silu(g_ref[...])
    vb = v.astype(BF16)
    row_seq = lax.broadcasted_iota(jnp.int32, (rows, kd), 0) // t
    lane_seq = lax.broadcasted_iota(jnp.int32, (kd, rows), 1) // t
    for h in range(heads):
        sl = slice(h * kd, (h + 1) * kd)
        k_t = kdec[:, sl].T
        w_t = whole[:, sl].T
        o_h = o[:, sl]
        for b in range(bb):
            s = s_ref[0, b, h]
            o_h = o_h + jnp.where(row_seq == b, _dot(qd[:, sl], s.astype(BF16)), 0.0)
            k_b = jnp.where(lane_seq == b, k_t, 0.0).astype(BF16)
            so_ref[0, b, h] = w_t[:, b * t:b * t + 1] * s + _dot(k_b, vb[:, sl])
        o_h = o_h * lax.rsqrt(jnp.mean(o_h * o_h, axis=-1, keepdims=True) + EPS)
        o_ref[:, sl] = (o_h * gate[:, sl]).astype(BF16)


def _step_blocking(st, scan_steps):
    t = st.dec_seq
    bb = BF16_SUBLANES // math.gcd(BF16_SUBLANES, t)
    n_blocks = st.dec_batch // bb
    if n_blocks > scan_steps:
        assert n_blocks % scan_steps == 0
        bb, n_blocks = bb * (n_blocks // scan_steps), scan_steps
    assert st.dec_batch == bb * n_blocks and scan_steps % n_blocks == 0 and st.n_prompt % (bb * t) == 0
    return bb, scan_steps // n_blocks


def _hgrn_step_call_kernel(bb, t, heads, kd, n_in, *refs):
    _hgrn_step_kernel(bb, t, heads, kd, *refs[:11], *refs[n_in:])


def _hgrn_mixer(st, proj, lb, gain, state_all, j, so_prev):
    _, _, heads, kd, vd = state_all.shape
    width = heads * kd
    c = min(HGRN_CHUNK, st.seq)
    hb = min(HGRN_HEADS_PER_STEP, heads)
    n_chunks = st.seq // c
    groups = heads // hb
    mstack, level, nbits = _hgrn_tables(c)
    lb2, gain2 = lb.reshape(1, width), gain.reshape(1, width)

    def pseg(jj):
        return pl.BlockSpec((c, hb * kd), lambda b, h, ci: (b * n_chunks + ci, jj * groups + h))

    head_row = pl.BlockSpec((1, hb * kd), lambda b, h, ci: (0, h))
    o_p, s_p = pl.pallas_call(
        functools.partial(_hgrn_kernel, c, nbits, hb, kd),
        grid=(st.batch, groups, n_chunks),
        in_specs=[pseg(0), pseg(1), pseg(2), pseg(3), head_row, head_row,
                  pl.BlockSpec((mstack.shape[0], 2 * c), lambda b, h, ci: (0, 0)),
                  pl.BlockSpec(level.shape, lambda b, h, ci: (0, 0))],
        out_specs=[pl.BlockSpec((c, hb * vd), lambda b, h, ci: (b * n_chunks + ci, h)),
                   pl.BlockSpec((1, hb, kd, vd), lambda b, h, ci: (b, h, 0, 0))],
        out_shape=[jax.ShapeDtypeStruct((st.n_prompt, heads * vd), BF16),
                   jax.ShapeDtypeStruct((st.batch, heads, kd, vd), F32)],
        scratch_shapes=[pltpu.VMEM((hb, kd, vd), F32)],
        compiler_params=_cparams("parallel", "parallel", "arbitrary"),
        name="hgrn_scan",
    )(proj, proj, proj, proj, lb2, gain2,
      jnp.asarray(np.concatenate([mstack, mstack], 1), BF16), jnp.asarray(level))

    t = st.dec_seq
    bb, _ = _step_blocking(st, st.dec_batch)
    rows = bb * t
    rb0 = st.n_prompt // rows
    sums, shifts = _step_tables(bb, t)
    seg = (np.arange(width)[:, None] // kd == np.arange(LANES)[None, :]).astype(np.float32)
    assert heads <= LANES

    def sseg(jj):
        return pl.BlockSpec((rows, width), lambda i: (rb0 + i, jj))

    full = lambda a: pl.BlockSpec(a.shape, lambda i: (0,) * a.ndim)
    state = pl.BlockSpec((1, bb, heads, kd, vd), lambda i: (j, i, 0, 0, 0))
    args = [proj, proj, proj, proj, lb2, gain2, jnp.asarray(sums, BF16), jnp.asarray(shifts, BF16),
            jnp.asarray(seg, BF16), jnp.asarray(seg.T, BF16), state_all]
    in_specs = [sseg(0), sseg(1), sseg(2), sseg(3), full(lb2), full(gain2), full(sums), full(shifts),
                full(seg), full(seg.T), state]
    aliases = {}
    if so_prev is not None:
        args.append(so_prev)
        in_specs.append(pl.BlockSpec(memory_space=pl.ANY))
        aliases[len(args) - 1] = 1
    o_s, so = pl.pallas_call(
        functools.partial(_hgrn_step_call_kernel, bb, t, heads, kd, len(args)),
        grid=(st.dec_batch // bb,),
        in_specs=in_specs,
        out_specs=[pl.BlockSpec((rows, width), lambda i: (i, 0)), state],
        out_shape=[jax.ShapeDtypeStruct((st.n_sample, heads * vd), BF16),
                   jax.ShapeDtypeStruct(state_all.shape, F32)],
        input_output_aliases=aliases,
        compiler_params=_cparams("parallel"),
        name="hgrn_step",
    )(*args)
    return o_p, s_p, o_s, so


def _rotate(x, cos, sin):
    half = x.shape[-1] // 2
    x1, x2 = x[:, :half], x[:, half:]
    return jnp.concatenate([x1 * cos - x2 * sin, x1 * sin + x2 * cos], axis=-1)


def _group_norm_gate(o, gain, g):
    mu = jnp.mean(o, axis=-1, keepdims=True)
    var = jnp.mean(jnp.square(o - mu), axis=-1, keepdims=True)
    return (_silu(g) * ((o - mu) * lax.rsqrt(var + EPS) * gain)).astype(BF16)


def _ret_kernel(c, dk, dv, lgs, q_ref, k_ref, v_ref, g_ref, cos_ref, sin_ref, gain_ref, o_ref, so_ref,
                s_scr, intra_scr):
    ci = pl.program_id(2)
    t_col = lax.broadcasted_iota(jnp.int32, (c, 1), 0).astype(F32)

    @pl.when(ci == 0)
    def _():
        s_scr[...] = jnp.zeros_like(s_scr)
        rel = (lax.broadcasted_iota(jnp.int32, (c, c), 0)
               - lax.broadcasted_iota(jnp.int32, (c, c), 1)).astype(F32)
        for h, lg in enumerate(lgs):
            intra_scr[h] = jnp.where(rel >= 0.0, jnp.exp(jnp.maximum(rel, 0.0) * lg), 0.0)

    cos, sin = cos_ref[...], sin_ref[...]
    for h, lg in enumerate(lgs):
        q = _rotate(q_ref[:, h * dk:(h + 1) * dk], cos, sin)
        k = _rotate(k_ref[:, h * dk:(h + 1) * dk], cos, sin) * (dk ** -0.5)
        v = v_ref[:, h * dv:(h + 1) * dv].astype(BF16)
        inter = jnp.exp((t_col + 1.0) * lg)
        tail = jnp.exp((c - 1.0 - t_col) * lg)
        whole = jnp.exp(jnp.full((1, 1), c, F32) * lg)
        s = s_scr[h]
        att = _dot_nt(q.astype(BF16), k.astype(BF16)) * intra_scr[h]
        o = _dot(att.astype(BF16), v) + _dot((q * inter).astype(BF16), s.astype(BF16))
        s_scr[h] = whole * s + _dot((k * tail).T.astype(BF16), v)
        o_ref[:, h * dv:(h + 1) * dv] = _group_norm_gate(
            o, gain_ref[:, h * dv:(h + 1) * dv], g_ref[:, h * dv:(h + 1) * dv])

    @pl.when(ci == pl.num_programs(2) - 1)
    def _():
        so_ref[0] = s_scr[...]


def _ret_step_kernel(bb, t, dk, dv, lgs, q_ref, k_ref, v_ref, g_ref, cos_ref, sin_ref, pos_ref, rel_ref,
                     gain_ref, s_ref, o_ref, so_ref):
    rows = bb * t
    cos, sin = cos_ref[...], sin_ref[...]
    t_col = pos_ref[...][:, :1]
    rel = rel_ref[...]
    row_seq = lax.broadcasted_iota(jnp.int32, (rows, dv), 0) // t
    lane_seq = lax.broadcasted_iota(jnp.int32, (dk, rows), 1) // t
    for h, lg in enumerate(lgs):
        q = _rotate(q_ref[:, h * dk:(h + 1) * dk], cos, sin)
        k = _rotate(k_ref[:, h * dk:(h + 1) * dk], cos, sin) * (dk ** -0.5)
        v = v_ref[:, h * dv:(h + 1) * dv].astype(BF16)
        inter = jnp.exp((t_col + 1.0) * lg)
        tail = jnp.exp((t - 1.0 - t_col) * lg)
        intra = jnp.where(rel >= 0.0, jnp.exp(jnp.maximum(rel, 0.0) * lg), 0.0)
        whole = jnp.exp(jnp.full((1, 1), t, F32) * lg)

        att = _dot_nt(q.astype(BF16), k.astype(BF16)) * intra
        o = _dot(att.astype(BF16), v)
        qd = (q * inter).astype(BF16)
        k_t = (k * tail).T
        for b in range(bb):
            s = s_ref[0, b, h]
            o = o + jnp.where(row_seq == b, _dot(qd, s.astype(BF16)), 0.0)
            k_b = jnp.where(lane_seq == b, k_t, 0.0).astype(BF16)
            so_ref[0, b, h] = whole * s + _dot(k_b, v)
        o_ref[:, h * dv:(h + 1) * dv] = _group_norm_gate(
            o, gain_ref[:, h * dv:(h + 1) * dv], g_ref[:, h * dv:(h + 1) * dv])


def _ret_mixer_kernel(c, hb, dk, dv, bb, t, stride, n_in, *refs):
    (lg_ref, qp, kp, vp, gp, cosp, sinp, gain_ref,
     qs, ks, vs, gs, coss, sins, pos_ref, rel_ref, s_ref) = refs[:17]
    o_p, sp_ref, o_s, so_ref, s_scr, intra_scr = refs[n_in:]
    lgs = [lg_ref[pl.program_id(1) * hb + h] for h in range(hb)]
    _ret_kernel(c, dk, dv, lgs, qp, kp, vp, gp, cosp, sinp, gain_ref, o_p, sp_ref, s_scr, intra_scr)

    @pl.when(pl.program_id(2) % stride == 0)
    def _():
        _ret_step_kernel(bb, t, dk, dv, lgs, qs, ks, vs, gs, coss, sins, pos_ref, rel_ref, gain_ref,
                         s_ref, o_s, so_ref)


def _ret_mixer(st, proj, cos_p, sin_p, cos_s, sin_s, log_gamma, gain, state_all, j, so_prev):
    _, _, heads, dk, dv = state_all.shape
    c = min(RET_CHUNK, st.seq)
    hb = min(RET_HEADS_PER_STEP, heads)
    groups = heads // hb
    n_chunks = st.seq // c
    v_off = 2 * heads * dk // (hb * dv)
    g_off = v_off + groups
    t = st.dec_seq
    bb, stride = _step_blocking(st, st.batch * n_chunks)
    rows = bb * t
    rb0 = st.n_prompt // rows
    r = np.arange(rows)
    same = (r[:, None] // t) == (r[None, :] // t)
    rel = np.where(same & (r[None, :] <= r[:, None]), r[:, None] - r[None, :], -1).astype(np.float32)
    pos = np.broadcast_to((r % t).astype(np.float32)[:, None], (rows, LANES))
    cos_r, sin_r = jnp.tile(cos_s, (bb, 1)), jnp.tile(sin_s, (bb, 1))

    def sblock(b, ci):
        return (b * n_chunks + ci) // stride

    def pcol(width, off):
        return pl.BlockSpec((c, hb * width), lambda b, h, ci: (b * n_chunks + ci, off + h))

    def scol(width, off):
        return pl.BlockSpec((rows, hb * width), lambda b, h, ci: (rb0 + sblock(b, ci), off + h))

    full = lambda a: pl.BlockSpec(a.shape, lambda b, h, ci: (0,) * a.ndim)
    rope = pl.BlockSpec((c, dk // 2), lambda b, h, ci: (ci, 0))
    state = pl.BlockSpec((1, bb, hb, dk, dv), lambda b, h, ci: (j, sblock(b, ci), h, 0, 0))
    args = [log_gamma, proj, proj, proj, proj, cos_p, sin_p, gain.reshape(1, heads * dv),
            proj, proj, proj, proj, cos_r, sin_r, jnp.asarray(pos), jnp.asarray(rel), state_all]
    in_specs = [pl.BlockSpec(memory_space=pltpu.SMEM),
                pcol(dk, 0), pcol(dk, groups), pcol(dv, v_off), pcol(dv, g_off), rope, rope,
                pl.BlockSpec((1, hb * dv), lambda b, h, ci: (0, h)),
                scol(dk, 0), scol(dk, groups), scol(dv, v_off), scol(dv, g_off),
                full(cos_r), full(sin_r), full(pos), full(rel), state]
    aliases = {}
    if so_prev is not None:
        args.append(so_prev)
        in_specs.append(pl.BlockSpec(memory_space=pl.ANY))
        aliases[len(args) - 1] = 3
    return pl.pallas_call(
        functools.partial(_ret_mixer_kernel, c, hb, dk, dv, bb, t, stride, len(args)),
        grid=(st.batch, groups, n_chunks),
        in_specs=in_specs,
        out_specs=[pl.BlockSpec((c, hb * dv), lambda b, h, ci: (b * n_chunks + ci, h)),
                   pl.BlockSpec((1, hb, dk, dv), lambda b, h, ci: (b, h, 0, 0)),
                   pl.BlockSpec((rows, hb * dv), lambda b, h, ci: (sblock(b, ci), h)),
                   state],
        out_shape=[jax.ShapeDtypeStruct((st.n_prompt, heads * dv), BF16),
                   jax.ShapeDtypeStruct((st.batch, heads, dk, dv), F32),
                   jax.ShapeDtypeStruct((st.n_sample, heads * dv), BF16),
                   jax.ShapeDtypeStruct(state_all.shape, F32)],
        scratch_shapes=[pltpu.VMEM((hb, dk, dv), F32), pltpu.VMEM((hb, c, c), F32)],
        input_output_aliases=aliases,
        compiler_params=_cparams("arbitrary", "arbitrary", "arbitrary"),
        name="ret_mixer",
    )(*args)


def _rope_tables(pos, half):
    inv = 1.0 / (ROPE_BASE ** jnp.linspace(0.0, 1.0, half, dtype=F32))
    ang = pos[:, None] * inv[None]
    return jnp.cos(ang), jnp.sin(ang)


def _expert_kernel(ts_ref, na_ref, tk_ref, eseq_ref, nk_ref, x_ref, w_ref, wg_hbm, wu_hbm, wd_hbm, o_ref,
                   wgb, wub, wdb, stage_g, stage_u, stage_d, sems):
    t = pl.program_id(0)
    active = t < na_ref[0]
    k = tk_ref[t]
    new_expert = (t == 0) | (k != tk_ref[jnp.maximum(t - 1, 0)])

    def weight_copies(kk, slot):
        e = eseq_ref[kk]
        return (pltpu.make_async_copy(wg_hbm.at[e], stage_g.at[slot], sems.at[0, slot]),
                pltpu.make_async_copy(wu_hbm.at[e], stage_u.at[slot], sems.at[1, slot]),
                pltpu.make_async_copy(wd_hbm.at[e], stage_d.at[slot], sems.at[2, slot]))

    @pl.when(t == 0)
    def _():
        for cp in weight_copies(0, 0):
            cp.start()

    @pl.when(active & new_expert)
    def _():
        slot = lax.rem(k, 2)

        @pl.when(k + 1 < nk_ref[0])
        def _():
            for cp in weight_copies(k + 1, 1 - slot):
                cp.start()

        for cp in weight_copies(k, slot):
            cp.wait()
        wgb[...] = stage_g[slot].astype(BF16)
        wub[...] = stage_u[slot].astype(BF16)
        wdb[...] = stage_d[slot].astype(BF16)

    @pl.when(active)
    def _():
        x = x_ref[...].astype(BF16)
        hg = _silu(_dot(x, wgb[...])) * _dot(x, wub[...])
        o_ref[...] = _dot((hg * w_ref[...]).astype(BF16), wdb[...])

    @pl.when(jnp.logical_not(active))
    def _():
        o_ref[...] = jnp.zeros_like(o_ref)


def _experts(x_sorted, w_sorted, tile_src, n_active, tile_pos, expert_seq, n_used, w_gate, w_up, w_down):
    rows = x_sorted.shape[0]
    _, d, ff = w_gate.shape
    tm = EXPERT_TILE
    hbm = pl.BlockSpec(memory_space=pl.ANY)
    grid_spec = pltpu.PrefetchScalarGridSpec(
        num_scalar_prefetch=5,
        grid=(rows // tm,),
        in_specs=[pl.BlockSpec((tm, d), lambda t, ts, *_: (ts[t], 0)),
                  pl.BlockSpec((tm, 1), lambda t, ts, *_: (ts[t], 0)),
                  hbm, hbm, hbm],
        out_specs=pl.BlockSpec((tm, d), lambda t, *_: (t, 0)),
        scratch_shapes=[pltpu.VMEM((d, ff), BF16), pltpu.VMEM((d, ff), BF16), pltpu.VMEM((ff, d), BF16),
                        pltpu.VMEM((2, d, ff), F32), pltpu.VMEM((2, d, ff), F32), pltpu.VMEM((2, ff, d), F32),
                        pltpu.SemaphoreType.DMA((3, 2))],
    )
    return pl.pallas_call(
        _expert_kernel,
        grid_spec=grid_spec,
        out_shape=jax.ShapeDtypeStruct((rows, d), F32),
        compiler_params=_cparams("arbitrary"),
        name="experts",
    )(tile_src, n_active, tile_pos, expert_seq, n_used, x_sorted, w_sorted, w_gate, w_up, w_down)


def _take_rows(a, idx):
    return a.at[idx].get(mode="promise_in_bounds")


def _step_lookup(table, starts, x):
    jumps = table[1:] - table[:-1]
    return table[0] + jnp.sum(jnp.where(x[None, :] >= starts[1:, None], jumps[:, None], 0), axis=0)


def _dispatch_plan(eid, ew, n_experts_total):
    n, top_k = eid.shape
    tm = EXPERT_TILE
    a = n * top_k
    n_tiles = a // tm + n_experts_total
    rows = n_tiles * tm
    i32 = jnp.int32
    q = jnp.arange(a, dtype=i32)
    sorted_e, order, sorted_w = lax.sort((eid.reshape(a), q, ew.reshape(a)), num_keys=1, is_stable=True)
    bounds = jnp.sum(sorted_e[None, :] < jnp.arange(n_experts_total + 1, dtype=i32)[:, None], axis=1, dtype=i32)
    off, end = bounds[:-1], bounds[1:]
    padded = (end - off + tm - 1) // tm * tm
    pad_end = jnp.cumsum(padded)
    pad_off = pad_end - padded
    delta = pad_off - off
    _, pos = lax.sort((order, q + _step_lookup(delta, off, q)), num_keys=1)
    n_active = (pad_end[-1] // tm).astype(i32)
    tile_src = jnp.minimum(jnp.arange(n_tiles, dtype=i32), n_active - 1)
    experts = jnp.arange(n_experts_total, dtype=i32)
    used = padded > 0
    n_used = jnp.sum(used, dtype=i32)
    expert_seq = jnp.minimum(jnp.sort(jnp.where(used, experts, n_experts_total)), n_experts_total - 1)
    tile_pos = _step_lookup(jnp.cumsum(used, dtype=i32) - 1, pad_off, tile_src * tm)
    p = jnp.arange(rows, dtype=i32)
    src = p - _step_lookup(delta, pad_off, p)
    valid = (src < _step_lookup(end, pad_off, p)) & (p < pad_end[-1])
    src = jnp.clip(src, 0, a - 1)
    row_token = jnp.where(valid, _take_rows(order, src) // top_k, p % n)
    row_w = jnp.where(valid, _take_rows(sorted_w, src), 0.0)
    return (row_token, row_w.reshape(rows, 1), pos.reshape(n, top_k), tile_src, n_active.reshape(1),
            tile_pos, expert_seq, n_used.reshape(1))


def _combine_kernel(n_prompt_tiles, x_ref, ya_ref, yb_ref, gr, gt, shr, sht, scr, sct, o_ref, hn_ref):
    is_s = pl.program_id(0) >= n_prompt_tiles
    x = x_ref[...] + _pick(is_s, gr, gt) * (ya_ref[...] + yb_ref[...])
    o_ref[...] = x
    hn_ref[...] = (_rms(x) * (1.0 + _pick(is_s, scr, sct)) + _pick(is_s, shr, sht)).astype(BF16)


def _combine_final_kernel(n_prompt_tiles, x_ref, ya_ref, yb_ref, gr, gt, fg_ref, op_ref, os_ref):
    is_s = pl.program_id(0) >= n_prompt_tiles
    x = x_ref[...] + _pick(is_s, gr, gt) * (ya_ref[...] + yb_ref[...])
    y = _rms(x) * fg_ref[...]

    @pl.when(jnp.logical_not(is_s))
    def _():
        op_ref[...] = y

    @pl.when(is_s)
    def _():
        os_ref[...] = y


def _combine(st, x, y_pairs, mod_row, mod_tok, layer, j_gate, final_gain, final):
    d = st.d
    tile = pl.BlockSpec((st.tm, d), lambda i: (i, 0))
    n_tiles = st.n_tiles
    common = [tile, tile, pl.BlockSpec((st.tm, d), lambda i: (n_tiles + i, 0)),
              *st.mod_specs(layer, d, lambda i: j_gate)]
    if final:
        return pl.pallas_call(
            functools.partial(_combine_final_kernel, st.n_prompt_tiles),
            grid=(st.n_tiles,),
            in_specs=[*common, pl.BlockSpec((1, d), lambda i: (0, 0))],
            out_specs=[pl.BlockSpec((st.tm, d), lambda i: (jnp.minimum(i, n_tiles - 2), 0)),
                       pl.BlockSpec((st.tm, d), lambda i: (0, 0))],
            out_shape=[jax.ShapeDtypeStruct((st.n_prompt, d), F32), jax.ShapeDtypeStruct((st.n_sample, d), F32)],
            compiler_params=_cparams("arbitrary"),
            name="moe_combine_final",
        )(x, y_pairs, y_pairs, mod_row, mod_tok, final_gain.reshape(1, d)), None
    return pl.pallas_call(
        functools.partial(_combine_kernel, st.n_prompt_tiles),
        grid=(st.n_tiles,),
        in_specs=[*common, *st.mod_specs(layer + 1, d, lambda i: 0), *st.mod_specs(layer + 1, d, lambda i: 1)],
        out_specs=[tile, tile],
        out_shape=[jax.ShapeDtypeStruct((st.n, d), F32), jax.ShapeDtypeStruct((st.n, d), BF16)],
        compiler_params=_cparams("parallel"),
        name="moe_combine",
    )(x, y_pairs, y_pairs, mod_row, mod_tok, mod_row, mod_tok, mod_row, mod_tok)


def kernel(x_prompt, x_sample, state_hgrn, state_ret, c_prompt, c_sample, ada_w, ada_b, hgrn_w_in, hgrn_lb_logits, hgrn_norm_gain, hgrn_w_out, ret_w_in, ret_norm_gain, ret_w_out, moe_router_group, moe_router_expert, moe_w_gate, moe_w_up, moe_w_down, final_norm_gain):
    batch, seq, d = x_prompt.shape
    dec_batch, dec_seq, _ = x_sample.shape
    depth = ada_w.shape[0]
    _, _, a_heads, a_kd, a_vd = state_hgrn.shape
    _, _, b_heads, b_dk, b_dv = state_ret.shape
    _, n_groups, _, n_exp = moe_router_expert.shape
    ff = moe_w_gate.shape[-1]
    n_exp_total = n_groups * n_exp
    st = _Stream(batch, seq, dec_batch, dec_seq, d)

    p = jax.nn.softmax(hgrn_lb_logits.astype(F32), axis=0)
    lb_all = jnp.cumsum(p, axis=0) - p[0:1]
    log_gamma = jnp.log1p(-jnp.exp2(-5.0 - jnp.arange(b_heads, dtype=F32)))
    cos_p, sin_p = _rope_tables(jnp.arange(seq, dtype=F32), b_dk // 2)
    cos_s, sin_s = _rope_tables(jnp.arange(dec_seq, dtype=F32) + PAST_LEN, b_dk // 2)
    w_gate = moe_w_gate.reshape(depth * n_exp_total, d, ff)
    w_up = moe_w_up.reshape(depth * n_exp_total, d, ff)
    w_down = moe_w_down.reshape(depth * n_exp_total, ff, d)

    c_rows = -(-batch // BF16_SUBLANES) * BF16_SUBLANES
    mod_row, mod_tok = _ada_mod(jnp.pad(c_prompt, ((0, c_rows - batch), (0, 0))),
                                jnp.repeat(c_sample, dec_seq, axis=0), ada_w, ada_b)
    mod_row = mod_row[:, :batch].reshape(depth * batch, 1, 6 * d)

    x = jnp.concatenate([x_prompt.reshape(-1, d), x_sample.reshape(-1, d)], 0)
    hg_p, rt_p, hg_s, rt_s = [], [], None, None
    hn = _norm_mod(st, x, mod_row, mod_tok, 0, 0, 1)
    for l in range(depth):
        j = l // 2
        if l % 2 == 0:
            proj = _proj(st, hn, hgrn_w_in, j)
            o_p, s_p, o_s, hg_s = _hgrn_mixer(st, proj, lb_all[j], hgrn_norm_gain[j], state_hgrn, j, hg_s)
            hg_p.append(s_p)
            w_out = hgrn_w_out
        else:
            proj = _proj(st, hn, ret_w_in, j)
            o_p, s_p, o_s, rt_s = _ret_mixer(st, proj, cos_p, sin_p, cos_s, sin_s, log_gamma,
                                             ret_norm_gain[j], state_ret, j, rt_s)
            rt_p.append(s_p)
            w_out = ret_w_out
        x = _proj_residual(st, o_p, o_s, w_out, j, x, mod_row, mod_tok, l, 2)

        w_router = jnp.concatenate(
            [moe_router_group[l], jnp.moveaxis(moe_router_expert[l], 0, 1).reshape(d, n_exp_total)], 1)
        w_router = jnp.pad(w_router, ((0, 0), (0, LANES - w_router.shape[1])))
        hx, eid, ew = _norm_mod_router(st, x, mod_row, mod_tok, l, 3, 4, w_router, n_groups, n_exp)
        row_token, row_w, pos, tile_src, n_active, tile_pos, expert_seq, n_used = _dispatch_plan(
            eid, ew, n_exp_total)
        y_sorted = _experts(_take_rows(hx, row_token), row_w, tile_src, n_active, tile_pos,
                            expert_seq + l * n_exp_total, n_used, w_gate, w_up, w_down)
        y_pairs = _take_rows(y_sorted, pos.T.reshape(-1))
        x, hn = _combine(st, x, y_pairs, mod_row, mod_tok, l, 5, final_norm_gain, l == depth - 1)

    y_prompt = x[0].reshape(batch, seq, d)
    y_sample = x[1].reshape(dec_batch, dec_seq, d)
    return (y_prompt, y_sample, jnp.stack(hg_p), jnp.stack(rt_p), hg_s, rt_s)
```

```python
import functools
import math

import numpy as np
import jax
import jax.numpy as jnp
from jax import lax
from jax.experimental import pallas as pl
from jax.experimental.pallas import tpu as pltpu

F32 = jnp.float32
BF16 = jnp.bfloat16
EPS = 1e-6
PAST_LEN = 16384
ROPE_BASE = 10000.0

V7X_VMEM_LIMIT_BYTES = 56 * 1024 * 1024
LANES = 128
BF16_SUBLANES = 16

HGRN_CHUNK = 128
HGRN_HEADS_PER_STEP = 8
RET_CHUNK = 256
RET_HEADS_PER_STEP = 2
EXPERT_TILE = 256
PROJ_TN = 1024
PROJ_IN_ROW_TILES = 8
ADA_TN = 1024


def _cparams(*sem):
    return pltpu.CompilerParams(dimension_semantics=sem, vmem_limit_bytes=V7X_VMEM_LIMIT_BYTES)


def _dot(a, b):
    return jnp.dot(a, b, preferred_element_type=F32)


def _dot_nt(a, b):
    return lax.dot_general(a, b, (((1,), (1,)), ((), ())), preferred_element_type=F32)


def _dot2(m, x):
    hi = x.astype(BF16)
    lo = (x - hi.astype(F32)).astype(BF16)
    return _dot(m, hi) + _dot(m, lo)


def _dot2_wide(mm, x):
    hi = x.astype(BF16)
    lo = (x - hi.astype(F32)).astype(BF16)
    return _dot(mm, jnp.concatenate([hi, lo], axis=0))


def _dot2_lhs(x, m):
    hi = x.astype(BF16)
    lo = (x - hi.astype(F32)).astype(BF16)
    return _dot(hi, m) + _dot(lo, m)


def _silu(x):
    return x * jax.nn.sigmoid(x)


def _rms(x):
    return x * lax.rsqrt(jnp.mean(x * x, axis=-1, keepdims=True) + EPS)


def _ada_kernel(cr_ref, ct_ref, w_ref, b_ref, or_ref, ot_ref):
    w = w_ref[0].astype(BF16)
    or_ref[0] = _dot(_silu(cr_ref[...]).astype(BF16), w) + b_ref[0]
    ot_ref[0] = _dot(_silu(ct_ref[...]).astype(BF16), w) + b_ref[0]


def _ada_mod(c_row, c_tok, ada_w, ada_b):
    depth, d, n6 = ada_w.shape
    r, t = c_row.shape[0], c_tok.shape[0]
    tn = min(ADA_TN, n6)
    return pl.pallas_call(
        _ada_kernel,
        grid=(depth, n6 // tn),
        in_specs=[pl.BlockSpec((r, d), lambda l, n: (0, 0)),
                  pl.BlockSpec((t, d), lambda l, n: (0, 0)),
                  pl.BlockSpec((1, d, tn), lambda l, n: (l, 0, n)),
                  pl.BlockSpec((1, 1, tn), lambda l, n: (l, 0, n))],
        out_specs=[pl.BlockSpec((1, r, tn), lambda l, n: (l, 0, n)),
                   pl.BlockSpec((1, t, tn), lambda l, n: (l, 0, n))],
        out_shape=[jax.ShapeDtypeStruct((depth, r, n6), F32), jax.ShapeDtypeStruct((depth, t, n6), F32)],
        compiler_params=_cparams("parallel", "parallel"),
        name="ada_mod",
    )(c_row, c_tok, ada_w, ada_b.reshape(depth, 1, n6))


class _Stream:
    def __init__(self, batch, seq, dec_batch, dec_seq, d):
        self.batch, self.seq, self.dec_batch, self.dec_seq, self.d = batch, seq, dec_batch, dec_seq, d
        self.n_prompt = batch * seq
        self.n_sample = dec_batch * dec_seq
        self.tm = self.n_sample
        assert seq % self.tm == 0 and self.tm % BF16_SUBLANES == 0
        self.tiles_per_seq = seq // self.tm
        self.n_prompt_tiles = self.n_prompt // self.tm
        self.n_tiles = self.n_prompt_tiles + 1
        self.n = self.n_prompt + self.n_sample

    def mod_specs(self, layer, width, col_block, tile_axis=0):
        tps, last, batch = self.tiles_per_seq, self.batch - 1, self.batch

        def row_map(*idx):
            return (layer * batch + jnp.minimum(idx[tile_axis] // tps, last), 0, col_block(*idx))
        return (pl.BlockSpec((1, 1, width), row_map),
                pl.BlockSpec((1, self.tm, width), lambda *idx: (layer, 0, col_block(*idx))))


def _pick(is_sample, row_ref, tok_ref):
    return jnp.where(is_sample, tok_ref[0], row_ref[0])


def _norm_mod_kernel(n_prompt_tiles, x_ref, shr, sht, scr, sct, o_ref):
    is_s = pl.program_id(0) >= n_prompt_tiles
    h = _rms(x_ref[...]) * (1.0 + _pick(is_s, scr, sct)) + _pick(is_s, shr, sht)
    o_ref[...] = h.astype(BF16)


def _norm_mod(st, x, mod_row, mod_tok, layer, j_shift, j_scale):
    d = st.d
    return pl.pallas_call(
        functools.partial(_norm_mod_kernel, st.n_prompt_tiles),
        grid=(st.n_tiles,),
        in_specs=[pl.BlockSpec((st.tm, d), lambda i: (i, 0)),
                  *st.mod_specs(layer, d, lambda i: j_shift), *st.mod_specs(layer, d, lambda i: j_scale)],
        out_specs=pl.BlockSpec((st.tm, d), lambda i: (i, 0)),
        out_shape=jax.ShapeDtypeStruct((st.n, d), BF16),
        compiler_params=_cparams("parallel"),
        name="norm_mod",
    )(x, mod_row, mod_tok, mod_row, mod_tok)


def _norm_mod_router_kernel(n_prompt_tiles, n_groups, n_experts, x_ref, shr, sht, scr, sct, wr_ref,
                            o_ref, eid_ref, ew_ref):
    is_s = pl.program_id(0) >= n_prompt_tiles
    h = _rms(x_ref[...]) * (1.0 + _pick(is_s, scr, sct)) + _pick(is_s, shr, sht)
    o_ref[...] = h

    logits = _dot(h.astype(BF16), wr_ref[...].astype(BF16))
    tm = logits.shape[0]
    lane = lax.broadcasted_iota(jnp.int32, (tm, LANES), 1)
    neg_inf = jnp.float32(-jnp.inf)

    def first_index_of_max(vals, vmax):
        return jnp.min(jnp.where(vals == vmax, lane, LANES), axis=1, keepdims=True)

    is_g = lane < n_groups
    lg = jnp.where(is_g, logits, neg_inf)
    mg = jnp.max(lg, axis=1, keepdims=True)
    eg = jnp.exp(lg - mg)
    pg = eg / jnp.sum(eg, axis=1, keepdims=True)
    gi = first_index_of_max(lg, mg)
    wg = jnp.sum(jnp.where(lane == gi, pg, 0.0), axis=1, keepdims=True)

    lo = n_groups + gi * n_experts
    sel = (lane >= lo) & (lane < lo + n_experts)
    le = jnp.where(sel, logits, neg_inf)
    me = jnp.max(le, axis=1, keepdims=True)
    ee = jnp.exp(le - me)
    pe = jnp.where(sel, ee / jnp.sum(ee, axis=1, keepdims=True), -1.0)
    p1 = jnp.max(pe, axis=1, keepdims=True)
    i1 = first_index_of_max(pe, p1)
    pe2 = jnp.where(lane == i1, -1.0, pe)
    p2 = jnp.max(pe2, axis=1, keepdims=True)
    i2 = first_index_of_max(pe2, p2)
    denom = p1 + p2
    col = lax.broadcasted_iota(jnp.int32, (tm, 2), 1)
    eid_ref[...] = jnp.where(col == 0, i1, i2) - n_groups
    ew_ref[...] = wg * jnp.where(col == 0, p1 / denom, p2 / denom)


def _norm_mod_router(st, x, mod_row, mod_tok, layer, j_shift, j_scale, w_router, n_groups, n_experts):
    d = st.d
    return pl.pallas_call(
        functools.partial(_norm_mod_router_kernel, st.n_prompt_tiles, n_groups, n_experts),
        grid=(st.n_tiles,),
        in_specs=[pl.BlockSpec((st.tm, d), lambda i: (i, 0)),
                  *st.mod_specs(layer, d, lambda i: j_shift), *st.mod_specs(layer, d, lambda i: j_scale),
                  pl.BlockSpec((d, LANES), lambda i: (0, 0))],
        out_specs=[pl.BlockSpec((st.tm, d), lambda i: (i, 0)),
                   pl.BlockSpec((st.tm, 2), lambda i: (i, 0)),
                   pl.BlockSpec((st.tm, 2), lambda i: (i, 0))],
        out_shape=[jax.ShapeDtypeStruct((st.n, d), F32),
                   jax.ShapeDtypeStruct((st.n, 2), jnp.int32),
                   jax.ShapeDtypeStruct((st.n, 2), F32)],
        compiler_params=_cparams("parallel"),
        name="norm_mod_router",
    )(x, mod_row, mod_tok, mod_row, mod_tok, w_router)


def _proj_kernel(a_ref, w_ref, o_ref, wb_ref):
    @pl.when(pl.program_id(1) == 0)
    def _():
        wb_ref[...] = w_ref[0].astype(BF16)
    o_ref[...] = _dot(a_ref[...], wb_ref[...]).astype(o_ref.dtype)


def _proj(st, a, w_all, j, col0=0, n_out=None, dtype=F32):
    _, k, n_all = w_all.shape
    n_out = n_all if n_out is None else n_out
    tn = min(PROJ_TN, n_out)
    assert col0 % tn == 0 and n_out % tn == 0
    nb0 = col0 // tn
    tm = st.n // PROJ_IN_ROW_TILES if st.n % (PROJ_IN_ROW_TILES * BF16_SUBLANES) == 0 else st.tm
    return pl.pallas_call(
        _proj_kernel,
        grid=(n_out // tn, st.n // tm),
        in_specs=[pl.BlockSpec((tm, k), lambda n, m: (m, 0)),
                  pl.BlockSpec((1, k, tn), lambda n, m: (j, 0, nb0 + n))],
        out_specs=pl.BlockSpec((tm, tn), lambda n, m: (m, n)),
        out_shape=jax.ShapeDtypeStruct((st.n, n_out), dtype),
        scratch_shapes=[pltpu.VMEM((k, tn), BF16)],
        compiler_params=_cparams("parallel", "arbitrary"),
        name="proj_in",
    )(a, w_all)


def _proj_res_kernel(n_prompt_tiles, ap_ref, as_ref, w_ref, x_ref, gr, gt, o_ref, wb_ref):
    @pl.when(pl.program_id(1) == 0)
    def _():
        wb_ref[...] = w_ref[0].astype(BF16)
    is_s = pl.program_id(1) >= n_prompt_tiles
    a = jnp.where(is_s, as_ref[...], ap_ref[...])
    o_ref[...] = x_ref[...] + _pick(is_s, gr, gt) * _dot(a, wb_ref[...])


def _proj_residual(st, a_prompt, a_sample, w_all, j, x, mod_row, mod_tok, layer, j_gate):
    _, k, d = w_all.shape
    tn = min(PROJ_TN, d)
    nb = d // tn
    last_prompt = st.n_prompt_tiles - 1
    return pl.pallas_call(
        functools.partial(_proj_res_kernel, st.n_prompt_tiles),
        grid=(nb, st.n_tiles),
        in_specs=[pl.BlockSpec((st.tm, k), lambda n, m: (jnp.minimum(m, last_prompt), 0)),
                  pl.BlockSpec((st.tm, k), lambda n, m: (0, 0)),
                  pl.BlockSpec((1, k, tn), lambda n, m: (j, 0, n), pipeline_mode=pl.Buffered(1)),
                  pl.BlockSpec((st.tm, tn), lambda n, m: (m, n)),
                  *st.mod_specs(layer, tn, lambda n, m: j_gate * nb + n, tile_axis=1)],
        out_specs=pl.BlockSpec((st.tm, tn), lambda n, m: (m, n)),
        out_shape=jax.ShapeDtypeStruct((st.n, d), F32),
        scratch_shapes=[pltpu.VMEM((k, tn), BF16)],
        compiler_params=_cparams("parallel", "arbitrary"),
        name="proj_out",
    )(a_prompt, a_sample, w_all, x, mod_row, mod_tok)


def _hgrn_tables(c):
    nbits = int(math.log2(c))
    assert 1 << nbits == c
    r = np.arange(c)[:, None]
    j = np.arange(c)[None, :]
    mats = [(j <= r), (j > r)]
    for b in range(nbits):
        mid = ((r >> (b + 1)) << (b + 1)) + (1 << b)
        right = r >= mid
        mats.append(np.where(right, (j >= mid) & (j <= r), (j > r) & (j < mid)))
    level = np.full((c, c), -1, np.int32)
    level[r[:, 0], r[:, 0]] = 0
    diff = r ^ j
    for b in range(nbits):
        level[(j < r) & ((diff >> b) == 1)] = b + 1
    return np.concatenate(mats, 0).astype(np.float32), level, nbits


def _hgrn_kernel(c, nbits, hb, kd, q_ref, z_ref, i_ref, g_ref, lb_ref, gain_ref, ms_ref, lv_ref,
                 o_ref, so_ref, s_scr):
    ci = pl.program_id(2)

    @pl.when(ci == 0)
    def _():
        s_scr[...] = jnp.zeros_like(s_scr)

    lb = lb_ref[...]
    f = lb + (1.0 - lb) * jax.nn.sigmoid(z_ref[...])
    logf = jnp.log(f)
    k = 1.0 - f
    q = _silu(q_ref[...])
    v = i_ref[...].astype(BF16)
    e = _dot2_wide(ms_ref[...], logf)
    cum = e[0:c]
    qd = (q * jnp.exp(cum)).astype(BF16)
    kdec = k * jnp.exp(e[c:2 * c])
    whole_row = cum[c - 1:c]
    lv = lv_ref[...]
    masks = [lv == b for b in range(nbits + 1)]
    qb, kb = q.astype(BF16), k.astype(BF16)
    levels = [(qb, kb)]
    for b in range(nbits):
        gl = jnp.exp(e[(2 + b) * c:(3 + b) * c]).astype(BF16)
        levels.append((qb * gl, kb * gl))
    gate = gain_ref[...] * _silu(g_ref[...])

    for h in range(hb):
        sl = slice(h * kd, (h + 1) * kd)
        s = s_scr[h]
        o = _dot(qd[:, sl], s.astype(BF16))
        att = jnp.zeros((c, c), F32)
        for mask, (ql, kl) in zip(masks, levels):
            att = jnp.where(mask, _dot_nt(ql[:, sl], kl[:, sl]), att)
        o = o + _dot(att.astype(BF16), v[:, sl])
        whole = jnp.exp(jnp.broadcast_to(whole_row[:, sl], (s.shape[1], s.shape[0])).T)
        s_scr[h] = whole * s + _dot(kdec[:, sl].T.astype(BF16), v[:, sl])
        o = o * lax.rsqrt(jnp.mean(o * o, axis=-1, keepdims=True) + EPS)
        o_ref[:, sl] = (o * gate[:, sl]).astype(BF16)

    @pl.when(ci == pl.num_programs(2) - 1)
    def _():
        so_ref[0] = s_scr[...]


def _step_tables(bb, t):
    rows = bb * t
    r = np.arange(rows)
    grp, pos = r // t, r % t
    same = grp[:, None] == grp[None, :]
    i, j = r[:, None], r[None, :]
    sums = [same & (j <= i), same & (j > i), same]
    for d in range(1, t):
        sums.append(same & (j <= i) & (j > i - d) & (pos[:, None] >= d))
    shifts = [same & (j == i - d) for d in range(1, t)]
    return (np.concatenate(sums, 0).astype(np.float32), np.concatenate(shifts, 0).astype(np.float32))


def _hgrn_step_kernel(bb, t, heads, kd, q_ref, z_ref, i_ref, g_ref, lb_ref, gain_ref, ms_ref, sh_ref,
                      seg_ref, segt_ref, s_ref, o_ref, so_ref):
    rows = bb * t
    lb = lb_ref[...]
    f = lb + (1.0 - lb) * jax.nn.sigmoid(z_ref[...])
    logf = jnp.log(f)
    k = 1.0 - f
    q = _silu(q_ref[...])
    v = i_ref[...]
    e = _dot2(ms_ref[...], logf)
    qd = (q * jnp.exp(e[0:rows])).astype(BF16)
    kdec = k * jnp.exp(e[rows:2 * rows])
    whole = jnp.exp(e[2 * rows:3 * rows])

    seg, segt = seg_ref[...], segt_ref[...]
    o = jnp.zeros_like(q)
    for d in range(t):
        if d == 0:
            ks, vs, w = k, v, q * k
        else:
            sh = sh_ref[(d - 1) * rows:d * rows]
            ks, vs = _dot2(sh, k), _dot2(sh, v)
            w = q * ks * jnp.exp(e[(2 + d) * rows:(3 + d) * rows])
        att = _dot2_lhs(w, seg)
        o = o + _dot2_lhs(att, segt) * vs

    gate = gain_ref[...] * _silu(g_ref[...])
    vb = v.astype(BF16)
    row_seq = lax.broadcasted_iota(jnp.int32, (rows, kd), 0) // t
    lane_seq = lax.broadcasted_iota(jnp.int32, (kd, rows), 1) // t
    for h in range(heads):
        sl = slice(h * kd, (h + 1) * kd)
        k_t = kdec[:, sl].T
        w_t = whole[:, sl].T
        o_h = o[:, sl]
        for b in range(bb):
            s = s_ref[0, b, h]
            o_h = o_h + jnp.where(row_seq == b, _dot(qd[:, sl], s.astype(BF16)), 0.0)
            k_b = jnp.where(lane_seq == b, k_t, 0.0).astype(BF16)
            so_ref[0, b, h] = w_t[:, b * t:b * t + 1] * s + _dot(k_b, vb[:, sl])
        o_h = o_h * lax.rsqrt(jnp.mean(o_h * o_h, axis=-1, keepdims=True) + EPS)
        o_ref[:, sl] = (o_h * gate[:, sl]).astype(BF16)


def _step_blocking(st, scan_steps):
    t = st.dec_seq
    bb = BF16_SUBLANES // math.gcd(BF16_SUBLANES, t)
    n_blocks = st.dec_batch // bb
    if n_blocks > scan_steps:
        assert n_blocks % scan_steps == 0
        bb, n_blocks = bb * (n_blocks // scan_steps), scan_steps
    assert st.dec_batch == bb * n_blocks and scan_steps % n_blocks == 0 and st.n_prompt % (bb * t) == 0
    return bb, scan_steps // n_blocks


def _hgrn_step_call_kernel(bb, t, heads, kd, n_in, *refs):
    _hgrn_step_kernel(bb, t, heads, kd, *refs[:11], *refs[n_in:])


def _hgrn_mixer(st, proj, lb, gain, state_all, j, so_prev):
    _, _, heads, kd, vd = state_all.shape
    width = heads * kd
    c = min(HGRN_CHUNK, st.seq)
    hb = min(HGRN_HEADS_PER_STEP, heads)
    n_chunks = st.seq // c
    groups = heads // hb
    mstack, level, nbits = _hgrn_tables(c)
    lb2, gain2 = lb.reshape(1, width), gain.reshape(1, width)

    def pseg(jj):
        return pl.BlockSpec((c, hb * kd), lambda b, h, ci: (b * n_chunks + ci, jj * groups + h))

    head_row = pl.BlockSpec((1, hb * kd), lambda b, h, ci: (0, h))
    o_p, s_p = pl.pallas_call(
        functools.partial(_hgrn_kernel, c, nbits, hb, kd),
        grid=(st.batch, groups, n_chunks),
        in_specs=[pseg(0), pseg(1), pseg(2), pseg(3), head_row, head_row,
                  pl.BlockSpec((mstack.shape[0], 2 * c), lambda b, h, ci: (0, 0)),
                  pl.BlockSpec(level.shape, lambda b, h, ci: (0, 0))],
        out_specs=[pl.BlockSpec((c, hb * vd), lambda b, h, ci: (b * n_chunks + ci, h)),
                   pl.BlockSpec((1, hb, kd, vd), lambda b, h, ci: (b, h, 0, 0))],
        out_shape=[jax.ShapeDtypeStruct((st.n_prompt, heads * vd), BF16),
                   jax.ShapeDtypeStruct((st.batch, heads, kd, vd), F32)],
        scratch_shapes=[pltpu.VMEM((hb, kd, vd), F32)],
        compiler_params=_cparams("parallel", "parallel", "arbitrary"),
        name="hgrn_scan",
    )(proj, proj, proj, proj, lb2, gain2,
      jnp.asarray(np.concatenate([mstack, mstack], 1), BF16), jnp.asarray(level))

    t = st.dec_seq
    bb, _ = _step_blocking(st, st.dec_batch)
    rows = bb * t
    rb0 = st.n_prompt // rows
    sums, shifts = _step_tables(bb, t)
    seg = (np.arange(width)[:, None] // kd == np.arange(LANES)[None, :]).astype(np.float32)
    assert heads <= LANES

    def sseg(jj):
        return pl.BlockSpec((rows, width), lambda i: (rb0 + i, jj))

    full = lambda a: pl.BlockSpec(a.shape, lambda i: (0,) * a.ndim)
    state = pl.BlockSpec((1, bb, heads, kd, vd), lambda i: (j, i, 0, 0, 0))
    args = [proj, proj, proj, proj, lb2, gain2, jnp.asarray(sums, BF16), jnp.asarray(shifts, BF16),
            jnp.asarray(seg, BF16), jnp.asarray(seg.T, BF16), state_all]
    in_specs = [sseg(0), sseg(1), sseg(2), sseg(3), full(lb2), full(gain2), full(sums), full(shifts),
                full(seg), full(seg.T), state]
    aliases = {}
    if so_prev is not None:
        args.append(so_prev)
        in_specs.append(pl.BlockSpec(memory_space=pl.ANY))
        aliases[len(args) - 1] = 1
    o_s, so = pl.pallas_call(
        functools.partial(_hgrn_step_call_kernel, bb, t, heads, kd, len(args)),
        grid=(st.dec_batch // bb,),
        in_specs=in_specs,
        out_specs=[pl.BlockSpec((rows, width), lambda i: (i, 0)), state],
        out_shape=[jax.ShapeDtypeStruct((st.n_sample, heads * vd), BF16),
                   jax.ShapeDtypeStruct(state_all.shape, F32)],
        input_output_aliases=aliases,
        compiler_params=_cparams("parallel"),
        name="hgrn_step",
    )(*args)
    return o_p, s_p, o_s, so


def _rotate(x, cos, sin):
    half = x.shape[-1] // 2
    x1, x2 = x[:, :half], x[:, half:]
    return jnp.concatenate([x1 * cos - x2 * sin, x1 * sin + x2 * cos], axis=-1)


def _group_norm_gate(o, gain, g):
    mu = jnp.mean(o, axis=-1, keepdims=True)
    var = jnp.mean(jnp.square(o - mu), axis=-1, keepdims=True)
    return (_silu(g) * ((o - mu) * lax.rsqrt(var + EPS) * gain)).astype(BF16)


def _ret_kernel(c, dk, dv, lgs, q_ref, k_ref, v_ref, g_ref, cos_ref, sin_ref, gain_ref, o_ref, so_ref,
                s_scr, intra_scr):
    ci = pl.program_id(2)
    t_col = lax.broadcasted_iota(jnp.int32, (c, 1), 0).astype(F32)

    @pl.when(ci == 0)
    def _():
        s_scr[...] = jnp.zeros_like(s_scr)
        rel = (lax.broadcasted_iota(jnp.int32, (c, c), 0)
               - lax.broadcasted_iota(jnp.int32, (c, c), 1)).astype(F32)
        for h, lg in enumerate(lgs):
            intra_scr[h] = jnp.where(rel >= 0.0, jnp.exp(jnp.maximum(rel, 0.0) * lg), 0.0)

    cos, sin = cos_ref[...], sin_ref[...]
    for h, lg in enumerate(lgs):
        q = _rotate(q_ref[:, h * dk:(h + 1) * dk], cos, sin)
        k = _rotate(k_ref[:, h * dk:(h + 1) * dk], cos, sin) * (dk ** -0.5)
        v = v_ref[:, h * dv:(h + 1) * dv].astype(BF16)
        inter = jnp.exp((t_col + 1.0) * lg)
        tail = jnp.exp((c - 1.0 - t_col) * lg)
        whole = jnp.exp(jnp.full((1, 1), c, F32) * lg)
        s = s_scr[h]
        att = _dot_nt(q.astype(BF16), k.astype(BF16)) * intra_scr[h]
        o = _dot(att.astype(BF16), v) + _dot((q * inter).astype(BF16), s.astype(BF16))
        s_scr[h] = whole * s + _dot((k * tail).T.astype(BF16), v)
        o_ref[:, h * dv:(h + 1) * dv] = _group_norm_gate(
            o, gain_ref[:, h * dv:(h + 1) * dv], g_ref[:, h * dv:(h + 1) * dv])

    @pl.when(ci == pl.num_programs(2) - 1)
    def _():
        so_ref[0] = s_scr[...]


def _ret_step_kernel(bb, t, dk, dv, lgs, q_ref, k_ref, v_ref, g_ref, cos_ref, sin_ref, pos_ref, rel_ref,
                     gain_ref, s_ref, o_ref, so_ref):
    rows = bb * t
    cos, sin = cos_ref[...], sin_ref[...]
    t_col = pos_ref[...][:, :1]
    rel = rel_ref[...]
    row_seq = lax.broadcasted_iota(jnp.int32, (rows, dv), 0) // t
    lane_seq = lax.broadcasted_iota(jnp.int32, (dk, rows), 1) // t
    for h, lg in enumerate(lgs):
        q = _rotate(q_ref[:, h * dk:(h + 1) * dk], cos, sin)
        k = _rotate(k_ref[:, h * dk:(h + 1) * dk], cos, sin) * (dk ** -0.5)
        v = v_ref[:, h * dv:(h + 1) * dv].astype(BF16)
        inter = jnp.exp((t_col + 1.0) * lg)
        tail = jnp.exp((t - 1.0 - t_col) * lg)
        intra = jnp.where(rel >= 0.0, jnp.exp(jnp.maximum(rel, 0.0) * lg), 0.0)
        whole = jnp.exp(jnp.full((1, 1), t, F32) * lg)

        att = _dot_nt(q.astype(BF16), k.astype(BF16)) * intra
        o = _dot(att.astype(BF16), v)
        qd = (q * inter).astype(BF16)
        k_t = (k * tail).T
        for b in range(bb):
            s = s_ref[0, b, h]
            o = o + jnp.where(row_seq == b, _dot(qd, s.astype(BF16)), 0.0)
            k_b = jnp.where(lane_seq == b, k_t, 0.0).astype(BF16)
            so_ref[0, b, h] = whole * s + _dot(k_b, v)
        o_ref[:, h * dv:(h + 1) * dv] = _group_norm_gate(
            o, gain_ref[:, h * dv:(h + 1) * dv], g_ref[:, h * dv:(h + 1) * dv])


def _ret_mixer_kernel(c, hb, dk, dv, bb, t, stride, n_in, *refs):
    (lg_ref, qp, kp, vp, gp, cosp, sinp, gain_ref,
     qs, ks, vs, gs, coss, sins, pos_ref, rel_ref, s_ref) = refs[:17]
    o_p, sp_ref, o_s, so_ref, s_scr, intra_scr = refs[n_in:]
    lgs = [lg_ref[pl.program_id(1) * hb + h] for h in range(hb)]
    _ret_kernel(c, dk, dv, lgs, qp, kp, vp, gp, cosp, sinp, gain_ref, o_p, sp_ref, s_scr, intra_scr)

    @pl.when(pl.program_id(2) % stride == 0)
    def _():
        _ret_step_kernel(bb, t, dk, dv, lgs, qs, ks, vs, gs, coss, sins, pos_ref, rel_ref, gain_ref,
                         s_ref, o_s, so_ref)


def _ret_mixer(st, qk, v, g, cos_p, sin_p, cos_s, sin_s, log_gamma, gain, state_all, j, so_prev):
    _, _, heads, dk, dv = state_all.shape
    c = min(RET_CHUNK, st.seq)
    hb = min(RET_HEADS_PER_STEP, heads)
    groups = heads // hb
    n_chunks = st.seq // c
    t = st.dec_seq
    bb, stride = _step_blocking(st, st.batch * n_chunks)
    rows = bb * t
    rb0 = st.n_prompt // rows
    r = np.arange(rows)
    same = (r[:, None] // t) == (r[None, :] // t)
    rel = np.where(same & (r[None, :] <= r[:, None]), r[:, None] - r[None, :], -1).astype(np.float32)
    pos = np.broadcast_to((r % t).astype(np.float32)[:, None], (rows, LANES))
    cos_r, sin_r = jnp.tile(cos_s, (bb, 1)), jnp.tile(sin_s, (bb, 1))

    def sblock(b, ci):
        return (b * n_chunks + ci) // stride

    def pcol(width, off):
        return pl.BlockSpec((c, hb * width), lambda b, h, ci: (b * n_chunks + ci, off + h))

    def scol(width, off):
        return pl.BlockSpec((rows, hb * width), lambda b, h, ci: (rb0 + sblock(b, ci), off + h))

    full = lambda a: pl.BlockSpec(a.shape, lambda b, h, ci: (0,) * a.ndim)
    rope = pl.BlockSpec((c, dk // 2), lambda b, h, ci: (ci, 0))
    state = pl.BlockSpec((1, bb, hb, dk, dv), lambda b, h, ci: (j, sblock(b, ci), h, 0, 0))
    args = [log_gamma, qk, qk, v, g, cos_p, sin_p, gain.reshape(1, heads * dv),
            qk, qk, v, g, cos_r, sin_r, jnp.asarray(pos), jnp.asarray(rel), state_all]
    in_specs = [pl.BlockSpec(memory_space=pltpu.SMEM),
                pcol(dk, 0), pcol(dk, groups), pcol(dv, 0), pcol(dv, 0), rope, rope,
                pl.BlockSpec((1, hb * dv), lambda b, h, ci: (0, h)),
                scol(dk, 0), scol(dk, groups), scol(dv, 0), scol(dv, 0),
                full(cos_r), full(sin_r), full(pos), full(rel), state]
    aliases = {}
    if so_prev is not None:
        args.append(so_prev)
        in_specs.append(pl.BlockSpec(memory_space=pl.ANY))
        aliases[len(args) - 1] = 3
    return pl.pallas_call(
        functools.partial(_ret_mixer_kernel, c, hb, dk, dv, bb, t, stride, len(args)),
        grid=(st.batch, groups, n_chunks),
        in_specs=in_specs,
        out_specs=[pl.BlockSpec((c, hb * dv), lambda b, h, ci: (b * n_chunks + ci, h)),
                   pl.BlockSpec((1, hb, dk, dv), lambda b, h, ci: (b, h, 0, 0)),
                   pl.BlockSpec((rows, hb * dv), lambda b, h, ci: (sblock(b, ci), h)),
                   state],
        out_shape=[jax.ShapeDtypeStruct((st.n_prompt, heads * dv), BF16),
                   jax.ShapeDtypeStruct((st.batch, heads, dk, dv), F32),
                   jax.ShapeDtypeStruct((st.n_sample, heads * dv), BF16),
                   jax.ShapeDtypeStruct(state_all.shape, F32)],
        scratch_shapes=[pltpu.VMEM((hb, dk, dv), F32), pltpu.VMEM((hb, c, c), F32)],
        input_output_aliases=aliases,
        compiler_params=_cparams("arbitrary", "arbitrary", "arbitrary"),
        name="ret_mixer",
    )(*args)


def _rope_tables(pos, half):
    inv = 1.0 / (ROPE_BASE ** jnp.linspace(0.0, 1.0, half, dtype=F32))
    ang = pos[:, None] * inv[None]
    return jnp.cos(ang), jnp.sin(ang)


def _expert_kernel(ts_ref, na_ref, tk_ref, eseq_ref, nk_ref, x_ref, w_ref, wg_hbm, wu_hbm, wd_hbm, o_ref,
                   wgb, wub, wdb, stage_g, stage_u, stage_d, sems):
    t = pl.program_id(0)
    active = t < na_ref[0]
    k = tk_ref[t]
    new_expert = (t == 0) | (k != tk_ref[jnp.maximum(t - 1, 0)])

    def weight_copies(kk, slot):
        e = eseq_ref[kk]
        return (pltpu.make_async_copy(wg_hbm.at[e], stage_g.at[slot], sems.at[0, slot]),
                pltpu.make_async_copy(wu_hbm.at[e], stage_u.at[slot], sems.at[1, slot]),
                pltpu.make_async_copy(wd_hbm.at[e], stage_d.at[slot], sems.at[2, slot]))

    @pl.when(t == 0)
    def _():
        for cp in weight_copies(0, 0):
            cp.start()

    @pl.when(active & new_expert)
    def _():
        slot = lax.rem(k, 2)

        @pl.when(k + 1 < nk_ref[0])
        def _():
            for cp in weight_copies(k + 1, 1 - slot):
                cp.start()

        for cp in weight_copies(k, slot):
            cp.wait()
        wgb[...] = stage_g[slot].astype(BF16)
        wub[...] = stage_u[slot].astype(BF16)
        wdb[...] = stage_d[slot].astype(BF16)

    @pl.when(active)
    def _():
        x = x_ref[...].astype(BF16)
        hg = _silu(_dot(x, wgb[...])) * _dot(x, wub[...])
        o_ref[...] = _dot((hg * w_ref[...]).astype(BF16), wdb[...])

    @pl.when(jnp.logical_not(active))
    def _():
        o_ref[...] = jnp.zeros_like(o_ref)


def _experts(x_sorted, w_sorted, tile_src, n_active, tile_pos, expert_seq, n_used, w_gate, w_up, w_down):
    rows = x_sorted.shape[0]
    _, d, ff = w_gate.shape
    tm = EXPERT_TILE
    hbm = pl.BlockSpec(memory_space=pl.ANY)
    grid_spec = pltpu.PrefetchScalarGridSpec(
        num_scalar_prefetch=5,
        grid=(rows // tm,),
        in_specs=[pl.BlockSpec((tm, d), lambda t, ts, *_: (ts[t], 0)),
                  pl.BlockSpec((tm, 1), lambda t, ts, *_: (ts[t], 0)),
                  hbm, hbm, hbm],
        out_specs=pl.BlockSpec((tm, d), lambda t, *_: (t, 0)),
        scratch_shapes=[pltpu.VMEM((d, ff), BF16), pltpu.VMEM((d, ff), BF16), pltpu.VMEM((ff, d), BF16),
                        pltpu.VMEM((2, d, ff), F32), pltpu.VMEM((2, d, ff), F32), pltpu.VMEM((2, ff, d), F32),
                        pltpu.SemaphoreType.DMA((3, 2))],
    )
    return pl.pallas_call(
        _expert_kernel,
        grid_spec=grid_spec,
        out_shape=jax.ShapeDtypeStruct((rows, d), F32),
        compiler_params=_cparams("arbitrary"),
        name="experts",
    )(tile_src, n_active, tile_pos, expert_seq, n_used, x_sorted, w_sorted, w_gate, w_up, w_down)


def _take_rows(a, idx):
    return a.at[idx].get(mode="promise_in_bounds")


def _step_lookup(table, starts, x):
    jumps = table[1:] - table[:-1]
    return table[0] + jnp.sum(jnp.where(x[None, :] >= starts[1:, None], jumps[:, None], 0), axis=0)


def _dispatch_plan(eid, ew, n_experts_total):
    n, top_k = eid.shape
    tm = EXPERT_TILE
    a = n * top_k
    n_tiles = a // tm + n_experts_total
    rows = n_tiles * tm
    i32 = jnp.int32
    q = jnp.arange(a, dtype=i32)
    sorted_e, order, sorted_w = lax.sort((eid.reshape(a), q, ew.reshape(a)), num_keys=1, is_stable=True)
    bounds = jnp.sum(sorted_e[None, :] < jnp.arange(n_experts_total + 1, dtype=i32)[:, None], axis=1, dtype=i32)
    off, end = bounds[:-1], bounds[1:]
    padded = (end - off + tm - 1) // tm * tm
    pad_end = jnp.cumsum(padded)
    pad_off = pad_end - padded
    delta = pad_off - off
    _, pos = lax.sort((order, q + _step_lookup(delta, off, q)), num_keys=1)
    n_active = (pad_end[-1] // tm).astype(i32)
    tile_src = jnp.minimum(jnp.arange(n_tiles, dtype=i32), n_active - 1)
    experts = jnp.arange(n_experts_total, dtype=i32)
    used = padded > 0
    n_used = jnp.sum(used, dtype=i32)
    expert_seq = jnp.minimum(jnp.sort(jnp.where(used, experts, n_experts_total)), n_experts_total - 1)
    tile_pos = _step_lookup(jnp.cumsum(used, dtype=i32) - 1, pad_off, tile_src * tm)
    p = jnp.arange(rows, dtype=i32)
    src = p - _step_lookup(delta, pad_off, p)
    valid = (src < _step_lookup(end, pad_off, p)) & (p < pad_end[-1])
    src = jnp.clip(src, 0, a - 1)
    row_token = jnp.where(valid, _take_rows(order, src) // top_k, p % n)
    row_w = jnp.where(valid, _take_rows(sorted_w, src), 0.0)
    return (row_token, row_w.reshape(rows, 1), pos.reshape(n, top_k), tile_src, n_active.reshape(1),
            tile_pos, expert_seq, n_used.reshape(1))


def _combine_kernel(n_prompt_tiles, x_ref, ya_ref, yb_ref, gr, gt, shr, sht, scr, sct, o_ref, hn_ref):
    is_s = pl.program_id(0) >= n_prompt_tiles
    x = x_ref[...] + _pick(is_s, gr, gt) * (ya_ref[...] + yb_ref[...])
    o_ref[...] = x
    hn_ref[...] = (_rms(x) * (1.0 + _pick(is_s, scr, sct)) + _pick(is_s, shr, sht)).astype(BF16)


def _combine_final_kernel(n_prompt_tiles, x_ref, ya_ref, yb_ref, gr, gt, fg_ref, op_ref, os_ref):
    is_s = pl.program_id(0) >= n_prompt_tiles
    x = x_ref[...] + _pick(is_s, gr, gt) * (ya_ref[...] + yb_ref[...])
    y = _rms(x) * fg_ref[...]

    @pl.when(jnp.logical_not(is_s))
    def _():
        op_ref[...] = y

    @pl.when(is_s)
    def _():
        os_ref[...] = y


def _combine(st, x, y_pairs, mod_row, mod_tok, layer, j_gate, final_gain, final):
    d = st.d
    tile = pl.BlockSpec((st.tm, d), lambda i: (i, 0))
    n_tiles = st.n_tiles
    common = [tile, tile, pl.BlockSpec((st.tm, d), lambda i: (n_tiles + i, 0)),
              *st.mod_specs(layer, d, lambda i: j_gate)]
    if final:
        return pl.pallas_call(
            functools.partial(_combine_final_kernel, st.n_prompt_tiles),
            grid=(st.n_tiles,),
            in_specs=[*common, pl.BlockSpec((1, d), lambda i: (0, 0))],
            out_specs=[pl.BlockSpec((st.tm, d), lambda i: (jnp.minimum(i, n_tiles - 2), 0)),
                       pl.BlockSpec((st.tm, d), lambda i: (0, 0))],
            out_shape=[jax.ShapeDtypeStruct((st.n_prompt, d), F32), jax.ShapeDtypeStruct((st.n_sample, d), F32)],
            compiler_params=_cparams("arbitrary"),
            name="moe_combine_final",
        )(x, y_pairs, y_pairs, mod_row, mod_tok, final_gain.reshape(1, d)), None
    return pl.pallas_call(
        functools.partial(_combine_kernel, st.n_prompt_tiles),
        grid=(st.n_tiles,),
        in_specs=[*common, *st.mod_specs(layer + 1, d, lambda i: 0), *st.mod_specs(layer + 1, d, lambda i: 1)],
        out_specs=[tile, tile],
        out_shape=[jax.ShapeDtypeStruct((st.n, d), F32), jax.ShapeDtypeStruct((st.n, d), BF16)],
        compiler_params=_cparams("parallel"),
        name="moe_combine",
    )(x, y_pairs, y_pairs, mod_row, mod_tok, mod_row, mod_tok, mod_row, mod_tok)


def kernel(x_prompt, x_sample, state_hgrn, state_ret, c_prompt, c_sample, ada_w, ada_b, hgrn_w_in, hgrn_lb_logits, hgrn_norm_gain, hgrn_w_out, ret_w_in, ret_norm_gain, ret_w_out, moe_router_group, moe_router_expert, moe_w_gate, moe_w_up, moe_w_down, final_norm_gain):
    batch, seq, d = x_prompt.shape
    dec_batch, dec_seq, _ = x_sample.shape
    depth = ada_w.shape[0]
    _, _, a_heads, a_kd, a_vd = state_hgrn.shape
    _, _, b_heads, b_dk, b_dv = state_ret.shape
    _, n_groups, _, n_exp = moe_router_expert.shape
    ff = moe_w_gate.shape[-1]
    n_exp_total = n_groups * n_exp
    st = _Stream(batch, seq, dec_batch, dec_seq, d)

    p = jax.nn.softmax(hgrn_lb_logits.astype(F32), axis=0)
    lb_all = jnp.cumsum(p, axis=0) - p[0:1]
    log_gamma = jnp.log1p(-jnp.exp2(-5.0 - jnp.arange(b_heads, dtype=F32)))
    cos_p, sin_p = _rope_tables(jnp.arange(seq, dtype=F32), b_dk // 2)
    cos_s, sin_s = _rope_tables(jnp.arange(dec_seq, dtype=F32) + PAST_LEN, b_dk // 2)
    w_gate = moe_w_gate.reshape(depth * n_exp_total, d, ff)
    w_up = moe_w_up.reshape(depth * n_exp_total, d, ff)
    w_down = moe_w_down.reshape(depth * n_exp_total, ff, d)

    c_rows = -(-batch // BF16_SUBLANES) * BF16_SUBLANES
    mod_row, mod_tok = _ada_mod(jnp.pad(c_prompt, ((0, c_rows - batch), (0, 0))),
                                jnp.repeat(c_sample, dec_seq, axis=0), ada_w, ada_b)
    mod_row = mod_row[:, :batch].reshape(depth * batch, 1, 6 * d)

    x = jnp.concatenate([x_prompt.reshape(-1, d), x_sample.reshape(-1, d)], 0)
    hg_p, rt_p, hg_s, rt_s = [], [], None, None
    hn = _norm_mod(st, x, mod_row, mod_tok, 0, 0, 1)
    for l in range(depth):
        j = l // 2
        if l % 2 == 0:
            proj = _proj(st, hn, hgrn_w_in, j)
            o_p, s_p, o_s, hg_s = _hgrn_mixer(st, proj, lb_all[j], hgrn_norm_gain[j], state_hgrn, j, hg_s)
            hg_p.append(s_p)
            w_out = hgrn_w_out
        else:
            n_qk, n_v = 2 * b_heads * b_dk, b_heads * b_dv
            qk = _proj(st, hn, ret_w_in, j, 0, n_qk)
            v = _proj(st, hn, ret_w_in, j, n_qk, n_v, BF16)
            g = _proj(st, hn, ret_w_in, j, n_qk + n_v, n_v)
            o_p, s_p, o_s, rt_s = _ret_mixer(st, qk, v, g, cos_p, sin_p, cos_s, sin_s, log_gamma,
                                             ret_norm_gain[j], state_ret, j, rt_s)
            rt_p.append(s_p)
            w_out = ret_w_out
        x = _proj_residual(st, o_p, o_s, w_out, j, x, mod_row, mod_tok, l, 2)

        w_router = jnp.concatenate(
            [moe_router_group[l], jnp.moveaxis(moe_router_expert[l], 0, 1).reshape(d, n_exp_total)], 1)
        w_router = jnp.pad(w_router, ((0, 0), (0, LANES - w_router.shape[1])))
        hx, eid, ew = _norm_mod_router(st, x, mod_row, mod_tok, l, 3, 4, w_router, n_groups, n_exp)
        row_token, row_w, pos, tile_src, n_active, tile_pos, expert_seq, n_used = _dispatch_plan(
            eid, ew, n_exp_total)
        y_sorted = _experts(_take_rows(hx, row_token), row_w, tile_src, n_active, tile_pos,
                            expert_seq + l * n_exp_total, n_used, w_gate, w_up, w_down)
        y_pairs = _take_rows(y_sorted, pos.T.reshape(-1))
        x, hn = _combine(st, x, y_pairs, mod_row, mod_tok, l, 5, final_norm_gain, l == depth - 1)

    y_prompt = x[0].reshape(batch, seq, d)
    y_sample = x[1].reshape(dec_batch, dec_seq, d)
    return (y_prompt, y_sample, jnp.stack(hg_p), jnp.stack(rt_p), hg_s, rt_s)
```

```python
import functools
import math

import numpy as np
import jax
import jax.numpy as jnp
from jax import lax
from jax.experimental import pallas as pl
from jax.experimental.pallas import tpu as pltpu

F32 = jnp.float32
BF16 = jnp.bfloat16
EPS = 1e-6
PAST_LEN = 16384
ROPE_BASE = 10000.0

V7X_VMEM_LIMIT_BYTES = 56 * 1024 * 1024
LANES = 128
BF16_SUBLANES = 16

HGRN_CHUNK = 128
HGRN_HEADS_PER_STEP = 16
RET_CHUNK = 256
RET_HEADS_PER_STEP = 4
EXPERT_TILE = 256
PROJ_TN = 1024
PROJ_IN_ROW_TILES = 8
ADA_TN = 1024


def _cparams(*sem):
    return pltpu.CompilerParams(dimension_semantics=sem, vmem_limit_bytes=V7X_VMEM_LIMIT_BYTES)


def _dot(a, b):
    return jnp.dot(a, b, preferred_element_type=F32)


def _dot_nt(a, b):
    return lax.dot_general(a, b, (((1,), (1,)), ((), ())), preferred_element_type=F32)


def _dot2(m, x):
    hi = x.astype(BF16)
    lo = (x - hi.astype(F32)).astype(BF16)
    return _dot(m, hi) + _dot(m, lo)


def _dot2_wide(mm, x):
    hi = x.astype(BF16)
    lo = (x - hi.astype(F32)).astype(BF16)
    return _dot(mm, jnp.concatenate([hi, lo], axis=0))


def _dot2_lhs(x, m):
    hi = x.astype(BF16)
    lo = (x - hi.astype(F32)).astype(BF16)
    return _dot(hi, m) + _dot(lo, m)


def _silu(x):
    return x * jax.nn.sigmoid(x)


def _rms(x):
    return x * lax.rsqrt(jnp.mean(x * x, axis=-1, keepdims=True) + EPS)


def _ada_kernel(cr_ref, ct_ref, w_ref, b_ref, or_ref, ot_ref):
    w = w_ref[0].astype(BF16)
    or_ref[0] = _dot(_silu(cr_ref[...]).astype(BF16), w) + b_ref[0]
    ot_ref[0] = _dot(_silu(ct_ref[...]).astype(BF16), w) + b_ref[0]


def _ada_mod(c_row, c_tok, ada_w, ada_b):
    depth, d, n6 = ada_w.shape
    r, t = c_row.shape[0], c_tok.shape[0]
    tn = min(ADA_TN, n6)
    return pl.pallas_call(
        _ada_kernel,
        grid=(depth, n6 // tn),
        in_specs=[pl.BlockSpec((r, d), lambda l, n: (0, 0)),
                  pl.BlockSpec((t, d), lambda l, n: (0, 0)),
                  pl.BlockSpec((1, d, tn), lambda l, n: (l, 0, n)),
                  pl.BlockSpec((1, 1, tn), lambda l, n: (l, 0, n))],
        out_specs=[pl.BlockSpec((1, r, tn), lambda l, n: (l, 0, n)),
                   pl.BlockSpec((1, t, tn), lambda l, n: (l, 0, n))],
        out_shape=[jax.ShapeDtypeStruct((depth, r, n6), F32), jax.ShapeDtypeStruct((depth, t, n6), F32)],
        compiler_params=_cparams("parallel", "parallel"),
        name="ada_mod",
    )(c_row, c_tok, ada_w, ada_b.reshape(depth, 1, n6))


class _Stream:
    def __init__(self, batch, seq, dec_batch, dec_seq, d):
        self.batch, self.seq, self.dec_batch, self.dec_seq, self.d = batch, seq, dec_batch, dec_seq, d
        self.n_prompt = batch * seq
        self.n_sample = dec_batch * dec_seq
        self.tm = self.n_sample
        assert seq % self.tm == 0 and self.tm % BF16_SUBLANES == 0
        self.tiles_per_seq = seq // self.tm
        self.n_prompt_tiles = self.n_prompt // self.tm
        self.n_tiles = self.n_prompt_tiles + 1
        self.n = self.n_prompt + self.n_sample

    def mod_specs(self, layer, width, col_block, tile_axis=0):
        tps, last, batch = self.tiles_per_seq, self.batch - 1, self.batch

        def row_map(*idx):
            return (layer * batch + jnp.minimum(idx[tile_axis] // tps, last), 0, col_block(*idx))
        return (pl.BlockSpec((1, 1, width), row_map),
                pl.BlockSpec((1, self.tm, width), lambda *idx: (layer, 0, col_block(*idx))))


def _pick(is_sample, row_ref, tok_ref):
    return jnp.where(is_sample, tok_ref[0], row_ref[0])


def _norm_mod_kernel(n_prompt_tiles, x_ref, shr, sht, scr, sct, o_ref):
    is_s = pl.program_id(0) >= n_prompt_tiles
    h = _rms(x_ref[...]) * (1.0 + _pick(is_s, scr, sct)) + _pick(is_s, shr, sht)
    o_ref[...] = h.astype(BF16)


def _norm_mod(st, x, mod_row, mod_tok, layer, j_shift, j_scale):
    d = st.d
    return pl.pallas_call(
        functools.partial(_norm_mod_kernel, st.n_prompt_tiles),
        grid=(st.n_tiles,),
        in_specs=[pl.BlockSpec((st.tm, d), lambda i: (i, 0)),
                  *st.mod_specs(layer, d, lambda i: j_shift), *st.mod_specs(layer, d, lambda i: j_scale)],
        out_specs=pl.BlockSpec((st.tm, d), lambda i: (i, 0)),
        out_shape=jax.ShapeDtypeStruct((st.n, d), BF16),
        compiler_params=_cparams("parallel"),
        name="norm_mod",
    )(x, mod_row, mod_tok, mod_row, mod_tok)


def _norm_mod_router_kernel(n_prompt_tiles, n_groups, n_experts, x_ref, shr, sht, scr, sct, wr_ref,
                            o_ref, eid_ref, ew_ref):
    is_s = pl.program_id(0) >= n_prompt_tiles
    h = _rms(x_ref[...]) * (1.0 + _pick(is_s, scr, sct)) + _pick(is_s, shr, sht)
    o_ref[...] = h

    logits = _dot(h.astype(BF16), wr_ref[...].astype(BF16))
    tm = logits.shape[0]
    lane = lax.broadcasted_iota(jnp.int32, (tm, LANES), 1)
    neg_inf = jnp.float32(-jnp.inf)

    def first_index_of_max(vals, vmax):
        return jnp.min(jnp.where(vals == vmax, lane, LANES), axis=1, keepdims=True)

    is_g = lane < n_groups
    lg = jnp.where(is_g, logits, neg_inf)
    mg = jnp.max(lg, axis=1, keepdims=True)
    eg = jnp.exp(lg - mg)
    pg = eg / jnp.sum(eg, axis=1, keepdims=True)
    gi = first_index_of_max(lg, mg)
    wg = jnp.sum(jnp.where(lane == gi, pg, 0.0), axis=1, keepdims=True)

    lo = n_groups + gi * n_experts
    sel = (lane >= lo) & (lane < lo + n_experts)
    le = jnp.where(sel, logits, neg_inf)
    me = jnp.max(le, axis=1, keepdims=True)
    ee = jnp.exp(le - me)
    pe = jnp.where(sel, ee / jnp.sum(ee, axis=1, keepdims=True), -1.0)
    p1 = jnp.max(pe, axis=1, keepdims=True)
    i1 = first_index_of_max(pe, p1)
    pe2 = jnp.where(lane == i1, -1.0, pe)
    p2 = jnp.max(pe2, axis=1, keepdims=True)
    i2 = first_index_of_max(pe2, p2)
    denom = p1 + p2
    col = lax.broadcasted_iota(jnp.int32, (tm, 2), 1)
    eid_ref[...] = jnp.where(col == 0, i1, i2) - n_groups
    ew_ref[...] = wg * jnp.where(col == 0, p1 / denom, p2 / denom)


def _norm_mod_router(st, x, mod_row, mod_tok, layer, j_shift, j_scale, w_router, n_groups, n_experts):
    d = st.d
    return pl.pallas_call(
        functools.partial(_norm_mod_router_kernel, st.n_prompt_tiles, n_groups, n_experts),
        grid=(st.n_tiles,),
        in_specs=[pl.BlockSpec((st.tm, d), lambda i: (i, 0)),
                  *st.mod_specs(layer, d, lambda i: j_shift), *st.mod_specs(layer, d, lambda i: j_scale),
                  pl.BlockSpec((d, LANES), lambda i: (0, 0))],
        out_specs=[pl.BlockSpec((st.tm, d), lambda i: (i, 0)),
                   pl.BlockSpec((st.tm, 2), lambda i: (i, 0)),
                   pl.BlockSpec((st.tm, 2), lambda i: (i, 0))],
        out_shape=[jax.ShapeDtypeStruct((st.n, d), F32),
                   jax.ShapeDtypeStruct((st.n, 2), jnp.int32),
                   jax.ShapeDtypeStruct((st.n, 2), F32)],
        compiler_params=_cparams("parallel"),
        name="norm_mod_router",
    )(x, mod_row, mod_tok, mod_row, mod_tok, w_router)


def _proj_kernel(a_ref, w_ref, o_ref, wb_ref):
    @pl.when(pl.program_id(1) == 0)
    def _():
        wb_ref[...] = w_ref[0].astype(BF16)
    o_ref[...] = _dot(a_ref[...], wb_ref[...]).astype(o_ref.dtype)


def _proj(st, a, w_all, j, col0=0, n_out=None, dtype=F32):
    _, k, n_all = w_all.shape
    n_out = n_all if n_out is None else n_out
    tn = min(PROJ_TN, n_out)
    assert col0 % tn == 0 and n_out % tn == 0
    nb0 = col0 // tn
    tm = st.n // PROJ_IN_ROW_TILES if st.n % (PROJ_IN_ROW_TILES * BF16_SUBLANES) == 0 else st.tm
    return pl.pallas_call(
        _proj_kernel,
        grid=(n_out // tn, st.n // tm),
        in_specs=[pl.BlockSpec((tm, k), lambda n, m: (m, 0)),
                  pl.BlockSpec((1, k, tn), lambda n, m: (j, 0, nb0 + n))],
        out_specs=pl.BlockSpec((tm, tn), lambda n, m: (m, n)),
        out_shape=jax.ShapeDtypeStruct((st.n, n_out), dtype),
        scratch_shapes=[pltpu.VMEM((k, tn), BF16)],
        compiler_params=_cparams("parallel", "arbitrary"),
        name="proj_in",
    )(a, w_all)


def _proj_res_kernel(n_prompt_tiles, ap_ref, as_ref, w_ref, x_ref, gr, gt, o_ref, wb_ref):
    @pl.when(pl.program_id(1) == 0)
    def _():
        wb_ref[...] = w_ref[0].astype(BF16)
    is_s = pl.program_id(1) >= n_prompt_tiles
    a = jnp.where(is_s, as_ref[...], ap_ref[...])
    o_ref[...] = x_ref[...] + _pick(is_s, gr, gt) * _dot(a, wb_ref[...])


def _proj_residual(st, a_prompt, a_sample, w_all, j, x, mod_row, mod_tok, layer, j_gate):
    _, k, d = w_all.shape
    tn = min(PROJ_TN, d)
    nb = d // tn
    last_prompt = st.n_prompt_tiles - 1
    return pl.pallas_call(
        functools.partial(_proj_res_kernel, st.n_prompt_tiles),
        grid=(nb, st.n_tiles),
        in_specs=[pl.BlockSpec((st.tm, k), lambda n, m: (jnp.minimum(m, last_prompt), 0)),
                  pl.BlockSpec((st.tm, k), lambda n, m: (0, 0)),
                  pl.BlockSpec((1, k, tn), lambda n, m: (j, 0, n), pipeline_mode=pl.Buffered(1)),
                  pl.BlockSpec((st.tm, tn), lambda n, m: (m, n)),
                  *st.mod_specs(layer, tn, lambda n, m: j_gate * nb + n, tile_axis=1)],
        out_specs=pl.BlockSpec((st.tm, tn), lambda n, m: (m, n)),
        out_shape=jax.ShapeDtypeStruct((st.n, d), F32),
        scratch_shapes=[pltpu.VMEM((k, tn), BF16)],
        compiler_params=_cparams("parallel", "arbitrary"),
        name="proj_out",
    )(a_prompt, a_sample, w_all, x, mod_row, mod_tok)


def _hgrn_tables(c):
    nbits = int(math.log2(c))
    assert 1 << nbits == c
    r = np.arange(c)[:, None]
    j = np.arange(c)[None, :]
    mats = [(j <= r), (j > r)]
    for b in range(nbits):
        mid = ((r >> (b + 1)) << (b + 1)) + (1 << b)
        right = r >= mid
        mats.append(np.where(right, (j >= mid) & (j <= r), (j > r) & (j < mid)))
    level = np.full((c, c), -1, np.int32)
    level[r[:, 0], r[:, 0]] = 0
    diff = r ^ j
    for b in range(nbits):
        level[(j < r) & ((diff >> b) == 1)] = b + 1
    return np.concatenate(mats, 0).astype(np.float32), level, nbits


def _hgrn_kernel(c, nbits, hb, kd, q_ref, z_ref, i_ref, g_ref, lb_ref, gain_ref, ms_ref, lv_ref,
                 o_ref, so_ref, s_scr):
    ci = pl.program_id(2)

    @pl.when(ci == 0)
    def _():
        s_scr[...] = jnp.zeros_like(s_scr)

    lb = lb_ref[...]
    f = lb + (1.0 - lb) * jax.nn.sigmoid(z_ref[...])
    logf = jnp.log(f)
    k = 1.0 - f
    q = _silu(q_ref[...])
    v = i_ref[...].astype(BF16)
    e = _dot2_wide(ms_ref[...], logf)
    cum = e[0:c]
    qd = (q * jnp.exp(cum)).astype(BF16)
    kdec = k * jnp.exp(e[c:2 * c])
    whole_row = cum[c - 1:c]
    lv = lv_ref[...]
    masks = [lv == b for b in range(nbits + 1)]
    qb, kb = q.astype(BF16), k.astype(BF16)
    levels = [(qb, kb)]
    for b in range(nbits):
        gl = jnp.exp(e[(2 + b) * c:(3 + b) * c]).astype(BF16)
        levels.append((qb * gl, kb * gl))
    gate = gain_ref[...] * _silu(g_ref[...])

    for h in range(hb):
        sl = slice(h * kd, (h + 1) * kd)
        s = s_scr[h]
        o = _dot(qd[:, sl], s.astype(BF16))
        att = jnp.zeros((c, c), F32)
        for mask, (ql, kl) in zip(masks, levels):
            att = jnp.where(mask, _dot_nt(ql[:, sl], kl[:, sl]), att)
        o = o + _dot(att.astype(BF16), v[:, sl])
        whole = jnp.exp(jnp.broadcast_to(whole_row[:, sl], (s.shape[1], s.shape[0])).T)
        s_scr[h] = whole * s + _dot(kdec[:, sl].T.astype(BF16), v[:, sl])
        o = o * lax.rsqrt(jnp.mean(o * o, axis=-1, keepdims=True) + EPS)
        o_ref[:, sl] = (o * gate[:, sl]).astype(BF16)

    @pl.when(ci == pl.num_programs(2) - 1)
    def _():
        so_ref[0] = s_scr[...]


def _step_tables(bb, t):
    rows = bb * t
    r = np.arange(rows)
    grp, pos = r // t, r % t
    same = grp[:, None] == grp[None, :]
    i, j = r[:, None], r[None, :]
    sums = [same & (j <= i), same & (j > i), same]
    for d in range(1, t):
        sums.append(same & (j <= i) & (j > i - d) & (pos[:, None] >= d))
    shifts = [same & (j == i - d) for d in range(1, t)]
    return (np.concatenate(sums, 0).astype(np.float32), np.concatenate(shifts, 0).astype(np.float32))


def _hgrn_step_kernel(bb, t, heads, kd, q_ref, z_ref, i_ref, g_ref, lb_ref, gain_ref, ms_ref, sh_ref,
                      seg_ref, segt_ref, s_ref, o_ref, so_ref):
    rows = bb * t
    lb = lb_ref[...]
    f = lb + (1.0 - lb) * jax.nn.sigmoid(z_ref[...])
    logf = jnp.log(f)
    k = 1.0 - f
    q = _silu(q_ref[...])
    v = i_ref[...]
    e = _dot2(ms_ref[...], logf)
    qd = (q * jnp.exp(e[0:rows])).astype(BF16)
    kdec = k * jnp.exp(e[rows:2 * rows])
    whole = jnp.exp(e[2 * rows:3 * rows])

    seg, segt = seg_ref[...], segt_ref[...]
    o = jnp.zeros_like(q)
    for d in range(t):
        if d == 0:
            ks, vs, w = k, v, q * k
        else:
            sh = sh_ref[(d - 1) * rows:d * rows]
            ks, vs = _dot2(sh, k), _dot2(sh, v)
            w = q * ks * jnp.exp(e[(2 + d) * rows:(3 + d) * rows])
        att = _dot2_lhs(w, seg)
        o = o + _dot2_lhs(att, segt) * vs

    gate = gain_ref[...] * _silu(g_ref[...])
    vb = v.astype(BF16)
    row_seq = lax.broadcasted_iota(jnp.int32, (rows, kd), 0) // t
    lane_seq = lax.broadcasted_iota(jnp.int32, (kd, rows), 1) // t
    for h in range(heads):
        sl = slice(h * kd, (h + 1) * kd)
        k_t = kdec[:, sl].T
        w_t = whole[:, sl].T
        o_h = o[:, sl]
        for b in range(bb):
            s = s_ref[0, b, h]
            o_h = o_h + jnp.where(row_seq == b, _dot(qd[:, sl], s.astype(BF16)), 0.0)
            k_b = jnp.where(lane_seq == b, k_t, 0.0).astype(BF16)
            so_ref[0, b, h] = w_t[:, b * t:b * t + 1] * s + _dot(k_b, vb[:, sl])
        o_h = o_h * lax.rsqrt(jnp.mean(o_h * o_h, axis=-1, keepdims=True) + EPS)
        o_ref[:, sl] = (o_h * gate[:, sl]).astype(BF16)


def _step_blocking(st, scan_steps):
    t = st.dec_seq
    bb = BF16_SUBLANES // math.gcd(BF16_SUBLANES, t)
    n_blocks = st.dec_batch // bb
    if n_blocks > scan_steps:
        assert n_blocks % scan_steps == 0
        bb, n_blocks = bb * (n_blocks // scan_steps), scan_steps
    assert st.dec_batch == bb * n_blocks and scan_steps % n_blocks == 0 and st.n_prompt % (bb * t) == 0
    return bb, scan_steps // n_blocks


def _hgrn_step_call_kernel(bb, t, heads, kd, n_in, *refs):
    _hgrn_step_kernel(bb, t, heads, kd, *refs[:11], *refs[n_in:])


def _hgrn_mixer(st, proj, lb, gain, state_all, j, so_prev):
    _, _, heads, kd, vd = state_all.shape
    width = heads * kd
    c = min(HGRN_CHUNK, st.seq)
    hb = min(HGRN_HEADS_PER_STEP, heads)
    n_chunks = st.seq // c
    groups = heads // hb
    mstack, level, nbits = _hgrn_tables(c)
    lb2, gain2 = lb.reshape(1, width), gain.reshape(1, width)

    def pseg(jj):
        return pl.BlockSpec((c, hb * kd), lambda b, h, ci: (b * n_chunks + ci, jj * groups + h))

    head_row = pl.BlockSpec((1, hb * kd), lambda b, h, ci: (0, h))
    o_p, s_p = pl.pallas_call(
        functools.partial(_hgrn_kernel, c, nbits, hb, kd),
        grid=(st.batch, groups, n_chunks),
        in_specs=[pseg(0), pseg(1), pseg(2), pseg(3), head_row, head_row,
                  pl.BlockSpec((mstack.shape[0], 2 * c), lambda b, h, ci: (0, 0)),
                  pl.BlockSpec(level.shape, lambda b, h, ci: (0, 0))],
        out_specs=[pl.BlockSpec((c, hb * vd), lambda b, h, ci: (b * n_chunks + ci, h)),
                   pl.BlockSpec((1, hb, kd, vd), lambda b, h, ci: (b, h, 0, 0))],
        out_shape=[jax.ShapeDtypeStruct((st.n_prompt, heads * vd), BF16),
                   jax.ShapeDtypeStruct((st.batch, heads, kd, vd), F32)],
        scratch_shapes=[pltpu.VMEM((hb, kd, vd), F32)],
        compiler_params=_cparams("parallel", "parallel", "arbitrary"),
        name="hgrn_scan",
    )(proj, proj, proj, proj, lb2, gain2,
      jnp.asarray(np.concatenate([mstack, mstack], 1), BF16), jnp.asarray(level))

    t = st.dec_seq
    bb, _ = _step_blocking(st, st.dec_batch)
    rows = bb * t
    rb0 = st.n_prompt // rows
    sums, shifts = _step_tables(bb, t)
    seg = (np.arange(width)[:, None] // kd == np.arange(LANES)[None, :]).astype(np.float32)
    assert heads <= LANES

    def sseg(jj):
        return pl.BlockSpec((rows, width), lambda i: (rb0 + i, jj))

    full = lambda a: pl.BlockSpec(a.shape, lambda i: (0,) * a.ndim)
    state = pl.BlockSpec((1, bb, heads, kd, vd), lambda i: (j, i, 0, 0, 0))
    args = [proj, proj, proj, proj, lb2, gain2, jnp.asarray(sums, BF16), jnp.asarray(shifts, BF16),
            jnp.asarray(seg, BF16), jnp.asarray(seg.T, BF16), state_all]
    in_specs = [sseg(0), sseg(1), sseg(2), sseg(3), full(lb2), full(gain2), full(sums), full(shifts),
                full(seg), full(seg.T), state]
    aliases = {}
    if so_prev is not None:
        args.append(so_prev)
        in_specs.append(pl.BlockSpec(memory_space=pl.ANY))
        aliases[len(args) - 1] = 1
    o_s, so = pl.pallas_call(
        functools.partial(_hgrn_step_call_kernel, bb, t, heads, kd, len(args)),
        grid=(st.dec_batch // bb,),
        in_specs=in_specs,
        out_specs=[pl.BlockSpec((rows, width), lambda i: (i, 0)), state],
        out_shape=[jax.ShapeDtypeStruct((st.n_sample, heads * vd), BF16),
                   jax.ShapeDtypeStruct(state_all.shape, F32)],
        input_output_aliases=aliases,
        compiler_params=_cparams("parallel"),
        name="hgrn_step",
    )(*args)
    return o_p, s_p, o_s, so


def _rotate(x, cos, sin):
    half = x.shape[-1] // 2
    x1, x2 = x[:, :half], x[:, half:]
    return jnp.concatenate([x1 * cos - x2 * sin, x1 * sin + x2 * cos], axis=-1)


def _group_norm_gate(o, gain, g):
    mu = jnp.mean(o, axis=-1, keepdims=True)
    var = jnp.mean(jnp.square(o - mu), axis=-1, keepdims=True)
    return (_silu(g) * ((o - mu) * lax.rsqrt(var + EPS) * gain)).astype(BF16)


def _ret_kernel(c, dk, dv, lgs, q_ref, k_ref, v_ref, g_ref, cos_ref, sin_ref, gain_ref, o_ref, so_ref,
                s_scr, intra_scr):
    ci = pl.program_id(2)
    t_col = lax.broadcasted_iota(jnp.int32, (c, 1), 0).astype(F32)

    @pl.when(ci == 0)
    def _():
        s_scr[...] = jnp.zeros_like(s_scr)
        rel = (lax.broadcasted_iota(jnp.int32, (c, c), 0)
               - lax.broadcasted_iota(jnp.int32, (c, c), 1)).astype(F32)
        for h, lg in enumerate(lgs):
            intra_scr[h] = jnp.where(rel >= 0.0, jnp.exp(jnp.maximum(rel, 0.0) * lg), 0.0)

    cos, sin = cos_ref[...], sin_ref[...]
    for h, lg in enumerate(lgs):
        q = _rotate(q_ref[:, h * dk:(h + 1) * dk], cos, sin)
        k = _rotate(k_ref[:, h * dk:(h + 1) * dk], cos, sin) * (dk ** -0.5)
        v = v_ref[:, h * dv:(h + 1) * dv].astype(BF16)
        inter = jnp.exp((t_col + 1.0) * lg)
        tail = jnp.exp((c - 1.0 - t_col) * lg)
        whole = jnp.exp(jnp.full((1, 1), c, F32) * lg)
        s = s_scr[h]
        att = _dot_nt(q.astype(BF16), k.astype(BF16)) * intra_scr[h]
        o = _dot(att.astype(BF16), v) + _dot((q * inter).astype(BF16), s.astype(BF16))
        s_scr[h] = whole * s + _dot((k * tail).T.astype(BF16), v)
        o_ref[:, h * dv:(h + 1) * dv] = _group_norm_gate(
            o, gain_ref[:, h * dv:(h + 1) * dv], g_ref[:, h * dv:(h + 1) * dv])

    @pl.when(ci == pl.num_programs(2) - 1)
    def _():
        so_ref[0] = s_scr[...]


def _ret_step_kernel(bb, t, dk, dv, lgs, q_ref, k_ref, v_ref, g_ref, cos_ref, sin_ref, pos_ref, rel_ref,
                     gain_ref, s_ref, o_ref, so_ref):
    rows = bb * t
    cos, sin = cos_ref[...], sin_ref[...]
    t_col = pos_ref[...][:, :1]
    rel = rel_ref[...]
    row_seq = lax.broadcasted_iota(jnp.int32, (rows, dv), 0) // t
    lane_seq = lax.broadcasted_iota(jnp.int32, (dk, rows), 1) // t
    for h, lg in enumerate(lgs):
        q = _rotate(q_ref[:, h * dk:(h + 1) * dk], cos, sin)
        k = _rotate(k_ref[:, h * dk:(h + 1) * dk], cos, sin) * (dk ** -0.5)
        v = v_ref[:, h * dv:(h + 1) * dv].astype(BF16)
        inter = jnp.exp((t_col + 1.0) * lg)
        tail = jnp.exp((t - 1.0 - t_col) * lg)
        intra = jnp.where(rel >= 0.0, jnp.exp(jnp.maximum(rel, 0.0) * lg), 0.0)
        whole = jnp.exp(jnp.full((1, 1), t, F32) * lg)

        att = _dot_nt(q.astype(BF16), k.astype(BF16)) * intra
        o = _dot(att.astype(BF16), v)
        qd = (q * inter).astype(BF16)
        k_t = (k * tail).T
        for b in range(bb):
            s = s_ref[0, b, h]
            o = o + jnp.where(row_seq == b, _dot(qd, s.astype(BF16)), 0.0)
            k_b = jnp.where(lane_seq == b, k_t, 0.0).astype(BF16)
            so_ref[0, b, h] = whole * s + _dot(k_b, v)
        o_ref[:, h * dv:(h + 1) * dv] = _group_norm_gate(
            o, gain_ref[:, h * dv:(h + 1) * dv], g_ref[:, h * dv:(h + 1) * dv])


def _ret_mixer_kernel(c, hb, dk, dv, bb, t, stride, n_in, *refs):
    (lg_ref, qp, kp, vp, gp, cosp, sinp, gain_ref,
     qs, ks, vs, gs, coss, sins, pos_ref, rel_ref, s_ref) = refs[:17]
    o_p, sp_ref, o_s, so_ref, s_scr, intra_scr = refs[n_in:]
    lgs = [lg_ref[pl.program_id(1) * hb + h] for h in range(hb)]
    _ret_kernel(c, dk, dv, lgs, qp, kp, vp, gp, cosp, sinp, gain_ref, o_p, sp_ref, s_scr, intra_scr)

    @pl.when(pl.program_id(2) % stride == 0)
    def _():
        _ret_step_kernel(bb, t, dk, dv, lgs, qs, ks, vs, gs, coss, sins, pos_ref, rel_ref, gain_ref,
                         s_ref, o_s, so_ref)


def _ret_mixer(st, qk, v, g, cos_p, sin_p, cos_s, sin_s, log_gamma, gain, state_all, j, so_prev):
    _, _, heads, dk, dv = state_all.shape
    c = min(RET_CHUNK, st.seq)
    hb = min(RET_HEADS_PER_STEP, heads)
    groups = heads // hb
    n_chunks = st.seq // c
    t = st.dec_seq
    bb, stride = _step_blocking(st, st.batch * n_chunks)
    rows = bb * t
    rb0 = st.n_prompt // rows
    r = np.arange(rows)
    same = (r[:, None] // t) == (r[None, :] // t)
    rel = np.where(same & (r[None, :] <= r[:, None]), r[:, None] - r[None, :], -1).astype(np.float32)
    pos = np.broadcast_to((r % t).astype(np.float32)[:, None], (rows, LANES))
    cos_r, sin_r = jnp.tile(cos_s, (bb, 1)), jnp.tile(sin_s, (bb, 1))

    def sblock(b, ci):
        return (b * n_chunks + ci) // stride

    def pcol(width, off):
        return pl.BlockSpec((c, hb * width), lambda b, h, ci: (b * n_chunks + ci, off + h))

    def scol(width, off):
        return pl.BlockSpec((rows, hb * width), lambda b, h, ci: (rb0 + sblock(b, ci), off + h))

    full = lambda a: pl.BlockSpec(a.shape, lambda b, h, ci: (0,) * a.ndim)
    rope = pl.BlockSpec((c, dk // 2), lambda b, h, ci: (ci, 0))
    state = pl.BlockSpec((1, bb, hb, dk, dv), lambda b, h, ci: (j, sblock(b, ci), h, 0, 0))
    args = [log_gamma, qk, qk, v, g, cos_p, sin_p, gain.reshape(1, heads * dv),
            qk, qk, v, g, cos_r, sin_r, jnp.asarray(pos), jnp.asarray(rel), state_all]
    in_specs = [pl.BlockSpec(memory_space=pltpu.SMEM),
                pcol(dk, 0), pcol(dk, groups), pcol(dv, 0), pcol(dv, 0), rope, rope,
                pl.BlockSpec((1, hb * dv), lambda b, h, ci: (0, h)),
                scol(dk, 0), scol(dk, groups), scol(dv, 0), scol(dv, 0),
                full(cos_r), full(sin_r), full(pos), full(rel), state]
    aliases = {}
    if so_prev is not None:
        args.append(so_prev)
        in_specs.append(pl.BlockSpec(memory_space=pl.ANY))
        aliases[len(args) - 1] = 3
    return pl.pallas_call(
        functools.partial(_ret_mixer_kernel, c, hb, dk, dv, bb, t, stride, len(args)),
        grid=(st.batch, groups, n_chunks),
        in_specs=in_specs,
        out_specs=[pl.BlockSpec((c, hb * dv), lambda b, h, ci: (b * n_chunks + ci, h)),
                   pl.BlockSpec((1, hb, dk, dv), lambda b, h, ci: (b, h, 0, 0)),
                   pl.BlockSpec((rows, hb * dv), lambda b, h, ci: (sblock(b, ci), h)),
                   state],
        out_shape=[jax.ShapeDtypeStruct((st.n_prompt, heads * dv), BF16),
                   jax.ShapeDtypeStruct((st.batch, heads, dk, dv), F32),
                   jax.ShapeDtypeStruct((st.n_sample, heads * dv), BF16),
                   jax.ShapeDtypeStruct(state_all.shape, F32)],
        scratch_shapes=[pltpu.VMEM((hb, dk, dv), F32), pltpu.VMEM((hb, c, c), F32)],
        input_output_aliases=aliases,
        compiler_params=_cparams("arbitrary", "arbitrary", "arbitrary"),
        name="ret_mixer",
    )(*args)


def _rope_tables(pos, half):
    inv = 1.0 / (ROPE_BASE ** jnp.linspace(0.0, 1.0, half, dtype=F32))
    ang = pos[:, None] * inv[None]
    return jnp.cos(ang), jnp.sin(ang)


def _expert_kernel(ts_ref, na_ref, tk_ref, eseq_ref, nk_ref, x_ref, w_ref, wg_hbm, wu_hbm, wd_hbm, o_ref,
                   wgb, wub, wdb, stage_g, stage_u, stage_d, sems):
    t = pl.program_id(0)
    active = t < na_ref[0]
    k = tk_ref[t]
    new_expert = (t == 0) | (k != tk_ref[jnp.maximum(t - 1, 0)])

    def weight_copies(kk, slot):
        e = eseq_ref[kk]
        return (pltpu.make_async_copy(wg_hbm.at[e], stage_g.at[slot], sems.at[0, slot]),
                pltpu.make_async_copy(wu_hbm.at[e], stage_u.at[slot], sems.at[1, slot]),
                pltpu.make_async_copy(wd_hbm.at[e], stage_d.at[slot], sems.at[2, slot]))

    @pl.when(t == 0)
    def _():
        for cp in weight_copies(0, 0):
            cp.start()

    @pl.when(active & new_expert)
    def _():
        slot = lax.rem(k, 2)

        @pl.when(k + 1 < nk_ref[0])
        def _():
            for cp in weight_copies(k + 1, 1 - slot):
                cp.start()

        for cp in weight_copies(k, slot):
            cp.wait()
        wgb[...] = stage_g[slot].astype(BF16)
        wub[...] = stage_u[slot].astype(BF16)
        wdb[...] = stage_d[slot].astype(BF16)

    @pl.when(active)
    def _():
        x = x_ref[...].astype(BF16)
        hg = _silu(_dot(x, wgb[...])) * _dot(x, wub[...])
        o_ref[...] = _dot((hg * w_ref[...]).astype(BF16), wdb[...])

    @pl.when(jnp.logical_not(active))
    def _():
        o_ref[...] = jnp.zeros_like(o_ref)


def _experts(x_sorted, w_sorted, tile_src, n_active, tile_pos, expert_seq, n_used, w_gate, w_up, w_down):
    rows = x_sorted.shape[0]
    _, d, ff = w_gate.shape
    tm = EXPERT_TILE
    hbm = pl.BlockSpec(memory_space=pl.ANY)
    grid_spec = pltpu.PrefetchScalarGridSpec(
        num_scalar_prefetch=5,
        grid=(rows // tm,),
        in_specs=[pl.BlockSpec((tm, d), lambda t, ts, *_: (ts[t], 0)),
                  pl.BlockSpec((tm, 1), lambda t, ts, *_: (ts[t], 0)),
                  hbm, hbm, hbm],
        out_specs=pl.BlockSpec((tm, d), lambda t, *_: (t, 0)),
        scratch_shapes=[pltpu.VMEM((d, ff), BF16), pltpu.VMEM((d, ff), BF16), pltpu.VMEM((ff, d), BF16),
                        pltpu.VMEM((2, d, ff), F32), pltpu.VMEM((2, d, ff), F32), pltpu.VMEM((2, ff, d), F32),
                        pltpu.SemaphoreType.DMA((3, 2))],
    )
    return pl.pallas_call(
        _expert_kernel,
        grid_spec=grid_spec,
        out_shape=jax.ShapeDtypeStruct((rows, d), F32),
        compiler_params=_cparams("arbitrary"),
        name="experts",
    )(tile_src, n_active, tile_pos, expert_seq, n_used, x_sorted, w_sorted, w_gate, w_up, w_down)


def _take_rows(a, idx):
    return a.at[idx].get(mode="promise_in_bounds")


def _step_lookup(table, starts, x):
    jumps = table[1:] - table[:-1]
    return table[0] + jnp.sum(jnp.where(x[None, :] >= starts[1:, None], jumps[:, None], 0), axis=0)


def _dispatch_plan(eid, ew, n_experts_total):
    n, top_k = eid.shape
    tm = EXPERT_TILE
    a = n * top_k
    n_tiles = a // tm + n_experts_total
    rows = n_tiles * tm
    i32 = jnp.int32
    q = jnp.arange(a, dtype=i32)
    sorted_e, order, sorted_w = lax.sort((eid.reshape(a), q, ew.reshape(a)), num_keys=1, is_stable=True)
    bounds = jnp.sum(sorted_e[None, :] < jnp.arange(n_experts_total + 1, dtype=i32)[:, None], axis=1, dtype=i32)
    off, end = bounds[:-1], bounds[1:]
    padded = (end - off + tm - 1) // tm * tm
    pad_end = jnp.cumsum(padded)
    pad_off = pad_end - padded
    delta = pad_off - off
    _, pos = lax.sort((order, q + _step_lookup(delta, off, q)), num_keys=1)
    n_active = (pad_end[-1] // tm).astype(i32)
    tile_src = jnp.minimum(jnp.arange(n_tiles, dtype=i32), n_active - 1)
    experts = jnp.arange(n_experts_total, dtype=i32)
    used = padded > 0
    n_used = jnp.sum(used, dtype=i32)
    expert_seq = jnp.minimum(jnp.sort(jnp.where(used, experts, n_experts_total)), n_experts_total - 1)
    tile_pos = _step_lookup(jnp.cumsum(used, dtype=i32) - 1, pad_off, tile_src * tm)
    p = jnp.arange(rows, dtype=i32)
    src = p - _step_lookup(delta, pad_off, p)
    valid = (src < _step_lookup(end, pad_off, p)) & (p < pad_end[-1])
    src = jnp.clip(src, 0, a - 1)
    row_token = jnp.where(valid, _take_rows(order, src) // top_k, p % n)
    row_w = jnp.where(valid, _take_rows(sorted_w, src), 0.0)
    return (row_token, row_w.reshape(rows, 1), pos.reshape(n, top_k), tile_src, n_active.reshape(1),
            tile_pos, expert_seq, n_used.reshape(1))


def _combine_kernel(n_prompt_tiles, x_ref, ya_ref, yb_ref, gr, gt, shr, sht, scr, sct, o_ref, hn_ref):
    is_s = pl.program_id(0) >= n_prompt_tiles
    x = x_ref[...] + _pick(is_s, gr, gt) * (ya_ref[...] + yb_ref[...])
    o_ref[...] = x
    hn_ref[...] = (_rms(x) * (1.0 + _pick(is_s, scr, sct)) + _pick(is_s, shr, sht)).astype(BF16)


def _combine_final_kernel(n_prompt_tiles, x_ref, ya_ref, yb_ref, gr, gt, fg_ref, op_ref, os_ref):
    is_s = pl.program_id(0) >= n_prompt_tiles
    x = x_ref[...] + _pick(is_s, gr, gt) * (ya_ref[...] + yb_ref[...])
    y = _rms(x) * fg_ref[...]

    @pl.when(jnp.logical_not(is_s))
    def _():
        op_ref[...] = y

    @pl.when(is_s)
    def _():
        os_ref[...] = y


def _combine(st, x, y_pairs, mod_row, mod_tok, layer, j_gate, final_gain, final):
    d = st.d
    tile = pl.BlockSpec((st.tm, d), lambda i: (i, 0))
    n_tiles = st.n_tiles
    common = [tile, tile, pl.BlockSpec((st.tm, d), lambda i: (n_tiles + i, 0)),
              *st.mod_specs(layer, d, lambda i: j_gate)]
    if final:
        return pl.pallas_call(
            functools.partial(_combine_final_kernel, st.n_prompt_tiles),
            grid=(st.n_tiles,),
            in_specs=[*common, pl.BlockSpec((1, d), lambda i: (0, 0))],
            out_specs=[pl.BlockSpec((st.tm, d), lambda i: (jnp.minimum(i, n_tiles - 2), 0)),
                       pl.BlockSpec((st.tm, d), lambda i: (0, 0))],
            out_shape=[jax.ShapeDtypeStruct((st.n_prompt, d), F32), jax.ShapeDtypeStruct((st.n_sample, d), F32)],
            compiler_params=_cparams("arbitrary"),
            name="moe_combine_final",
        )(x, y_pairs, y_pairs, mod_row, mod_tok, final_gain.reshape(1, d)), None
    return pl.pallas_call(
        functools.partial(_combine_kernel, st.n_prompt_tiles),
        grid=(st.n_tiles,),
        in_specs=[*common, *st.mod_specs(layer + 1, d, lambda i: 0), *st.mod_specs(layer + 1, d, lambda i: 1)],
        out_specs=[tile, tile],
        out_shape=[jax.ShapeDtypeStruct((st.n, d), F32), jax.ShapeDtypeStruct((st.n, d), BF16)],
        compiler_params=_cparams("parallel"),
        name="moe_combine",
    )(x, y_pairs, y_pairs, mod_row, mod_tok, mod_row, mod_tok, mod_row, mod_tok)


def kernel(x_prompt, x_sample, state_hgrn, state_ret, c_prompt, c_sample, ada_w, ada_b, hgrn_w_in, hgrn_lb_logits, hgrn_norm_gain, hgrn_w_out, ret_w_in, ret_norm_gain, ret_w_out, moe_router_group, moe_router_expert, moe_w_gate, moe_w_up, moe_w_down, final_norm_gain):
    batch, seq, d = x_prompt.shape
    dec_batch, dec_seq, _ = x_sample.shape
    depth = ada_w.shape[0]
    _, _, a_heads, a_kd, a_vd = state_hgrn.shape
    _, _, b_heads, b_dk, b_dv = state_ret.shape
    _, n_groups, _, n_exp = moe_router_expert.shape
    ff = moe_w_gate.shape[-1]
    n_exp_total = n_groups * n_exp
    st = _Stream(batch, seq, dec_batch, dec_seq, d)

    p = jax.nn.softmax(hgrn_lb_logits.astype(F32), axis=0)
    lb_all = jnp.cumsum(p, axis=0) - p[0:1]
    log_gamma = jnp.log1p(-jnp.exp2(-5.0 - jnp.arange(b_heads, dtype=F32)))
    cos_p, sin_p = _rope_tables(jnp.arange(seq, dtype=F32), b_dk // 2)
    cos_s, sin_s = _rope_tables(jnp.arange(dec_seq, dtype=F32) + PAST_LEN, b_dk // 2)
    w_gate = moe_w_gate.reshape(depth * n_exp_total, d, ff)
    w_up = moe_w_up.reshape(depth * n_exp_total, d, ff)
    w_down = moe_w_down.reshape(depth * n_exp_total, ff, d)

    c_rows = -(-batch // BF16_SUBLANES) * BF16_SUBLANES
    mod_row, mod_tok = _ada_mod(jnp.pad(c_prompt, ((0, c_rows - batch), (0, 0))),
                                jnp.repeat(c_sample, dec_seq, axis=0), ada_w, ada_b)
    mod_row = mod_row[:, :batch].reshape(depth * batch, 1, 6 * d)

    x = jnp.concatenate([x_prompt.reshape(-1, d), x_sample.reshape(-1, d)], 0)
    hg_p, rt_p, hg_s, rt_s = [], [], None, None
    hn = _norm_mod(st, x, mod_row, mod_tok, 0, 0, 1)
    for l in range(depth):
        j = l // 2
        if l % 2 == 0:
            proj = _proj(st, hn, hgrn_w_in, j)
            o_p, s_p, o_s, hg_s = _hgrn_mixer(st, proj, lb_all[j], hgrn_norm_gain[j], state_hgrn, j, hg_s)
            hg_p.append(s_p)
            w_out = hgrn_w_out
        else:
            n_qk, n_v = 2 * b_heads * b_dk, b_heads * b_dv
            qk = _proj(st, hn, ret_w_in, j, 0, n_qk)
            v = _proj(st, hn, ret_w_in, j, n_qk, n_v, BF16)
            g = _proj(st, hn, ret_w_in, j, n_qk + n_v, n_v)
            o_p, s_p, o_s, rt_s = _ret_mixer(st, qk, v, g, cos_p, sin_p, cos_s, sin_s, log_gamma,
                                             ret_norm_gain[j], state_ret, j, rt_s)
            rt_p.append(s_p)
            w_out = ret_w_out
        x = _proj_residual(st, o_p, o_s, w_out, j, x, mod_row, mod_tok, l, 2)

        w_router = jnp.concatenate(
            [moe_router_group[l], jnp.moveaxis(moe_router_expert[l], 0, 1).reshape(d, n_exp_total)], 1)
        w_router = jnp.pad(w_router, ((0, 0), (0, LANES - w_router.shape[1])))
        hx, eid, ew = _norm_mod_router(st, x, mod_row, mod_tok, l, 3, 4, w_router, n_groups, n_exp)
        row_token, row_w, pos, tile_src, n_active, tile_pos, expert_seq, n_used = _dispatch_plan(
            eid, ew, n_exp_total)
        y_sorted = _experts(_take_rows(hx, row_token), row_w, tile_src, n_active, tile_pos,
                            expert_seq + l * n_exp_total, n_used, w_gate, w_up, w_down)
        y_pairs = _take_rows(y_sorted, pos.T.reshape(-1))
        x, hn = _combine(st, x, y_pairs, mod_row, mod_tok, l, 5, final_norm_gain, l == depth - 1)

    y_prompt = x[0].reshape(batch, seq, d)
    y_sample = x[1].reshape(dec_batch, dec_seq, d)
    return (y_prompt, y_sample, jnp.stack(hg_p), jnp.stack(rt_p), hg_s, rt_s)
```

```python
import functools
import math

import numpy as np
import jax
import jax.numpy as jnp
from jax import lax
from jax.experimental import pallas as pl
from jax.experimental.pallas import tpu as pltpu

F32 = jnp.float32
BF16 = jnp.bfloat16
EPS = 1e-6
PAST_LEN = 16384
ROPE_BASE = 10000.0

V7X_VMEM_LIMIT_BYTES = 56 * 1024 * 1024
LANES = 128
BF16_SUBLANES = 16

HGRN_CHUNK = 128
HGRN_HEADS_PER_STEP = 16
RET_CHUNK = 256
RET_HEADS_PER_STEP = 4
EXPERT_TILE = 256
PROJ_TN = 1024
PROJ_IN_ROW_TILES = 8
ADA_TN = 1024


def _cparams(*sem):
    return pltpu.CompilerParams(dimension_semantics=sem, vmem_limit_bytes=V7X_VMEM_LIMIT_BYTES)


def _dot(a, b):
    return jnp.dot(a, b, preferred_element_type=F32)


def _dot_nt(a, b):
    return lax.dot_general(a, b, (((1,), (1,)), ((), ())), preferred_element_type=F32)


def _dot2(m, x):
    hi = x.astype(BF16)
    lo = (x - hi.astype(F32)).astype(BF16)
    return _dot(m, hi) + _dot(m, lo)


def _dot2_wide(mm, x):
    hi = x.astype(BF16)
    lo = (x - hi.astype(F32)).astype(BF16)
    return _dot(mm, jnp.concatenate([hi, lo], axis=0))


def _dot2_lhs(x, m):
    hi = x.astype(BF16)
    lo = (x - hi.astype(F32)).astype(BF16)
    return _dot(hi, m) + _dot(lo, m)


def _silu(x):
    return x * jax.nn.sigmoid(x)


def _rms(x):
    return x * lax.rsqrt(jnp.mean(x * x, axis=-1, keepdims=True) + EPS)


def _ada_kernel(cr_ref, ct_ref, w_ref, b_ref, or_ref, ot_ref):
    w = w_ref[0].astype(BF16)
    or_ref[0] = _dot(_silu(cr_ref[...]).astype(BF16), w) + b_ref[0]
    ot_ref[0] = _dot(_silu(ct_ref[...]).astype(BF16), w) + b_ref[0]


def _ada_mod(c_row, c_tok, ada_w, ada_b):
    depth, d, n6 = ada_w.shape
    r, t = c_row.shape[0], c_tok.shape[0]
    tn = min(ADA_TN, n6)
    return pl.pallas_call(
        _ada_kernel,
        grid=(depth, n6 // tn),
        in_specs=[pl.BlockSpec((r, d), lambda l, n: (0, 0)),
                  pl.BlockSpec((t, d), lambda l, n: (0, 0)),
                  pl.BlockSpec((1, d, tn), lambda l, n: (l, 0, n)),
                  pl.BlockSpec((1, 1, tn), lambda l, n: (l, 0, n))],
        out_specs=[pl.BlockSpec((1, r, tn), lambda l, n: (l, 0, n)),
                   pl.BlockSpec((1, t, tn), lambda l, n: (l, 0, n))],
        out_shape=[jax.ShapeDtypeStruct((depth, r, n6), F32), jax.ShapeDtypeStruct((depth, t, n6), F32)],
        compiler_params=_cparams("parallel", "parallel"),
        name="ada_mod",
    )(c_row, c_tok, ada_w, ada_b.reshape(depth, 1, n6))


class _Stream:
    def __init__(self, batch, seq, dec_batch, dec_seq, d):
        self.batch, self.seq, self.dec_batch, self.dec_seq, self.d = batch, seq, dec_batch, dec_seq, d
        self.n_prompt = batch * seq
        self.n_sample = dec_batch * dec_seq
        self.tm = self.n_sample
        assert seq % self.tm == 0 and self.tm % BF16_SUBLANES == 0
        self.tiles_per_seq = seq // self.tm
        self.n_prompt_tiles = self.n_prompt // self.tm
        self.n_tiles = self.n_prompt_tiles + 1
        self.n = self.n_prompt + self.n_sample

    def mod_specs(self, layer, width, col_block, tile_axis=0):
        tps, last, batch = self.tiles_per_seq, self.batch - 1, self.batch

        def row_map(*idx):
            return (layer * batch + jnp.minimum(idx[tile_axis] // tps, last), 0, col_block(*idx))
        return (pl.BlockSpec((1, 1, width), row_map),
                pl.BlockSpec((1, self.tm, width), lambda *idx: (layer, 0, col_block(*idx))))


def _pick(is_sample, row_ref, tok_ref):
    return jnp.where(is_sample, tok_ref[0], row_ref[0])


def _norm_mod_kernel(n_prompt_tiles, x_ref, shr, sht, scr, sct, o_ref):
    is_s = pl.program_id(0) >= n_prompt_tiles
    h = _rms(x_ref[...]) * (1.0 + _pick(is_s, scr, sct)) + _pick(is_s, shr, sht)
    o_ref[...] = h.astype(BF16)


def _norm_mod(st, x, mod_row, mod_tok, layer, j_shift, j_scale):
    d = st.d
    return pl.pallas_call(
        functools.partial(_norm_mod_kernel, st.n_prompt_tiles),
        grid=(st.n_tiles,),
        in_specs=[pl.BlockSpec((st.tm, d), lambda i: (i, 0)),
                  *st.mod_specs(layer, d, lambda i: j_shift), *st.mod_specs(layer, d, lambda i: j_scale)],
        out_specs=pl.BlockSpec((st.tm, d), lambda i: (i, 0)),
        out_shape=jax.ShapeDtypeStruct((st.n, d), BF16),
        compiler_params=_cparams("parallel"),
        name="norm_mod",
    )(x, mod_row, mod_tok, mod_row, mod_tok)


def _norm_mod_router_kernel(n_prompt_tiles, n_groups, n_experts, x_ref, shr, sht, scr, sct, wr_ref,
                            o_ref, eid_ref, ew_ref):
    is_s = pl.program_id(0) >= n_prompt_tiles
    h = _rms(x_ref[...]) * (1.0 + _pick(is_s, scr, sct)) + _pick(is_s, shr, sht)
    o_ref[...] = h

    logits = _dot(h.astype(BF16), wr_ref[...].astype(BF16))
    tm = logits.shape[0]
    lane = lax.broadcasted_iota(jnp.int32, (tm, LANES), 1)
    neg_inf = jnp.float32(-jnp.inf)

    def first_index_of_max(vals, vmax):
        return jnp.min(jnp.where(vals == vmax, lane, LANES), axis=1, keepdims=True)

    is_g = lane < n_groups
    lg = jnp.where(is_g, logits, neg_inf)
    mg = jnp.max(lg, axis=1, keepdims=True)
    eg = jnp.exp(lg - mg)
    pg = eg / jnp.sum(eg, axis=1, keepdims=True)
    gi = first_index_of_max(lg, mg)
    wg = jnp.sum(jnp.where(lane == gi, pg, 0.0), axis=1, keepdims=True)

    lo = n_groups + gi * n_experts
    sel = (lane >= lo) & (lane < lo + n_experts)
    le = jnp.where(sel, logits, neg_inf)
    me = jnp.max(le, axis=1, keepdims=True)
    ee = jnp.exp(le - me)
    pe = jnp.where(sel, ee / jnp.sum(ee, axis=1, keepdims=True), -1.0)
    p1 = jnp.max(pe, axis=1, keepdims=True)
    i1 = first_index_of_max(pe, p1)
    pe2 = jnp.where(lane == i1, -1.0, pe)
    p2 = jnp.max(pe2, axis=1, keepdims=True)
    i2 = first_index_of_max(pe2, p2)
    denom = p1 + p2
    col = lax.broadcasted_iota(jnp.int32, (tm, 2), 1)
    eid_ref[...] = jnp.where(col == 0, i1, i2) - n_groups
    ew_ref[...] = wg * jnp.where(col == 0, p1 / denom, p2 / denom)


def _norm_mod_router(st, x, mod_row, mod_tok, layer, j_shift, j_scale, w_router, n_groups, n_experts):
    d = st.d
    return pl.pallas_call(
        functools.partial(_norm_mod_router_kernel, st.n_prompt_tiles, n_groups, n_experts),
        grid=(st.n_tiles,),
        in_specs=[pl.BlockSpec((st.tm, d), lambda i: (i, 0)),
                  *st.mod_specs(layer, d, lambda i: j_shift), *st.mod_specs(layer, d, lambda i: j_scale),
                  pl.BlockSpec((d, LANES), lambda i: (0, 0))],
        out_specs=[pl.BlockSpec((st.tm, d), lambda i: (i, 0)),
                   pl.BlockSpec((st.tm, 2), lambda i: (i, 0)),
                   pl.BlockSpec((st.tm, 2), lambda i: (i, 0))],
        out_shape=[jax.ShapeDtypeStruct((st.n, d), F32),
                   jax.ShapeDtypeStruct((st.n, 2), jnp.int32),
                   jax.ShapeDtypeStruct((st.n, 2), F32)],
        compiler_params=_cparams("parallel"),
        name="norm_mod_router",
    )(x, mod_row, mod_tok, mod_row, mod_tok, w_router)


def _proj_kernel(a_ref, w_ref, o_ref, wb_ref):
    @pl.when(pl.program_id(1) == 0)
    def _():
        wb_ref[...] = w_ref[0].astype(BF16)
    o_ref[...] = _dot(a_ref[...], wb_ref[...]).astype(o_ref.dtype)


def _proj(st, a, w_all, j, col0=0, n_out=None, dtype=F32):
    _, k, n_all = w_all.shape
    n_out = n_all if n_out is None else n_out
    tn = min(PROJ_TN, n_out)
    assert col0 % tn == 0 and n_out % tn == 0
    nb0 = col0 // tn
    tm = st.n // PROJ_IN_ROW_TILES if st.n % (PROJ_IN_ROW_TILES * BF16_SUBLANES) == 0 else st.tm
    return pl.pallas_call(
        _proj_kernel,
        grid=(n_out // tn, st.n // tm),
        in_specs=[pl.BlockSpec((tm, k), lambda n, m: (m, 0)),
                  pl.BlockSpec((1, k, tn), lambda n, m: (j, 0, nb0 + n))],
        out_specs=pl.BlockSpec((tm, tn), lambda n, m: (m, n)),
        out_shape=jax.ShapeDtypeStruct((st.n, n_out), dtype),
        scratch_shapes=[pltpu.VMEM((k, tn), BF16)],
        compiler_params=_cparams("parallel", "arbitrary"),
        name="proj_in",
    )(a, w_all)


def _proj_res_kernel(n_prompt_tiles, ap_ref, as_ref, w_ref, x_ref, gr, gt, o_ref, wb_ref):
    @pl.when(pl.program_id(1) == 0)
    def _():
        wb_ref[...] = w_ref[0].astype(BF16)
    is_s = pl.program_id(1) >= n_prompt_tiles
    a = jnp.where(is_s, as_ref[...], ap_ref[...])
    o_ref[...] = x_ref[...] + _pick(is_s, gr, gt) * _dot(a, wb_ref[...])


def _proj_residual(st, a_prompt, a_sample, w_all, j, x, mod_row, mod_tok, layer, j_gate):
    _, k, d = w_all.shape
    tn = min(PROJ_TN, d)
    nb = d // tn
    last_prompt = st.n_prompt_tiles - 1
    return pl.pallas_call(
        functools.partial(_proj_res_kernel, st.n_prompt_tiles),
        grid=(nb, st.n_tiles),
        in_specs=[pl.BlockSpec((st.tm, k), lambda n, m: (jnp.minimum(m, last_prompt), 0)),
                  pl.BlockSpec((st.tm, k), lambda n, m: (0, 0)),
                  pl.BlockSpec((1, k, tn), lambda n, m: (j, 0, n), pipeline_mode=pl.Buffered(1)),
                  pl.BlockSpec((st.tm, tn), lambda n, m: (m, n)),
                  *st.mod_specs(layer, tn, lambda n, m: j_gate * nb + n, tile_axis=1)],
        out_specs=pl.BlockSpec((st.tm, tn), lambda n, m: (m, n)),
        out_shape=jax.ShapeDtypeStruct((st.n, d), F32),
        scratch_shapes=[pltpu.VMEM((k, tn), BF16)],
        compiler_params=_cparams("parallel", "arbitrary"),
        name="proj_out",
    )(a_prompt, a_sample, w_all, x, mod_row, mod_tok)


def _hgrn_tables(c):
    nbits = int(math.log2(c))
    assert 1 << nbits == c
    r = np.arange(c)[:, None]
    j = np.arange(c)[None, :]
    mats = [(j <= r), (j > r)]
    for b in range(nbits):
        mid = ((r >> (b + 1)) << (b + 1)) + (1 << b)
        right = r >= mid
        mats.append(np.where(right, (j >= mid) & (j <= r), (j > r) & (j < mid)))
    level = np.full((c, c), -1, np.int32)
    level[r[:, 0], r[:, 0]] = 0
    diff = r ^ j
    for b in range(nbits):
        level[(j < r) & ((diff >> b) == 1)] = b + 1
    return np.concatenate(mats, 0).astype(np.float32), level, nbits


def _hgrn_kernel(c, nbits, hb, kd, q_ref, z_ref, i_ref, g_ref, lb_ref, gain_ref, ms_ref, lv_ref,
                 o_ref, so_ref, s_scr):
    ci = pl.program_id(2)

    @pl.when(ci == 0)
    def _():
        s_scr[...] = jnp.zeros_like(s_scr)

    lb = lb_ref[...]
    f = lb + (1.0 - lb) * jax.nn.sigmoid(z_ref[...])
    logf = jnp.log(f)
    k = 1.0 - f
    q = _silu(q_ref[...])
    v = i_ref[...].astype(BF16)
    e = _dot2_wide(ms_ref[...], logf)
    cum = e[0:c]
    qd = (q * jnp.exp(cum)).astype(BF16)
    kdec = k * jnp.exp(e[c:2 * c])
    whole_row = cum[c - 1:c]
    lv = lv_ref[...]
    masks = [lv == b for b in range(nbits + 1)]
    qb, kb = q.astype(BF16), k.astype(BF16)
    levels = []
    for b in range(nbits):
        gl = jnp.exp(e[(2 + b) * c:(3 + b) * c]).astype(BF16)
        levels.append((qb * gl, kb * gl))
    gate = gain_ref[...] * _silu(g_ref[...])
    qk_diag = qb.astype(F32) * kb.astype(F32)

    for h in range(hb):
        sl = slice(h * kd, (h + 1) * kd)
        s = s_scr[h]
        diag = jnp.sum(qk_diag[:, sl], axis=-1, keepdims=True).astype(BF16).astype(F32)
        o = _dot(qd[:, sl], s.astype(BF16)) + diag * v[:, sl].astype(F32)
        att = jnp.zeros((c, c), F32)
        for mask, (ql, kl) in zip(masks[1:], levels):
            att = jnp.where(mask, _dot_nt(ql[:, sl], kl[:, sl]), att)
        o = o + _dot(att.astype(BF16), v[:, sl])
        whole = jnp.exp(jnp.broadcast_to(whole_row[:, sl], (s.shape[1], s.shape[0])).T)
        s_scr[h] = whole * s + _dot(kdec[:, sl].T.astype(BF16), v[:, sl])
        o = o * lax.rsqrt(jnp.mean(o * o, axis=-1, keepdims=True) + EPS)
        o_ref[:, sl] = (o * gate[:, sl]).astype(BF16)

    @pl.when(ci == pl.num_programs(2) - 1)
    def _():
        so_ref[0] = s_scr[...]


def _step_tables(bb, t):
    rows = bb * t
    r = np.arange(rows)
    grp, pos = r // t, r % t
    same = grp[:, None] == grp[None, :]
    i, j = r[:, None], r[None, :]
    sums = [same & (j <= i), same & (j > i), same]
    for d in range(1, t):
        sums.append(same & (j <= i) & (j > i - d) & (pos[:, None] >= d))
    shifts = [same & (j == i - d) for d in range(1, t)]
    return (np.concatenate(sums, 0).astype(np.float32), np.concatenate(shifts, 0).astype(np.float32))


def _hgrn_step_kernel(bb, t, heads, kd, q_ref, z_ref, i_ref, g_ref, lb_ref, gain_ref, ms_ref, sh_ref,
                      seg_ref, segt_ref, s_ref, o_ref, so_ref):
    rows = bb * t
    lb = lb_ref[...]
    f = lb + (1.0 - lb) * jax.nn.sigmoid(z_ref[...])
    logf = jnp.log(f)
    k = 1.0 - f
    q = _silu(q_ref[...])
    v = i_ref[...]
    e = _dot2(ms_ref[...], logf)
    qd = (q * jnp.exp(e[0:rows])).astype(BF16)
    kdec = k * jnp.exp(e[rows:2 * rows])
    whole = jnp.exp(e[2 * rows:3 * rows])

    seg, segt = seg_ref[...], segt_ref[...]
    o = jnp.zeros_like(q)
    for d in range(t):
        if d == 0:
            ks, vs, w = k, v, q * k
        else:
            sh = sh_ref[(d - 1) * rows:d * rows]
            ks, vs = _dot2(sh, k), _dot2(sh, v)
            w = q * ks * jnp.exp(e[(2 + d) * rows:(3 + d) * rows])
        att = _dot2_lhs(w, seg)
        o = o + _dot2_lhs(att, segt) * vs

    gate = gain_ref[...] * _silu(g_ref[...])
    vb = v.astype(BF16)
    row_seq = lax.broadcasted_iota(jnp.int32, (rows, kd), 0) // t
    lane_seq = lax.broadcasted_iota(jnp.int32, (kd, rows), 1) // t
    for h in range(heads):
        sl = slice(h * kd, (h + 1) * kd)
        k_t = kdec[:, sl].T
        w_t = whole[:, sl].T
        o_h = o[:, sl]
        for b in range(bb):
            s = s_ref[0, b, h]
            o_h = o_h + jnp.where(row_seq == b, _dot(qd[:, sl], s.astype(BF16)), 0.0)
            k_b = jnp.where(lane_seq == b, k_t, 0.0).astype(BF16)
            so_ref[0, b, h] = w_t[:, b * t:b * t + 1] * s + _dot(k_b, vb[:, sl])
        o_h = o_h * lax.rsqrt(jnp.mean(o_h * o_h, axis=-1, keepdims=True) + EPS)
        o_ref[:, sl] = (o_h * gate[:, sl]).astype(BF16)


def _step_blocking(st, scan_steps):
    t = st.dec_seq
    bb = BF16_SUBLANES // math.gcd(BF16_SUBLANES, t)
    n_blocks = st.dec_batch // bb
    if n_blocks > scan_steps:
        assert n_blocks % scan_steps == 0
        bb, n_blocks = bb * (n_blocks // scan_steps), scan_steps
    assert st.dec_batch == bb * n_blocks and scan_steps % n_blocks == 0 and st.n_prompt % (bb * t) == 0
    return bb, scan_steps // n_blocks


def _hgrn_step_call_kernel(bb, t, heads, kd, n_in, *refs):
    _hgrn_step_kernel(bb, t, heads, kd, *refs[:11], *refs[n_in:])


def _hgrn_mixer(st, proj, lb, gain, state_all, j, so_prev):
    _, _, heads, kd, vd = state_all.shape
    width = heads * kd
    c = min(HGRN_CHUNK, st.seq)
    hb = min(HGRN_HEADS_PER_STEP, heads)
    n_chunks = st.seq // c
    groups = heads // hb
    mstack, level, nbits = _hgrn_tables(c)
    lb2, gain2 = lb.reshape(1, width), gain.reshape(1, width)

    def pseg(jj):
        return pl.BlockSpec((c, hb * kd), lambda b, h, ci: (b * n_chunks + ci, jj * groups + h))

    head_row = pl.BlockSpec((1, hb * kd), lambda b, h, ci: (0, h))
    o_p, s_p = pl.pallas_call(
        functools.partial(_hgrn_kernel, c, nbits, hb, kd),
        grid=(st.batch, groups, n_chunks),
        in_specs=[pseg(0), pseg(1), pseg(2), pseg(3), head_row, head_row,
                  pl.BlockSpec((mstack.shape[0], 2 * c), lambda b, h, ci: (0, 0)),
                  pl.BlockSpec(level.shape, lambda b, h, ci: (0, 0))],
        out_specs=[pl.BlockSpec((c, hb * vd), lambda b, h, ci: (b * n_chunks + ci, h)),
                   pl.BlockSpec((1, hb, kd, vd), lambda b, h, ci: (b, h, 0, 0))],
        out_shape=[jax.ShapeDtypeStruct((st.n_prompt, heads * vd), BF16),
                   jax.ShapeDtypeStruct((st.batch, heads, kd, vd), F32)],
        scratch_shapes=[pltpu.VMEM((hb, kd, vd), F32)],
        compiler_params=_cparams("parallel", "parallel", "arbitrary"),
        name="hgrn_scan",
    )(proj, proj, proj, proj, lb2, gain2,
      jnp.asarray(np.concatenate([mstack, mstack], 1), BF16), jnp.asarray(level))

    t = st.dec_seq
    bb, _ = _step_blocking(st, st.dec_batch)
    rows = bb * t
    rb0 = st.n_prompt // rows
    sums, shifts = _step_tables(bb, t)
    seg = (np.arange(width)[:, None] // kd == np.arange(LANES)[None, :]).astype(np.float32)
    assert heads <= LANES

    def sseg(jj):
        return pl.BlockSpec((rows, width), lambda i: (rb0 + i, jj))

    full = lambda a: pl.BlockSpec(a.shape, lambda i: (0,) * a.ndim)
    state = pl.BlockSpec((1, bb, heads, kd, vd), lambda i: (j, i, 0, 0, 0))
    args = [proj, proj, proj, proj, lb2, gain2, jnp.asarray(sums, BF16), jnp.asarray(shifts, BF16),
            jnp.asarray(seg, BF16), jnp.asarray(seg.T, BF16), state_all]
    in_specs = [sseg(0), sseg(1), sseg(2), sseg(3), full(lb2), full(gain2), full(sums), full(shifts),
                full(seg), full(seg.T), state]
    aliases = {}
    if so_prev is not None:
        args.append(so_prev)
        in_specs.append(pl.BlockSpec(memory_space=pl.ANY))
        aliases[len(args) - 1] = 1
    o_s, so = pl.pallas_call(
        functools.partial(_hgrn_step_call_kernel, bb, t, heads, kd, len(args)),
        grid=(st.dec_batch // bb,),
        in_specs=in_specs,
        out_specs=[pl.BlockSpec((rows, width), lambda i: (i, 0)), state],
        out_shape=[jax.ShapeDtypeStruct((st.n_sample, heads * vd), BF16),
                   jax.ShapeDtypeStruct(state_all.shape, F32)],
        input_output_aliases=aliases,
        compiler_params=_cparams("parallel"),
        name="hgrn_step",
    )(*args)
    return o_p, s_p, o_s, so


def _rotate(x, cos, sin):
    half = x.shape[-1] // 2
    x1, x2 = x[:, :half], x[:, half:]
    return jnp.concatenate([x1 * cos - x2 * sin, x1 * sin + x2 * cos], axis=-1)


def _group_norm_gate(o, gain, g):
    mu = jnp.mean(o, axis=-1, keepdims=True)
    var = jnp.mean(jnp.square(o - mu), axis=-1, keepdims=True)
    return (_silu(g) * ((o - mu) * lax.rsqrt(var + EPS) * gain)).astype(BF16)


def _ret_kernel(c, dk, dv, lgs, q_ref, k_ref, v_ref, g_ref, cos_ref, sin_ref, gain_ref, o_ref, so_ref,
                s_scr, intra_scr):
    ci = pl.program_id(2)
    t_col = lax.broadcasted_iota(jnp.int32, (c, 1), 0).astype(F32)

    @pl.when(ci == 0)
    def _():
        s_scr[...] = jnp.zeros_like(s_scr)
        rel = (lax.broadcasted_iota(jnp.int32, (c, c), 0)
               - lax.broadcasted_iota(jnp.int32, (c, c), 1)).astype(F32)
        for h, lg in enumerate(lgs):
            intra_scr[h] = jnp.where(rel >= 0.0, jnp.exp(jnp.maximum(rel, 0.0) * lg), 0.0)

    cos, sin = cos_ref[...], sin_ref[...]
    for h, lg in enumerate(lgs):
        q = _rotate(q_ref[:, h * dk:(h + 1) * dk], cos, sin)
        k = _rotate(k_ref[:, h * dk:(h + 1) * dk], cos, sin) * (dk ** -0.5)
        v = v_ref[:, h * dv:(h + 1) * dv].astype(BF16)
        inter = jnp.exp((t_col + 1.0) * lg)
        tail = jnp.exp((c - 1.0 - t_col) * lg)
        whole = jnp.exp(jnp.full((1, 1), c, F32) * lg)
        s = s_scr[h]
        att = _dot_nt(q.astype(BF16), k.astype(BF16)) * intra_scr[h]
        o = _dot(att.astype(BF16), v) + _dot((q * inter).astype(BF16), s.astype(BF16))
        s_scr[h] = whole * s + _dot((k * tail).T.astype(BF16), v)
        o_ref[:, h * dv:(h + 1) * dv] = _group_norm_gate(
            o, gain_ref[:, h * dv:(h + 1) * dv], g_ref[:, h * dv:(h + 1) * dv])

    @pl.when(ci == pl.num_programs(2) - 1)
    def _():
        so_ref[0] = s_scr[...]


def _ret_step_kernel(bb, t, dk, dv, lgs, q_ref, k_ref, v_ref, g_ref, cos_ref, sin_ref, pos_ref, rel_ref,
                     gain_ref, s_ref, o_ref, so_ref):
    rows = bb * t
    cos, sin = cos_ref[...], sin_ref[...]
    t_col = pos_ref[...][:, :1]
    rel = rel_ref[...]
    row_seq = lax.broadcasted_iota(jnp.int32, (rows, dv), 0) // t
    lane_seq = lax.broadcasted_iota(jnp.int32, (dk, rows), 1) // t
    for h, lg in enumerate(lgs):
        q = _rotate(q_ref[:, h * dk:(h + 1) * dk], cos, sin)
        k = _rotate(k_ref[:, h * dk:(h + 1) * dk], cos, sin) * (dk ** -0.5)
        v = v_ref[:, h * dv:(h + 1) * dv].astype(BF16)
        inter = jnp.exp((t_col + 1.0) * lg)
        tail = jnp.exp((t - 1.0 - t_col) * lg)
        intra = jnp.where(rel >= 0.0, jnp.exp(jnp.maximum(rel, 0.0) * lg), 0.0)
        whole = jnp.exp(jnp.full((1, 1), t, F32) * lg)

        att = _dot_nt(q.astype(BF16), k.astype(BF16)) * intra
        o = _dot(att.astype(BF16), v)
        qd = (q * inter).astype(BF16)
        k_t = (k * tail).T
        for b in range(bb):
            s = s_ref[0, b, h]
            o = o + jnp.where(row_seq == b, _dot(qd, s.astype(BF16)), 0.0)
            k_b = jnp.where(lane_seq == b, k_t, 0.0).astype(BF16)
            so_ref[0, b, h] = whole * s + _dot(k_b, v)
        o_ref[:, h * dv:(h + 1) * dv] = _group_norm_gate(
            o, gain_ref[:, h * dv:(h + 1) * dv], g_ref[:, h * dv:(h + 1) * dv])


def _ret_mixer_kernel(c, hb, dk, dv, bb, t, stride, n_in, *refs):
    (lg_ref, qp, kp, vp, gp, cosp, sinp, gain_ref,
     qs, ks, vs, gs, coss, sins, pos_ref, rel_ref, s_ref) = refs[:17]
    o_p, sp_ref, o_s, so_ref, s_scr, intra_scr = refs[n_in:]
    lgs = [lg_ref[pl.program_id(1) * hb + h] for h in range(hb)]
    _ret_kernel(c, dk, dv, lgs, qp, kp, vp, gp, cosp, sinp, gain_ref, o_p, sp_ref, s_scr, intra_scr)

    @pl.when(pl.program_id(2) % stride == 0)
    def _():
        _ret_step_kernel(bb, t, dk, dv, lgs, qs, ks, vs, gs, coss, sins, pos_ref, rel_ref, gain_ref,
                         s_ref, o_s, so_ref)


def _ret_mixer(st, qk, v, g, cos_p, sin_p, cos_s, sin_s, log_gamma, gain, state_all, j, so_prev):
    _, _, heads, dk, dv = state_all.shape
    c = min(RET_CHUNK, st.seq)
    hb = min(RET_HEADS_PER_STEP, heads)
    groups = heads // hb
    n_chunks = st.seq // c
    t = st.dec_seq
    bb, stride = _step_blocking(st, st.batch * n_chunks)
    rows = bb * t
    rb0 = st.n_prompt // rows
    r = np.arange(rows)
    same = (r[:, None] // t) == (r[None, :] // t)
    rel = np.where(same & (r[None, :] <= r[:, None]), r[:, None] - r[None, :], -1).astype(np.float32)
    pos = np.broadcast_to((r % t).astype(np.float32)[:, None], (rows, LANES))
    cos_r, sin_r = jnp.tile(cos_s, (bb, 1)), jnp.tile(sin_s, (bb, 1))

    def sblock(b, ci):
        return (b * n_chunks + ci) // stride

    def pcol(width, off):
        return pl.BlockSpec((c, hb * width), lambda b, h, ci: (b * n_chunks + ci, off + h))

    def scol(width, off):
        return pl.BlockSpec((rows, hb * width), lambda b, h, ci: (rb0 + sblock(b, ci), off + h))

    full = lambda a: pl.BlockSpec(a.shape, lambda b, h, ci: (0,) * a.ndim)
    rope = pl.BlockSpec((c, dk // 2), lambda b, h, ci: (ci, 0))
    state = pl.BlockSpec((1, bb, hb, dk, dv), lambda b, h, ci: (j, sblock(b, ci), h, 0, 0))
    args = [log_gamma, qk, qk, v, g, cos_p, sin_p, gain.reshape(1, heads * dv),
            qk, qk, v, g, cos_r, sin_r, jnp.asarray(pos), jnp.asarray(rel), state_all]
    in_specs = [pl.BlockSpec(memory_space=pltpu.SMEM),
                pcol(dk, 0), pcol(dk, groups), pcol(dv, 0), pcol(dv, 0), rope, rope,
                pl.BlockSpec((1, hb * dv), lambda b, h, ci: (0, h)),
                scol(dk, 0), scol(dk, groups), scol(dv, 0), scol(dv, 0),
                full(cos_r), full(sin_r), full(pos), full(rel), state]
    aliases = {}
    if so_prev is not None:
        args.append(so_prev)
        in_specs.append(pl.BlockSpec(memory_space=pl.ANY))
        aliases[len(args) - 1] = 3
    return pl.pallas_call(
        functools.partial(_ret_mixer_kernel, c, hb, dk, dv, bb, t, stride, len(args)),
        grid=(st.batch, groups, n_chunks),
        in_specs=in_specs,
        out_specs=[pl.BlockSpec((c, hb * dv), lambda b, h, ci: (b * n_chunks + ci, h)),
                   pl.BlockSpec((1, hb, dk, dv), lambda b, h, ci: (b, h, 0, 0)),
                   pl.BlockSpec((rows, hb * dv), lambda b, h, ci: (sblock(b, ci), h)),
                   state],
        out_shape=[jax.ShapeDtypeStruct((st.n_prompt, heads * dv), BF16),
                   jax.ShapeDtypeStruct((st.batch, heads, dk, dv), F32),
                   jax.ShapeDtypeStruct((st.n_sample, heads * dv), BF16),
                   jax.ShapeDtypeStruct(state_all.shape, F32)],
        scratch_shapes=[pltpu.VMEM((hb, dk, dv), F32), pltpu.VMEM((hb, c, c), F32)],
        input_output_aliases=aliases,
        compiler_params=_cparams("arbitrary", "arbitrary", "arbitrary"),
        name="ret_mixer",
    )(*args)


def _rope_tables(pos, half):
    inv = 1.0 / (ROPE_BASE ** jnp.linspace(0.0, 1.0, half, dtype=F32))
    ang = pos[:, None] * inv[None]
    return jnp.cos(ang), jnp.sin(ang)


def _expert_kernel(ts_ref, na_ref, tk_ref, eseq_ref, nk_ref, x_ref, w_ref, wg_hbm, wu_hbm, wd_hbm, o_ref,
                   wgb, wub, wdb, stage_g, stage_u, stage_d, sems):
    t = pl.program_id(0)
    active = t < na_ref[0]
    k = tk_ref[t]
    new_expert = (t == 0) | (k != tk_ref[jnp.maximum(t - 1, 0)])

    def weight_copies(kk, slot):
        e = eseq_ref[kk]
        return (pltpu.make_async_copy(wg_hbm.at[e], stage_g.at[slot], sems.at[0, slot]),
                pltpu.make_async_copy(wu_hbm.at[e], stage_u.at[slot], sems.at[1, slot]),
                pltpu.make_async_copy(wd_hbm.at[e], stage_d.at[slot], sems.at[2, slot]))

    @pl.when(t == 0)
    def _():
        for cp in weight_copies(0, 0):
            cp.start()

    @pl.when(active & new_expert)
    def _():
        slot = lax.rem(k, 2)

        @pl.when(k + 1 < nk_ref[0])
        def _():
            for cp in weight_copies(k + 1, 1 - slot):
                cp.start()

        for cp in weight_copies(k, slot):
            cp.wait()
        wgb[...] = stage_g[slot].astype(BF16)
        wub[...] = stage_u[slot].astype(BF16)
        wdb[...] = stage_d[slot].astype(BF16)

    @pl.when(active)
    def _():
        x = x_ref[...].astype(BF16)
        hg = _silu(_dot(x, wgb[...])) * _dot(x, wub[...])
        o_ref[...] = _dot((hg * w_ref[...]).astype(BF16), wdb[...])

    @pl.when(jnp.logical_not(active))
    def _():
        o_ref[...] = jnp.zeros_like(o_ref)


def _experts(x_sorted, w_sorted, tile_src, n_active, tile_pos, expert_seq, n_used, w_gate, w_up, w_down):
    rows = x_sorted.shape[0]
    _, d, ff = w_gate.shape
    tm = EXPERT_TILE
    hbm = pl.BlockSpec(memory_space=pl.ANY)
    grid_spec = pltpu.PrefetchScalarGridSpec(
        num_scalar_prefetch=5,
        grid=(rows // tm,),
        in_specs=[pl.BlockSpec((tm, d), lambda t, ts, *_: (ts[t], 0)),
                  pl.BlockSpec((tm, 1), lambda t, ts, *_: (ts[t], 0)),
                  hbm, hbm, hbm],
        out_specs=pl.BlockSpec((tm, d), lambda t, *_: (t, 0)),
        scratch_shapes=[pltpu.VMEM((d, ff), BF16), pltpu.VMEM((d, ff), BF16), pltpu.VMEM((ff, d), BF16),
                        pltpu.VMEM((2, d, ff), F32), pltpu.VMEM((2, d, ff), F32), pltpu.VMEM((2, ff, d), F32),
                        pltpu.SemaphoreType.DMA((3, 2))],
    )
    return pl.pallas_call(
        _expert_kernel,
        grid_spec=grid_spec,
        out_shape=jax.ShapeDtypeStruct((rows, d), F32),
        compiler_params=_cparams("arbitrary"),
        name="experts",
    )(tile_src, n_active, tile_pos, expert_seq, n_used, x_sorted, w_sorted, w_gate, w_up, w_down)


def _take_rows(a, idx):
    return a.at[idx].get(mode="promise_in_bounds")


def _step_lookup(table, starts, x):
    jumps = table[1:] - table[:-1]
    return table[0] + jnp.sum(jnp.where(x[None, :] >= starts[1:, None], jumps[:, None], 0), axis=0)


def _dispatch_plan(eid, ew, n_experts_total):
    n, top_k = eid.shape
    tm = EXPERT_TILE
    a = n * top_k
    n_tiles = a // tm + n_experts_total
    rows = n_tiles * tm
    i32 = jnp.int32
    q = jnp.arange(a, dtype=i32)
    sorted_e, order, sorted_w = lax.sort((eid.reshape(a), q, ew.reshape(a)), num_keys=1, is_stable=True)
    bounds = jnp.sum(sorted_e[None, :] < jnp.arange(n_experts_total + 1, dtype=i32)[:, None], axis=1, dtype=i32)
    off, end = bounds[:-1], bounds[1:]
    padded = (end - off + tm - 1) // tm * tm
    pad_end = jnp.cumsum(padded)
    pad_off = pad_end - padded
    delta = pad_off - off
    _, pos = lax.sort((order, q + _step_lookup(delta, off, q)), num_keys=1)
    n_active = (pad_end[-1] // tm).astype(i32)
    tile_src = jnp.minimum(jnp.arange(n_tiles, dtype=i32), n_active - 1)
    experts = jnp.arange(n_experts_total, dtype=i32)
    used = padded > 0
    n_used = jnp.sum(used, dtype=i32)
    expert_seq = jnp.minimum(jnp.sort(jnp.where(used, experts, n_experts_total)), n_experts_total - 1)
    tile_pos = _step_lookup(jnp.cumsum(used, dtype=i32) - 1, pad_off, tile_src * tm)
    p = jnp.arange(rows, dtype=i32)
    src = p - _step_lookup(delta, pad_off, p)
    valid = (src < _step_lookup(end, pad_off, p)) & (p < pad_end[-1])
    src = jnp.clip(src, 0, a - 1)
    row_token = jnp.where(valid, _take_rows(order, src) // top_k, p % n)
    row_w = jnp.where(valid, _take_rows(sorted_w, src), 0.0)
    return (row_token, row_w.reshape(rows, 1), pos.reshape(n, top_k), tile_src, n_active.reshape(1),
            tile_pos, expert_seq, n_used.reshape(1))


def _combine_kernel(n_prompt_tiles, x_ref, ya_ref, yb_ref, gr, gt, shr, sht, scr, sct, o_ref, hn_ref):
    is_s = pl.program_id(0) >= n_prompt_tiles
    x = x_ref[...] + _pick(is_s, gr, gt) * (ya_ref[...] + yb_ref[...])
    o_ref[...] = x
    hn_ref[...] = (_rms(x) * (1.0 + _pick(is_s, scr, sct)) + _pick(is_s, shr, sht)).astype(BF16)


def _combine_final_kernel(n_prompt_tiles, x_ref, ya_ref, yb_ref, gr, gt, fg_ref, op_ref, os_ref):
    is_s = pl.program_id(0) >= n_prompt_tiles
    x = x_ref[...] + _pick(is_s, gr, gt) * (ya_ref[...] + yb_ref[...])
    y = _rms(x) * fg_ref[...]

    @pl.when(jnp.logical_not(is_s))
    def _():
        op_ref[...] = y

    @pl.when(is_s)
    def _():
        os_ref[...] = y


def _combine(st, x, y_pairs, mod_row, mod_tok, layer, j_gate, final_gain, final):
    d = st.d
    tile = pl.BlockSpec((st.tm, d), lambda i: (i, 0))
    n_tiles = st.n_tiles
    common = [tile, tile, pl.BlockSpec((st.tm, d), lambda i: (n_tiles + i, 0)),
              *st.mod_specs(layer, d, lambda i: j_gate)]
    if final:
        return pl.pallas_call(
            functools.partial(_combine_final_kernel, st.n_prompt_tiles),
            grid=(st.n_tiles,),
            in_specs=[*common, pl.BlockSpec((1, d), lambda i: (0, 0))],
            out_specs=[pl.BlockSpec((st.tm, d), lambda i: (jnp.minimum(i, n_tiles - 2), 0)),
                       pl.BlockSpec((st.tm, d), lambda i: (0, 0))],
            out_shape=[jax.ShapeDtypeStruct((st.n_prompt, d), F32), jax.ShapeDtypeStruct((st.n_sample, d), F32)],
            compiler_params=_cparams("arbitrary"),
            name="moe_combine_final",
        )(x, y_pairs, y_pairs, mod_row, mod_tok, final_gain.reshape(1, d)), None
    return pl.pallas_call(
        functools.partial(_combine_kernel, st.n_prompt_tiles),
        grid=(st.n_tiles,),
        in_specs=[*common, *st.mod_specs(layer + 1, d, lambda i: 0), *st.mod_specs(layer + 1, d, lambda i: 1)],
        out_specs=[tile, tile],
        out_shape=[jax.ShapeDtypeStruct((st.n, d), F32), jax.ShapeDtypeStruct((st.n, d), BF16)],
        compiler_params=_cparams("parallel"),
        name="moe_combine",
    )(x, y_pairs, y_pairs, mod_row, mod_tok, mod_row, mod_tok, mod_row, mod_tok)


def kernel(x_prompt, x_sample, state_hgrn, state_ret, c_prompt, c_sample, ada_w, ada_b, hgrn_w_in, hgrn_lb_logits, hgrn_norm_gain, hgrn_w_out, ret_w_in, ret_norm_gain, ret_w_out, moe_router_group, moe_router_expert, moe_w_gate, moe_w_up, moe_w_down, final_norm_gain):
    batch, seq, d = x_prompt.shape
    dec_batch, dec_seq, _ = x_sample.shape
    depth = ada_w.shape[0]
    _, _, a_heads, a_kd, a_vd = state_hgrn.shape
    _, _, b_heads, b_dk, b_dv = state_ret.shape
    _, n_groups, _, n_exp = moe_router_expert.shape
    ff = moe_w_gate.shape[-1]
    n_exp_total = n_groups * n_exp
    st = _Stream(batch, seq, dec_batch, dec_seq, d)

    p = jax.nn.softmax(hgrn_lb_logits.astype(F32), axis=0)
    lb_all = jnp.cumsum(p, axis=0) - p[0:1]
    log_gamma = jnp.log1p(-jnp.exp2(-5.0 - jnp.arange(b_heads, dtype=F32)))
    cos_p, sin_p = _rope_tables(jnp.arange(seq, dtype=F32), b_dk // 2)
    cos_s, sin_s = _rope_tables(jnp.arange(dec_seq, dtype=F32) + PAST_LEN, b_dk // 2)
    w_gate = moe_w_gate.reshape(depth * n_exp_total, d, ff)
    w_up = moe_w_up.reshape(depth * n_exp_total, d, ff)
    w_down = moe_w_down.reshape(depth * n_exp_total, ff, d)

    c_rows = -(-batch // BF16_SUBLANES) * BF16_SUBLANES
    mod_row, mod_tok = _ada_mod(jnp.pad(c_prompt, ((0, c_rows - batch), (0, 0))),
                                jnp.repeat(c_sample, dec_seq, axis=0), ada_w, ada_b)
    mod_row = mod_row[:, :batch].reshape(depth * batch, 1, 6 * d)

    x = jnp.concatenate([x_prompt.reshape(-1, d), x_sample.reshape(-1, d)], 0)
    hg_p, rt_p, hg_s, rt_s = [], [], None, None
    hn = _norm_mod(st, x, mod_row, mod_tok, 0, 0, 1)
    for l in range(depth):
        j = l // 2
        if l % 2 == 0:
            proj = _proj(st, hn, hgrn_w_in, j)
            o_p, s_p, o_s, hg_s = _hgrn_mixer(st, proj, lb_all[j], hgrn_norm_gain[j], state_hgrn, j, hg_s)
            hg_p.append(s_p)
            w_out = hgrn_w_out
        else:
            n_qk, n_v = 2 * b_heads * b_dk, b_heads * b_dv
            qk = _proj(st, hn, ret_w_in, j, 0, n_qk)
            v = _proj(st, hn, ret_w_in, j, n_qk, n_v, BF16)
            g = _proj(st, hn, ret_w_in, j, n_qk + n_v, n_v)
            o_p, s_p, o_s, rt_s = _ret_mixer(st, qk, v, g, cos_p, sin_p, cos_s, sin_s, log_gamma,
                                             ret_norm_gain[j], state_ret, j, rt_s)
            rt_p.append(s_p)
            w_out = ret_w_out
        x = _proj_residual(st, o_p, o_s, w_out, j, x, mod_row, mod_tok, l, 2)

        w_router = jnp.concatenate(
            [moe_router_group[l], jnp.moveaxis(moe_router_expert[l], 0, 1).reshape(d, n_exp_total)], 1)
        w_router = jnp.pad(w_router, ((0, 0), (0, LANES - w_router.shape[1])))
        hx, eid, ew = _norm_mod_router(st, x, mod_row, mod_tok, l, 3, 4, w_router, n_groups, n_exp)
        row_token, row_w, pos, tile_src, n_active, tile_pos, expert_seq, n_used = _dispatch_plan(
            eid, ew, n_exp_total)
        y_sorted = _experts(_take_rows(hx, row_token), row_w, tile_src, n_active, tile_pos,
                            expert_seq + l * n_exp_total, n_used, w_gate, w_up, w_down)
        y_pairs = _take_rows(y_sorted, pos.T.reshape(-1))
        x, hn = _combine(st, x, y_pairs, mod_row, mod_tok, l, 5, final_norm_gain, l == depth - 1)

    y_prompt = x[0].reshape(batch, seq, d)
    y_sample = x[1].reshape(dec_batch, dec_seq, d)
    return (y_prompt, y_sample, jnp.stack(hg_p), jnp.stack(rt_p), hg_s, rt_s)
```
